```python
import math
import jax, jax.numpy as jnp
from jax import lax
import numpy as np

D_MODEL = 2048
BATCH = 2
SEQ = 4096
DEPTH = 2

HEAD_DIM = 128
MIX_WIDTH = D_MODEL
N_HEADS = MIX_WIDTH // HEAD_DIM
N_HEADS_A = N_HEADS // 2
N_HEADS_B = N_HEADS - N_HEADS_A
WIDTH_A = N_HEADS_A * HEAD_DIM
WIDTH_B = N_HEADS_B * HEAD_DIM
DIFF_DIM = HEAD_DIM // 2
PROJ_WIDTH = 3 * WIDTH_A + 3 * WIDTH_B
DILATED_PATTERNS = ((128, 1), (512, 4), (2048, 16))
DILATED_BLOCK = 64
N_REL_BUCKETS = 32
REL_MAX_DISTANCE = 128
Q_BLOCK = 128
N_GROUPS = 4
EXPERTS_PER_GROUP = 8
N_EXPERTS = N_GROUPS * EXPERTS_PER_GROUP
EXPERT_FFN = D_MODEL // 2
TOP_K_INNER = 2
MOE_BLOCK = 256
RMS_EPS = 1e-6
SUBLN_EPS = 1e-5
NEG_BIG = -1e30

kernel_name = 'hybrid_dilated_diffattn_hmoe_encoder'


def rmsnorm(x, g, eps=RMS_EPS):
    xf = x.astype(jnp.float32)
    y = xf * lax.rsqrt(jnp.mean(xf * xf, axis=-1, keepdims=True) + eps)
    return (y * g.astype(jnp.float32)).astype(x.dtype)


def t5_bucket(rel):
    nb = N_REL_BUCKETS // 2
    max_exact = nb // 2
    n = -rel
    ret = jnp.where(n < 0, nb, 0)
    n = jnp.abs(n)
    nf = jnp.maximum(n, 1).astype(jnp.float32)
    large = max_exact + (jnp.log(nf / max_exact) / math.log(REL_MAX_DISTANCE / max_exact) * (nb - max_exact)).astype(jnp.int32)
    large = jnp.minimum(large, nb - 1)
    return (ret + jnp.where(n < max_exact, n, large)).astype(jnp.int32)


def dilated_window_attention(q, k, v, bias_table, window, dilation):
    b, s, h, dh = q.shape
    radius = window // (2 * dilation)
    blk = DILATED_BLOCK
    L = s // dilation
    nb = -(-L // blk)
    Lp = nb * blk

    def strided(t):
        t = t.reshape(b, L, dilation, h, dh).transpose(0, 2, 3, 1, 4)
        return jnp.pad(t, ((0, 0), (0, 0), (0, 0), (0, Lp - L), (0, 0)))

    def bands(t):
        t = jnp.pad(strided(t), ((0, 0), (0, 0), (0, 0), (blk, blk), (0, 0)))
        return jnp.concatenate([t[:, :, :, o * blk:o * blk + Lp, :].reshape(b, dilation, h, nb, blk, dh) for o in range(3)], axis=-2)

    qs = strided(q).reshape(b, dilation, h, nb, blk, dh)
    ks, vs = bands(k), bands(v)
    t_idx = jnp.arange(blk)
    s_idx = jnp.arange(3 * blk)
    dm = s_idx[None, :] - blk - t_idx[:, None]
    key_m = jnp.arange(nb)[:, None, None] * blk - blk + s_idx[None, None, :]
    valid = (jnp.abs(dm) <= radius)[None] & (key_m >= 0) & (key_m < L)
    bias = bias_table[t5_bucket(dm * dilation)].transpose(2, 0, 1).astype(jnp.float32)
    scores = jnp.einsum('bzhnqd,bzhnkd->bzhnqk', qs, ks).astype(jnp.float32) * (dh ** -0.5) + bias[None, None, :, None]
    scores = jnp.where(valid[None, None, None], scores, NEG_BIG)
    m = jnp.max(scores, axis=-1, keepdims=True)
    p = jnp.exp(scores - m)
    den = jnp.sum(p, axis=-1)
    out = jnp.einsum('bzhnqk,bzhnkd->bzhnqd', p.astype(v.dtype), vs).astype(jnp.float32) / den[..., None]
    lse = m[..., 0] + jnp.log(den)
    out = out.reshape(b, dilation, h, Lp, dh)[:, :, :, :L].transpose(0, 3, 1, 2, 4).reshape(b, s, h, dh)
    lse = lse.reshape(b, dilation, h, Lp)[..., :L].transpose(0, 3, 1, 2).reshape(b, s, h)
    return out, lse


def dilated_mixture(q, k, v, bias_table):
    res = [dilated_window_attention(q, k, v, bias_table, w, d) for (w, d) in DILATED_PATTERNS]
    outs = jnp.stack([r[0] for r in res], axis=0)
    lses = jnp.stack([r[1] for r in res], axis=0)
    wts = jax.nn.softmax(lses, axis=0)
    return jnp.sum(wts[..., None] * outs, axis=0).astype(q.dtype)


def differential_attention(q1, q2, k1, k2, v, bias_table, lam):
    b, s, h, dd = q1.shape
    nq = s // Q_BLOCK
    kpos = jnp.arange(s)

    def to_blocks(t):
        return t.reshape(b, nq, Q_BLOCK, h, dd).transpose(1, 0, 2, 3, 4)

    def one_block(args):
        q1b, q2b, i = args
        qpos = i * Q_BLOCK + jnp.arange(Q_BLOCK)
        bias = bias_table[t5_bucket(kpos[None, :] - qpos[:, None])].transpose(2, 0, 1).astype(jnp.float32)
        s1 = jnp.einsum('bqhd,bkhd->bhqk', q1b, k1).astype(jnp.float32) * (dd ** -0.5) + bias
        s2 = jnp.einsum('bqhd,bkhd->bhqk', q2b, k2).astype(jnp.float32) * (dd ** -0.5) + bias
        a = jax.nn.softmax(s1, axis=-1) - lam * jax.nn.softmax(s2, axis=-1)
        return jnp.einsum('bhqk,bkhd->bqhd', a.astype(v.dtype), v)

    out = lax.map(one_block, (to_blocks(q1), to_blocks(q2), jnp.arange(nq)))
    return out.transpose(1, 0, 2, 3, 4).reshape(b, s, h, v.shape[-1])


def hierarchical_moe(h, w_rg, w_re, w_gate, w_up, w_down):
    b, s, d = h.shape
    t = h.reshape(b * s, d)
    n_tok = b * s
    g_prob = jax.nn.softmax((t @ w_rg).astype(jnp.float32), axis=-1)
    grp = jnp.argmax(g_prob, axis=-1).astype(jnp.int32)
    p_grp = jnp.max(g_prob, axis=-1)
    e_logits = jnp.einsum('td,gde->tge', t, w_re).astype(jnp.float32)
    e_logits = jnp.take_along_axis(e_logits, grp[:, None, None], axis=1)[:, 0]
    top_val, top_idx = lax.top_k(e_logits, TOP_K_INNER)
    gate = jax.nn.softmax(top_val, axis=-1) * p_grp[:, None]
    e_flat = (grp[:, None] * EXPERTS_PER_GROUP + top_idx).reshape(-1).astype(jnp.int32)
    w_flat = gate.reshape(-1)
    tok_flat = jnp.repeat(jnp.arange(n_tok, dtype=jnp.int32), TOP_K_INNER)
    n_assign = n_tok * TOP_K_INNER
    order = jnp.argsort(e_flat)
    e_s = e_flat[order]
    counts = jnp.bincount(e_flat, length=N_EXPERTS)
    padded = (counts + MOE_BLOCK - 1) // MOE_BLOCK * MOE_BLOCK
    pend = jnp.cumsum(padded)
    pstart = pend - padded
    start = jnp.cumsum(counts) - counts
    dest = pstart[e_s] + jnp.arange(n_assign, dtype=jnp.int32) - start[e_s]
    n_blocks = -(-n_assign // MOE_BLOCK) + N_EXPERTS
    rows = n_blocks * MOE_BLOCK
    row_tok = jnp.zeros((rows,), jnp.int32).at[dest].set(tok_flat[order])
    row_w = jnp.zeros((rows,), jnp.float32).at[dest].set(w_flat[order])
    blk_exp = jnp.minimum(jnp.searchsorted(pend, jnp.arange(n_blocks) * MOE_BLOCK, side='right'), N_EXPERTS - 1).astype(jnp.int32)
    xb = t[row_tok].reshape(n_blocks, MOE_BLOCK, d)

    def run_expert(args):
        xe, e = args
        hid = jax.nn.silu(xe @ w_gate[e]) * (xe @ w_up[e])
        return hid @ w_down[e]

    yb = lax.map(run_expert, (xb, blk_exp))
    y = yb.reshape(rows, d) * row_w[:, None].astype(t.dtype)
    out = jnp.zeros_like(t).at[row_tok].add(y)
    return out.reshape(b, s, d)


def setup_inputs(seed: int = 0) -> dict:
    key = jax.random.key(seed)
    ks = jax.random.split(key, 20)
    f32 = jnp.float32
    nrm = lambda k, shp, sc: jax.random.normal(k, shp, f32) * sc
    return {
        'x': nrm(ks[0], (BATCH, SEQ, D_MODEL), 1.0),
        'rel_bias': nrm(ks[1], (N_REL_BUCKETS, N_HEADS), 0.5),
        'norm_attn': 1.0 + nrm(ks[2], (DEPTH, D_MODEL), 0.05),
        'w_in': nrm(ks[3], (DEPTH, D_MODEL, PROJ_WIDTH), D_MODEL ** -0.5),
        'gain_a': 1.0 + nrm(ks[4], (DEPTH, N_HEADS_A, HEAD_DIM), 0.05),
        'gain_b': 1.0 + nrm(ks[5], (DEPTH, N_HEADS_B, HEAD_DIM), 0.05),
        'lam_q1': nrm(ks[6], (DEPTH, DIFF_DIM), 0.1),
        'lam_k1': nrm(ks[7], (DEPTH, DIFF_DIM), 0.1),
        'lam_q2': nrm(ks[8], (DEPTH, DIFF_DIM), 0.1),
        'lam_k2': nrm(ks[9], (DEPTH, DIFF_DIM), 0.1),
        'w_out': nrm(ks[10], (DEPTH, MIX_WIDTH, D_MODEL), MIX_WIDTH ** -0.5),
        'norm_ffn': 1.0 + nrm(ks[11], (DEPTH, D_MODEL), 0.05),
        'w_router_group': nrm(ks[12], (DEPTH, D_MODEL, N_GROUPS), D_MODEL ** -0.5),
        'w_router_expert': nrm(ks[13], (DEPTH, N_GROUPS, D_MODEL, EXPERTS_PER_GROUP), D_MODEL ** -0.5),
        'w_gate': nrm(ks[14], (DEPTH, N_EXPERTS, D_MODEL, EXPERT_FFN), D_MODEL ** -0.5),
        'w_up': nrm(ks[15], (DEPTH, N_EXPERTS, D_MODEL, EXPERT_FFN), D_MODEL ** -0.5),
        'w_down': nrm(ks[16], (DEPTH, N_EXPERTS, EXPERT_FFN, D_MODEL), EXPERT_FFN ** -0.5),
        'norm_final': 1.0 + nrm(ks[17], (D_MODEL,), 0.05),
    }


def reference(x, rel_bias, norm_attn, w_in, gain_a, gain_b, lam_q1, lam_k1, lam_q2, lam_k2, w_out, norm_ffn,
              w_router_group, w_router_expert, w_gate, w_up, w_down, norm_final):
    b, s, _ = x.shape
    bias_a = rel_bias[:, :N_HEADS_A]
    bias_b = rel_bias[:, N_HEADS_A:]
    cuts = [WIDTH_A, 2 * WIDTH_A, 3 * WIDTH_A, 3 * WIDTH_A + WIDTH_B, 3 * WIDTH_A + 2 * WIDTH_B]
    for l in range(DEPTH):
        h = rmsnorm(x, norm_attn[l])
        proj = jnp.einsum('bsd,de->bse', h, w_in[l])
        qa, ka, va, qb, kb, vb = jnp.split(proj, cuts, axis=-1)
        heads_a = lambda t: t.reshape(b, s, N_HEADS_A, HEAD_DIM)
        out_a = dilated_mixture(heads_a(qa), heads_a(ka), heads_a(va), bias_a)
        out_a = rmsnorm(out_a, gain_a[l])
        qb = qb.reshape(b, s, N_HEADS_B, 2, DIFF_DIM)
        kb = kb.reshape(b, s, N_HEADS_B, 2, DIFF_DIM)
        vb = vb.reshape(b, s, N_HEADS_B, HEAD_DIM)
        lam_init = 0.8 - 0.6 * math.exp(-0.3 * l)
        lam = (jnp.exp(jnp.sum(lam_q1[l].astype(jnp.float32) * lam_k1[l].astype(jnp.float32)))
               - jnp.exp(jnp.sum(lam_q2[l].astype(jnp.float32) * lam_k2[l].astype(jnp.float32))) + lam_init)
        out_b = differential_attention(qb[:, :, :, 0], qb[:, :, :, 1], kb[:, :, :, 0], kb[:, :, :, 1], vb, bias_b, lam)
        out_b = rmsnorm(out_b, gain_b[l], SUBLN_EPS) * (1.0 - lam_init)
        mix = jnp.concatenate([out_a.reshape(b, s, WIDTH_A), out_b.reshape(b, s, WIDTH_B)], axis=-1)
        x = x + jnp.einsum('bse,ed->bsd', mix, w_out[l])
        h = rmsnorm(x, norm_ffn[l])
        x = x + hierarchical_moe(h, w_router_group[l], w_router_expert[l], w_gate[l], w_up[l], w_down[l])
    return rmsnorm(x, norm_final)
```

```python
import functools
import math

import jax
import jax.numpy as jnp
from jax import lax
from jax.experimental import pallas as pl
from jax.experimental.pallas import tpu as pltpu

F32 = jnp.float32
BF16 = jnp.bfloat16

LANES = 128
HEAD_DIM = 128
DIFF_DIM = HEAD_DIM // 2
DILATED_PATTERNS = ((128, 1), (512, 4), (2048, 16))
DILATED_HALO = 64
N_REL_BUCKETS = 32
REL_MAX_DISTANCE = 128
N_GROUPS = 4
EXPERTS_PER_GROUP = 8
N_EXPERTS = N_GROUPS * EXPERTS_PER_GROUP
MOE_BLOCK = 256
RMS_EPS = 1e-6
SUBLN_EPS = 1e-5
NEG_BIG = -1e30
VMEM_LIMIT = 56 * 1024 * 1024

_NT = (((1,), (1,)), ((), ()))


def _t5_bucket(rel):
    nb = N_REL_BUCKETS // 2
    max_exact = nb // 2
    n = -rel
    ret = jnp.where(n < 0, nb, 0)
    n = jnp.abs(n)
    nf = jnp.maximum(n, 1).astype(jnp.float32)
    large = max_exact + (jnp.log(nf / max_exact) / math.log(REL_MAX_DISTANCE / max_exact)
                         * (nb - max_exact)).astype(jnp.int32)
    large = jnp.minimum(large, nb - 1)
    return (ret + jnp.where(n < max_exact, n, large)).astype(jnp.int32)


def _params(*sem):
    return pltpu.CompilerParams(dimension_semantics=sem, vmem_limit_bytes=VMEM_LIMIT)


def _norm_proj_kernel(*refs, has_y, eps):
    if has_y:
        x_ref, ya_ref, yb_ref, g_ref, w_ref, o_ref, xs_ref, xn_ref = refs
    else:
        x_ref, g_ref, w_ref, o_ref, xn_ref = refs

    @pl.when(pl.program_id(1) == 0)
    def _():
        x = x_ref[...]
        if has_y:
            x = x + ya_ref[...] + yb_ref[...]
            xs_ref[...] = x
        ms = jnp.mean(x * x, axis=-1, keepdims=True)
        xn_ref[...] = (x * lax.rsqrt(ms + eps) * g_ref[...]).astype(BF16)

    o_ref[...] = jnp.dot(xn_ref[...], w_ref[...], preferred_element_type=F32).astype(o_ref.dtype)


def _norm_proj(x, y2, g, w, *, tm=512, tn=1024):
    t, d = x.shape
    n = w.shape[1]
    has_y = y2 is not None
    row = lambda i, j: (i, 0)
    in_specs = [pl.BlockSpec((tm, d), row)]
    args = [x]
    if has_y:
        in_specs += [pl.BlockSpec((tm, d), row), pl.BlockSpec((tm, d), lambda i, j: (i, 1))]
        args += [y2, y2]
    in_specs += [pl.BlockSpec((1, d), lambda i, j: (0, 0)), pl.BlockSpec((d, tn), lambda i, j: (0, j))]
    args += [g.reshape(1, d), w]
    out_shape = [jax.ShapeDtypeStruct((t, n), BF16)]
    out_specs = [pl.BlockSpec((tm, tn), lambda i, j: (i, j))]
    if has_y:
        out_shape.append(jax.ShapeDtypeStruct((t, d), F32))
        out_specs.append(pl.BlockSpec((tm, d), row))
    res = pl.pallas_call(
        functools.partial(_norm_proj_kernel, has_y=has_y, eps=RMS_EPS),
        grid=(t // tm, n // tn),
        in_specs=in_specs, out_specs=out_specs, out_shape=out_shape,
        scratch_shapes=[pltpu.VMEM((tm, d), BF16)],
        compiler_params=_params("arbitrary", "arbitrary"),
    )(*args)
    return (res[0], res[1]) if has_y else (res[0], x)


def _dilated_kernel(q_ref, kp_ref, kc_ref, kn_ref, vp_ref, vc_ref, vn_ref, bias_ref, o_ref, lse_ref,
                    *, tq, n_heads, seq_sub, scale):
    i = pl.program_id(2)
    halo = DILATED_HALO
    win = tq + 2 * halo
    key_m = i * tq - halo + lax.broadcasted_iota(jnp.int32, (tq, win), 1)
    in_range = (key_m >= 0) & (key_m < seq_sub)
    lane = lax.broadcasted_iota(jnp.int32, (tq, LANES), 1)
    lse_all = jnp.zeros((tq, LANES), F32)
    for h in range(n_heads):
        cs = slice(h * HEAD_DIM, (h + 1) * HEAD_DIM)
        q = q_ref[:, cs]
        k = jnp.concatenate([kp_ref[tq - halo:, cs], kc_ref[:, cs], kn_ref[:halo, cs]], axis=0)
        v = jnp.concatenate([vp_ref[tq - halo:, cs], vc_ref[:, cs], vn_ref[:halo, cs]], axis=0)
        s = lax.dot_general(q, k, _NT, preferred_element_type=F32) * scale + bias_ref[h]
        s = jnp.where(in_range, s, NEG_BIG)
        m = jnp.max(s, axis=-1, keepdims=True)
        p = jnp.exp(s - m)
        den = jnp.sum(p, axis=-1, keepdims=True)
        o_ref[:, cs] = jnp.dot(p.astype(BF16), v, preferred_element_type=F32) / den
        lse_all = jnp.where(lane == h, m + jnp.log(den), lse_all)
    lse_ref[...] = lse_all


def _dilated_bias(bias_table, n_heads, window, dilation, tq):
    radius = window // (2 * dilation)
    assert radius <= DILATED_HALO
    win = tq + 2 * DILATED_HALO
    dm = jnp.arange(win)[None, :] - DILATED_HALO - jnp.arange(tq)[:, None]
    b = bias_table[_t5_bucket(dm * dilation)][:, :, :n_heads].astype(F32)
    b = jnp.where((jnp.abs(dm) <= radius)[:, :, None], b, NEG_BIG)
    return b.transpose(2, 0, 1)


def _dilated_attention(proj, bias_table, batch, seq, n_heads, window, dilation, *, tq=128):
    t, pw = proj.shape
    wa = n_heads * HEAD_DIM
    nblk = pw // wa
    sub = seq // dilation
    nq = sub // tq
    view = proj.reshape(t // dilation, dilation * pw)
    bias = _dilated_bias(bias_table, n_heads, window, dilation, tq)

    def spec(which, shift):
        def imap(b, r, i):
            return (b * nq + jnp.clip(i + shift, 0, nq - 1), r * nblk + which)
        return pl.BlockSpec((tq, wa), imap)

    out, lse = pl.pallas_call(
        functools.partial(_dilated_kernel, tq=tq, n_heads=n_heads, seq_sub=sub, scale=HEAD_DIM ** -0.5),
        grid=(batch, dilation, nq),
        in_specs=[spec(0, 0), spec(1, -1), spec(1, 0), spec(1, 1), spec(2, -1), spec(2, 0), spec(2, 1),
                  pl.BlockSpec(bias.shape, lambda b, r, i: (0, 0, 0))],
        out_specs=[pl.BlockSpec((tq, wa), lambda b, r, i: (b * nq + i, r)),
                   pl.BlockSpec((tq, LANES), lambda b, r, i: (b * nq + i, r))],
        out_shape=[jax.ShapeDtypeStruct((t // dilation, dilation * wa), F32),
                   jax.ShapeDtypeStruct((t // dilation, dilation * LANES), F32)],
        compiler_params=_params("arbitrary", "arbitrary", "arbitrary"),
    )(view, view, view, view, view, view, view, bias)
    return out.reshape(t, wa), lse.reshape(t, LANES)


def _mix_kernel(o0_ref, o1_ref, o2_ref, l0_ref, l1_ref, l2_ref, g_ref, out_ref, *, n_heads, eps):
    ls = [l0_ref[...], l1_ref[...], l2_ref[...]]
    top = jnp.maximum(jnp.maximum(ls[0], ls[1]), ls[2])
    es = [jnp.exp(l - top) for l in ls]
    tot = es[0] + es[1] + es[2]
    ws = [e / tot for e in es]
    o_refs = (o0_ref, o1_ref, o2_ref)
    for h in range(n_heads):
        cs = slice(h * HEAD_DIM, (h + 1) * HEAD_DIM)
        mix = ws[0][:, h:h + 1] * o_refs[0][:, cs]
        for p in (1, 2):
            mix = mix + ws[p][:, h:h + 1] * o_refs[p][:, cs]
        ms = jnp.mean(mix * mix, axis=-1, keepdims=True)
        out_ref[:, cs] = (mix * lax.rsqrt(ms + eps) * g_ref[:, cs]).astype(out_ref.dtype)


def _dilated_mixture(outs, lses, gain, *, tm=512):
    t, wa = outs[0].shape
    n_heads = wa // HEAD_DIM
    row = lambda i: (i, 0)
    return pl.pallas_call(
        functools.partial(_mix_kernel, n_heads=n_heads, eps=RMS_EPS),
        grid=(t // tm,),
        in_specs=[pl.BlockSpec((tm, wa), row)] * 3 + [pl.BlockSpec((tm, LANES), row)] * 3
                 + [pl.BlockSpec((1, wa), lambda i: (0, 0))],
        out_specs=pl.BlockSpec((tm, wa), row),
        out_shape=jax.ShapeDtypeStruct((t, wa), BF16),
        compiler_params=_params("arbitrary"),
    )(*outs, *lses, gain.reshape(1, wa))


def _diff_attn_kernel(lam_ref, far_ref, q_ref, k_ref, v_ref, bias_ref, g_ref, o_ref,
                      m1_ref, l1_ref, a1_ref, m2_ref, l2_ref, a2_ref, *, tile, n_tiles, eps, post_scale):
    h = pl.program_id(1)
    i = pl.program_id(2)
    lane = lax.broadcasted_iota(jnp.int32, (tile, HEAD_DIM), 1)
    qs = q_ref[...] * (DIFF_DIM ** -0.5)
    zero = jnp.zeros_like(qs)
    q_maps = (jnp.where(lane < DIFF_DIM, qs, zero), jnp.where(lane >= DIFF_DIM, qs, zero))
    states = ((m1_ref, l1_ref, a1_ref), (m2_ref, l2_ref, a2_ref))
    for m_ref, l_ref, a_ref in states:
        m_ref[...] = jnp.full(m_ref.shape, NEG_BIG, F32)
        l_ref[...] = jnp.zeros(l_ref.shape, F32)
        a_ref[...] = jnp.zeros(a_ref.shape, F32)

    def visit(j, bias_tile, const):
        start = pl.multiple_of(j * tile, tile)
        ks = k_ref[pl.ds(start, tile), :]
        vs = v_ref[pl.ds(start, tile), :]
        for qm, (m_ref, l_ref, a_ref) in zip(q_maps, states):
            s = lax.dot_general(qm, ks, _NT, preferred_element_type=F32)
            if bias_tile is not None:
                s = s + bias_tile
            m_prev = m_ref[...]
            row_max = jnp.max(s, axis=-1, keepdims=True)
            if const is not None:
                row_max = row_max + const
            m_next = jnp.maximum(m_prev, row_max)
            shift = m_next[:, :1]
            if const is not None:
                shift = shift - const
            p = jnp.exp(s - shift)
            alpha = jnp.exp(m_prev - m_next)
            l_ref[...] = alpha * l_ref[...] + jnp.sum(p, axis=-1, keepdims=True)
            a_ref[...] = alpha * a_ref[...] + jnp.dot(p.astype(BF16), vs, preferred_element_type=F32)
            m_ref[...] = m_next

    def far(const):
        def body(j, carry):
            visit(j, None, const)
            return carry
        return body

    lax.fori_loop(0, jnp.maximum(i - 1, 0), far(far_ref[h, 0]), 0)

    @pl.when(i >= 1)
    def _():
        visit(i - 1, bias_ref[0, 0], None)

    visit(i, bias_ref[0, 1], None)

    @pl.when(i + 1 < n_tiles)
    def _():
        visit(i + 1, bias_ref[0, 2], None)

    lax.fori_loop(i + 2, n_tiles, far(far_ref[h, 1]), 0)

    o = a1_ref[...] / l1_ref[...] - lam_ref[0] * (a2_ref[...] / l2_ref[...])
    ms = jnp.mean(o * o, axis=-1, keepdims=True)
    o_ref[...] = ((o * lax.rsqrt(ms + eps) * g_ref[0]) * post_scale).astype(o_ref.dtype)


def _diff_attention(proj, bias_table, gain, lam, lam_init, batch, seq, n_heads_a, n_heads_b, *, tile=256):
    t, pw = proj.shape
    n_tiles = seq // tile
    q0 = 3 * n_heads_a
    k0 = q0 + n_heads_b
    v0 = k0 + n_heads_b
    rel = (jnp.arange(3)[:, None, None] - 1) * tile + jnp.arange(tile)[None, None, :] - jnp.arange(tile)[None, :, None]
    near = bias_table[_t5_bucket(rel)][..., n_heads_a:].astype(F32).transpose(3, 0, 1, 2)
    far = bias_table[_t5_bucket(jnp.array([-seq, seq]))][:, n_heads_a:].astype(F32).T
    smem = pl.BlockSpec(memory_space=pltpu.SMEM)
    return pl.pallas_call(
        functools.partial(_diff_attn_kernel, tile=tile, n_tiles=n_tiles, eps=SUBLN_EPS,
                          post_scale=1.0 - lam_init),
        grid=(batch, n_heads_b, n_tiles),
        in_specs=[smem, smem,
                  pl.BlockSpec((tile, HEAD_DIM), lambda b, h, i: (b * n_tiles + i, q0 + h)),
                  pl.BlockSpec((seq, HEAD_DIM), lambda b, h, i: (b, k0 + h)),
                  pl.BlockSpec((seq, HEAD_DIM), lambda b, h, i: (b, v0 + h)),
                  pl.BlockSpec((1, 3, tile, tile), lambda b, h, i: (h, 0, 0, 0)),
                  pl.BlockSpec((1, 1, HEAD_DIM), lambda b, h, i: (h, 0, 0))],
        out_specs=pl.BlockSpec((tile, HEAD_DIM), lambda b, h, i: (b * n_tiles + i, h)),
        out_shape=jax.ShapeDtypeStruct((t, n_heads_b * HEAD_DIM), BF16),
        scratch_shapes=[pltpu.VMEM((tile, HEAD_DIM), F32)] * 6,
        compiler_params=_params("arbitrary", "arbitrary", "arbitrary"),
    )(lam.reshape(1), far, proj, proj, proj, near, gain.reshape(n_heads_b, 1, HEAD_DIM))


def _out_router_kernel(a_ref, b_ref, x_ref, wo_ref, g_ref, wrh_ref, wrl_ref,
                       x1_ref, hn_ref, route_ref, cnt_ref, *, tm, wa, eps):
    @pl.when(pl.program_id(0) == 0)
    def _():
        cnt_ref[...] = jnp.zeros(cnt_ref.shape, F32)

    y = jnp.dot(a_ref[...], wo_ref[:wa, :], preferred_element_type=F32)
    y = y + jnp.dot(b_ref[...], wo_ref[wa:, :], preferred_element_type=F32)
    x1 = x_ref[...] + y
    x1_ref[...] = x1
    ms = jnp.mean(x1 * x1, axis=-1, keepdims=True)
    hn = x1 * lax.rsqrt(ms + eps) * g_ref[...]
    hn_ref[...] = hn

    hi = hn.astype(BF16)
    lo = (hn - hi.astype(F32)).astype(BF16)
    logits = (jnp.dot(hi, wrh_ref[...], preferred_element_type=F32)
              + jnp.dot(lo, wrh_ref[...], preferred_element_type=F32)
              + jnp.dot(hi, wrl_ref[...], preferred_element_type=F32))

    lane = lax.broadcasted_iota(jnp.int32, (tm, LANES), 1)
    neg_inf = jnp.float32(-jnp.inf)

    def first_max(vals):
        top = jnp.max(vals, axis=-1, keepdims=True)
        idx = jnp.min(jnp.where(vals == top, lane, LANES), axis=-1, keepdims=True)
        return top, idx

    is_grp = lane < N_GROUPS
    g_top, g_idx = first_max(jnp.where(is_grp, logits, neg_inf))
    g_sum = jnp.sum(jnp.where(is_grp, jnp.exp(logits - g_top), 0.0), axis=-1, keepdims=True)
    p_grp = 1.0 / g_sum
    e_lo = N_GROUPS + EXPERTS_PER_GROUP * g_idx
    e_vals = jnp.where((lane >= e_lo) & (lane < e_lo + EXPERTS_PER_GROUP), logits, neg_inf)
    top1, i1 = first_max(e_vals)
    top2, i2 = first_max(jnp.where(lane == i1, neg_inf, e_vals))
    b2 = jnp.exp(top2 - top1)
    w0 = p_grp / (1.0 + b2)
    w1 = p_grp * b2 / (1.0 + b2)

    oh0 = (lane == i1).astype(F32)
    oh1 = (lane == i2).astype(F32)
    both = oh0 + oh1
    r_i = lax.broadcasted_iota(jnp.int32, (tm, tm), 0)
    c_i = lax.broadcasted_iota(jnp.int32, (tm, tm), 1)
    before = jnp.where(r_i > c_i, 1.0, 0.0).astype(BF16)
    base = cnt_ref[...] + jnp.dot(before, both.astype(BF16), preferred_element_type=F32)
    rank0 = jnp.sum(base * oh0, axis=-1, keepdims=True)
    rank1 = jnp.sum(base * oh1, axis=-1, keepdims=True)
    cnt_ref[...] = cnt_ref[...] + jnp.sum(both, axis=0, keepdims=True)

    fields = ((i1 - N_GROUPS).astype(F32), (i2 - N_GROUPS).astype(F32), w0, w1, rank0, rank1)
    route = jnp.zeros((tm, LANES), F32)
    for n, f in enumerate(fields):
        route = jnp.where(lane == n, f, route)
    route_ref[...] = route


def _out_router(out_a, out_b, x, w_out, g, wr_hi, wr_lo, *, tm=256):
    t, d = x.shape
    wa = out_a.shape[1]
    wb = out_b.shape[1]
    row = lambda i: (i, 0)
    const = lambda i: (0, 0)
    return pl.pallas_call(
        functools.partial(_out_router_kernel, tm=tm, wa=wa, eps=RMS_EPS),
        grid=(t // tm,),
        in_specs=[pl.BlockSpec((tm, wa), row), pl.BlockSpec((tm, wb), row), pl.BlockSpec((tm, d), row),
                  pl.BlockSpec((wa + wb, d), const), pl.BlockSpec((1, d), const),
                  pl.BlockSpec((d, LANES), const), pl.BlockSpec((d, LANES), const)],
        out_specs=[pl.BlockSpec((tm, d), row), pl.BlockSpec((tm, d), row),
                   pl.BlockSpec((tm, LANES), row), pl.BlockSpec((1, LANES), const)],
        out_shape=[jax.ShapeDtypeStruct((t, d), F32), jax.ShapeDtypeStruct((t, d), F32),
                   jax.ShapeDtypeStruct((t, LANES), F32), jax.ShapeDtypeStruct((1, LANES), F32)],
        compiler_params=_params("arbitrary"),
    )(out_a, out_b, x, w_out, g.reshape(1, d), wr_hi, wr_lo)


def _moe_kernel(nused_ref, bexp_ref, src_ref, nxt_ref, rw_ref, hn_hbm, wg_ref, wu_ref, wd_ref, out_hbm,
                xbuf, ybuf, gsem, ssem, *, rows, n_tok):
    del bexp_ref
    b = pl.program_id(0)
    n_used = nused_ref[0]
    slot = lax.rem(b, 2)

    def start_gather(idx_ref, s):
        def body(r, carry):
            tok = lax.shift_right_logical(jnp.maximum(idx_ref[0, 0, r], 0), 1)
            pltpu.make_async_copy(hn_hbm.at[pl.ds(tok, 1)], xbuf.at[s, pl.ds(r, 1)], gsem.at[s]).start()
            return carry
        lax.fori_loop(0, rows, body, 0)

    def wait_gather(s):
        pltpu.make_async_copy(hn_hbm.at[pl.ds(0, rows)], xbuf.at[s], gsem.at[s]).wait()

    def start_scatter(s):
        def body(r, carry):
            v = src_ref[0, 0, r]
            dst = jnp.where(v >= 0, v, 2 * n_tok + s * rows + r)
            pltpu.make_async_copy(ybuf.at[s, pl.ds(r, 1)], out_hbm.at[pl.ds(dst, 1)], ssem.at[s]).start()
            return carry
        lax.fori_loop(0, rows, body, 0)

    def wait_scatter(s):
        pltpu.make_async_copy(ybuf.at[s], out_hbm.at[pl.ds(0, rows)], ssem.at[s]).wait()

    @pl.when(b < n_used)
    def _():
        @pl.when(b == 0)
        def _():
            ybuf[1] = jnp.zeros(ybuf.shape[1:], F32)
            for s in range(2):
                dump = pltpu.make_async_copy(ybuf.at[1], out_hbm.at[pl.ds(2 * n_tok + s * rows, rows)], ssem.at[1])
                dump.start()
                dump.wait()
            start_gather(src_ref, slot)

        @pl.when(b + 1 < n_used)
        def _():
            start_gather(nxt_ref, 1 - slot)

        wait_gather(slot)

        @pl.when(b >= 2)
        def _():
            wait_scatter(slot)

        xb = xbuf[slot].astype(BF16)
        gate = jnp.dot(xb, wg_ref[0], preferred_element_type=F32)
        up = jnp.dot(xb, wu_ref[0], preferred_element_type=F32)
        hid = (gate * (1.0 / (1.0 + jnp.exp(-gate))) * up).astype(BF16)
        y = jnp.dot(hid, wd_ref[0], preferred_element_type=F32)
        ybuf[slot] = y * rw_ref[...]
        start_scatter(slot)

        @pl.when(b + 1 >= n_used)
        def _():
            @pl.when(b >= 1)
            def _():
                wait_scatter(1 - slot)
            wait_scatter(slot)


def _moe(hn, row_src, row_w, blk_exp, n_used, w_gate, w_up, w_down):
    t, d = hn.shape
    n_rows = row_src.shape[0]
    rows = MOE_BLOCK
    n_blocks = n_rows // rows
    ffn = w_gate.shape[2]
    src3 = row_src.reshape(n_blocks, 1, rows)
    grid_spec = pltpu.PrefetchScalarGridSpec(
        num_scalar_prefetch=2,
        grid=(n_blocks,),
        in_specs=[
            pl.BlockSpec((1, 1, rows), lambda b, nu, be: (b, 0, 0), memory_space=pltpu.SMEM),
            pl.BlockSpec((1, 1, rows), lambda b, nu, be: (jnp.minimum(b + 1, n_blocks - 1), 0, 0),
                         memory_space=pltpu.SMEM),
            pl.BlockSpec((rows, 1), lambda b, nu, be: (b, 0)),
            pl.BlockSpec(memory_space=pl.ANY),
            pl.BlockSpec((1, d, ffn), lambda b, nu, be: (be[b], 0, 0)),
            pl.BlockSpec((1, d, ffn), lambda b, nu, be: (be[b], 0, 0)),
            pl.BlockSpec((1, ffn, d), lambda b, nu, be: (be[b], 0, 0)),
        ],
        out_specs=pl.BlockSpec(memory_space=pl.ANY),
        scratch_shapes=[pltpu.VMEM((2, rows, d), F32), pltpu.VMEM((2, rows, d), F32),
                        pltpu.SemaphoreType.DMA((2,)), pltpu.SemaphoreType.DMA((2,))],
    )
    out = pl.pallas_call(
        functools.partial(_moe_kernel, rows=rows, n_tok=t),
        grid_spec=grid_spec,
        out_shape=jax.ShapeDtypeStruct((2 * t + 2 * rows, d), F32),
        compiler_params=_params("arbitrary"),
    )(n_used, blk_exp, src3, src3, row_w.reshape(n_rows, 1), hn, w_gate, w_up, w_down)
    return out.reshape(t + rows, 2 * d)


def _dispatch(route, counts, n_tok):
    rows = MOE_BLOCK
    n_blocks = -(-(2 * n_tok) // rows) + N_EXPERTS
    e = route[:, 0:2].astype(jnp.int32)
    w = route[:, 2:4]
    rank = route[:, 4:6].astype(jnp.int32)
    cnt = counts[0, N_GROUPS:N_GROUPS + N_EXPERTS].astype(jnp.int32)
    padded = (cnt + rows - 1) // rows * rows
    pend = jnp.cumsum(padded)
    pstart = pend - padded
    dest = (pstart[e] + rank).reshape(-1)
    code = (2 * jnp.arange(n_tok, dtype=jnp.int32)[:, None] + jnp.arange(2, dtype=jnp.int32)[None, :]).reshape(-1)
    row_src = jnp.full((n_blocks * rows,), -1, jnp.int32).at[dest].set(code)
    row_w = jnp.zeros((n_blocks * rows,), F32).at[dest].set(w.reshape(-1))
    n_used = (pend[-1] // rows).astype(jnp.int32)
    blk = jnp.arange(n_blocks, dtype=jnp.int32)
    blk_exp = jnp.minimum(jnp.sum(pend[None, :] <= (blk * rows)[:, None], axis=1), N_EXPERTS - 1).astype(jnp.int32)
    blk_exp = jnp.where(blk < n_used, blk_exp, blk_exp[jnp.maximum(n_used - 1, 0)])
    return row_src, row_w, blk_exp, n_used.reshape(1)


def _final_kernel(x_ref, ya_ref, yb_ref, g_ref, o_ref, *, eps):
    x = x_ref[...] + ya_ref[...] + yb_ref[...]
    ms = jnp.mean(x * x, axis=-1, keepdims=True)
    o_ref[...] = x * lax.rsqrt(ms + eps) * g_ref[...]


def _final_norm(x, y2, g, *, tm=512):
    t, d = x.shape
    row = lambda i: (i, 0)
    return pl.pallas_call(
        functools.partial(_final_kernel, eps=RMS_EPS),
        grid=(t // tm,),
        in_specs=[pl.BlockSpec((tm, d), row), pl.BlockSpec((tm, d), row), pl.BlockSpec((tm, d), lambda i: (i, 1)),
                  pl.BlockSpec((1, d), lambda i: (0, 0))],
        out_specs=pl.BlockSpec((tm, d), row),
        out_shape=jax.ShapeDtypeStruct((t, d), F32),
        compiler_params=_params("arbitrary"),
    )(x, y2, y2, g.reshape(1, d))


def kernel(x, rel_bias, norm_attn, w_in, gain_a, gain_b, lam_q1, lam_k1, lam_q2, lam_k2, w_out, norm_ffn,
           w_router_group, w_router_expert, w_gate, w_up, w_down, norm_final):
    batch, seq, d = x.shape
    depth = w_in.shape[0]
    n_heads_a = gain_a.shape[1]
    n_heads_b = gain_b.shape[1]
    t = batch * seq
    xs = x.reshape(t, d)
    y2 = None
    for l in range(depth):
        proj, xs = _norm_proj(xs, y2, norm_attn[l], w_in[l].astype(BF16))

        parts = [_dilated_attention(proj, rel_bias, batch, seq, n_heads_a, w, dil) for (w, dil) in DILATED_PATTERNS]
        out_a = _dilated_mixture([p[0] for p in parts], [p[1] for p in parts], gain_a[l])

        lam_init = 0.8 - 0.6 * math.exp(-0.3 * l)
        lam = (jnp.exp(jnp.sum(lam_q1[l].astype(F32) * lam_k1[l].astype(F32)))
               - jnp.exp(jnp.sum(lam_q2[l].astype(F32) * lam_k2[l].astype(F32))) + lam_init)
        out_b = _diff_attention(proj, rel_bias, gain_b[l], lam, lam_init, batch, seq, n_heads_a, n_heads_b)

        w_r = jnp.concatenate([w_router_group[l], w_router_expert[l].transpose(1, 0, 2).reshape(d, N_EXPERTS)], axis=1)
        w_r = jnp.pad(w_r.astype(F32), ((0, 0), (0, LANES - w_r.shape[1])))
        wr_hi = w_r.astype(BF16)
        wr_lo = (w_r - wr_hi.astype(F32)).astype(BF16)
        xs, hn, route, counts = _out_router(out_a, out_b, xs, w_out[l].astype(BF16), norm_ffn[l], wr_hi, wr_lo)

        row_src, row_w, blk_exp, n_used = _dispatch(route, counts, t)
        y2 = _moe(hn, row_src, row_w, blk_exp, n_used,
                  w_gate[l].astype(BF16), w_up[l].astype(BF16), w_down[l].astype(BF16))
    out = _final_norm(xs, y2, norm_final)
    return out.reshape(batch, seq, d)
```

```python
import functools
import math

import jax
import jax.numpy as jnp
from jax import lax
from jax.experimental import pallas as pl
from jax.experimental.pallas import tpu as pltpu

F32 = jnp.float32
BF16 = jnp.bfloat16

LANES = 128
HEAD_DIM = 128
DIFF_DIM = HEAD_DIM // 2
DILATED_PATTERNS = ((128, 1), (512, 4), (2048, 16))
DILATED_HALO = 64
N_REL_BUCKETS = 32
REL_MAX_DISTANCE = 128
N_GROUPS = 4
EXPERTS_PER_GROUP = 8
N_EXPERTS = N_GROUPS * EXPERTS_PER_GROUP
MOE_BLOCK = 256
RMS_EPS = 1e-6
SUBLN_EPS = 1e-5
NEG_BIG = -1e30
VMEM_LIMIT = 56 * 1024 * 1024

_NT = (((1,), (1,)), ((), ()))

_R_E0, _R_E1, _R_W0, _R_W1, _R_RANK0, _R_RANK1 = range(6)


def _t5_bucket(rel):
    nb = N_REL_BUCKETS // 2
    max_exact = nb // 2
    n = -rel
    ret = jnp.where(n < 0, nb, 0)
    n = jnp.abs(n)
    nf = jnp.maximum(n, 1).astype(jnp.float32)
    large = max_exact + (jnp.log(nf / max_exact) / math.log(REL_MAX_DISTANCE / max_exact)
                         * (nb - max_exact)).astype(jnp.int32)
    large = jnp.minimum(large, nb - 1)
    return (ret + jnp.where(n < max_exact, n, large)).astype(jnp.int32)


def _rel_bias(bias_table, rel):
    onehot = (_t5_bucket(rel).reshape(-1)[None, :] == jnp.arange(N_REL_BUCKETS)[:, None]).astype(F32)
    vals = jnp.dot(bias_table.astype(F32).T, onehot, precision=lax.Precision.HIGHEST)
    return vals.reshape((bias_table.shape[1],) + rel.shape)


def _params(*sem):
    return pltpu.CompilerParams(dimension_semantics=sem, vmem_limit_bytes=VMEM_LIMIT)


def _add_expert_outputs(x, route_ref, ya_ref, yb_ref):
    return x + route_ref[:, _R_W0:_R_W0 + 1] * ya_ref[...] + route_ref[:, _R_W1:_R_W1 + 1] * yb_ref[...]


def _expert_output_specs(tm, d, row_map_3d):
    return [pl.BlockSpec((None, tm, d), row_map_3d(0)), pl.BlockSpec((None, tm, d), row_map_3d(1))]


def _norm_proj_kernel(*refs, has_y, eps):
    if has_y:
        x_ref, route_ref, ya_ref, yb_ref, g_ref, w_ref, o_ref, xs_ref, xn_ref = refs
    else:
        x_ref, g_ref, w_ref, o_ref, xn_ref = refs

    @pl.when(pl.program_id(1) == 0)
    def _():
        x = x_ref[...]
        if has_y:
            x = _add_expert_outputs(x, route_ref, ya_ref, yb_ref)
            xs_ref[...] = x
        ms = jnp.mean(x * x, axis=-1, keepdims=True)
        xn_ref[...] = (x * lax.rsqrt(ms + eps) * g_ref[...]).astype(BF16)

    o_ref[...] = jnp.dot(xn_ref[...], w_ref[...], preferred_element_type=F32).astype(o_ref.dtype)


def _norm_proj(x, moe, g, w, *, tm=512, tn=1024):
    t, d = x.shape
    n = w.shape[1]
    has_y = moe is not None
    row = lambda i, j: (i, 0)
    in_specs = [pl.BlockSpec((tm, d), row)]
    args = [x]
    if has_y:
        route, planes = moe
        in_specs += [pl.BlockSpec((tm, LANES), row)]
        in_specs += _expert_output_specs(tm, d, lambda p: (lambda i, j: (p, i, 0)))
        args += [route, planes, planes]
    in_specs += [pl.BlockSpec((1, d), lambda i, j: (0, 0)), pl.BlockSpec((d, tn), lambda i, j: (0, j))]
    args += [g.reshape(1, d), w]
    out_shape = [jax.ShapeDtypeStruct((t, n), BF16)]
    out_specs = [pl.BlockSpec((tm, tn), lambda i, j: (i, j))]
    if has_y:
        out_shape.append(jax.ShapeDtypeStruct((t, d), F32))
        out_specs.append(pl.BlockSpec((tm, d), row))
    res = pl.pallas_call(
        functools.partial(_norm_proj_kernel, has_y=has_y, eps=RMS_EPS),
        grid=(t // tm, n // tn),
        in_specs=in_specs, out_specs=out_specs, out_shape=out_shape,
        scratch_shapes=[pltpu.VMEM((tm, d), BF16)],
        compiler_params=_params("arbitrary", "arbitrary"),
    )(*args)
    return (res[0], res[1]) if has_y else (res[0], x)


def _dilated_kernel(q_ref, kp_ref, kc_ref, kn_ref, vp_ref, vc_ref, vn_ref, bias_ref, o_ref, lse_ref,
                    *, tq, n_heads, seq_sub, scale):
    i = pl.program_id(2)
    halo = DILATED_HALO
    win = tq + 2 * halo
    key_m = i * tq - halo + lax.broadcasted_iota(jnp.int32, (tq, win), 1)
    in_range = (key_m >= 0) & (key_m < seq_sub)
    lane = lax.broadcasted_iota(jnp.int32, (tq, LANES), 1)
    lse_all = jnp.zeros((tq, LANES), F32)
    for h in range(n_heads):
        cs = slice(h * HEAD_DIM, (h + 1) * HEAD_DIM)
        q = q_ref[:, cs]
        k = jnp.concatenate([kp_ref[tq - halo:, cs], kc_ref[:, cs], kn_ref[:halo, cs]], axis=0)
        v = jnp.concatenate([vp_ref[tq - halo:, cs], vc_ref[:, cs], vn_ref[:halo, cs]], axis=0)
        s = lax.dot_general(q, k, _NT, preferred_element_type=F32) * scale + bias_ref[h]
        s = jnp.where(in_range, s, NEG_BIG)
        m = jnp.max(s, axis=-1, keepdims=True)
        p = jnp.exp(s - m)
        den = jnp.sum(p, axis=-1, keepdims=True)
        o_ref[:, cs] = jnp.dot(p.astype(BF16), v, preferred_element_type=F32) / den
        lse_all = jnp.where(lane == h, m + jnp.log(den), lse_all)
    lse_ref[...] = lse_all


def _dilated_bias(bias_table, n_heads, window, dilation, tq):
    radius = window // (2 * dilation)
    assert radius <= DILATED_HALO
    win = tq + 2 * DILATED_HALO
    dm = jnp.arange(win)[None, :] - DILATED_HALO - jnp.arange(tq)[:, None]
    b = _rel_bias(bias_table, dm * dilation)[:n_heads]
    return jnp.where((jnp.abs(dm) <= radius)[None], b, NEG_BIG)


def _dilated_attention(proj, bias_table, batch, seq, n_heads, window, dilation, *, tq=128):
    t, pw = proj.shape
    wa = n_heads * HEAD_DIM
    nblk = pw // wa
    sub = seq // dilation
    nq = sub // tq
    view = proj.reshape(t // dilation, dilation * pw)
    bias = _dilated_bias(bias_table, n_heads, window, dilation, tq)

    def spec(which, shift):
        def imap(b, r, i):
            return (b * nq + jnp.clip(i + shift, 0, nq - 1), r * nblk + which)
        return pl.BlockSpec((tq, wa), imap)

    out, lse = pl.pallas_call(
        functools.partial(_dilated_kernel, tq=tq, n_heads=n_heads, seq_sub=sub, scale=HEAD_DIM ** -0.5),
        grid=(batch, dilation, nq),
        in_specs=[spec(0, 0), spec(1, -1), spec(1, 0), spec(1, 1), spec(2, -1), spec(2, 0), spec(2, 1),
                  pl.BlockSpec(bias.shape, lambda b, r, i: (0, 0, 0))],
        out_specs=[pl.BlockSpec((tq, wa), lambda b, r, i: (b * nq + i, r)),
                   pl.BlockSpec((tq, LANES), lambda b, r, i: (b * nq + i, r))],
        out_shape=[jax.ShapeDtypeStruct((t // dilation, dilation * wa), F32),
                   jax.ShapeDtypeStruct((t // dilation, dilation * LANES), F32)],
        compiler_params=_params("arbitrary", "arbitrary", "arbitrary"),
    )(view, view, view, view, view, view, view, bias)
    return out.reshape(t, wa), lse.reshape(t, LANES)


def _mix_kernel(o0_ref, o1_ref, o2_ref, l0_ref, l1_ref, l2_ref, g_ref, out_ref, *, n_heads, eps):
    ls = [l0_ref[...], l1_ref[...], l2_ref[...]]
    top = jnp.maximum(jnp.maximum(ls[0], ls[1]), ls[2])
    es = [jnp.exp(l - top) for l in ls]
    tot = es[0] + es[1] + es[2]
    ws = [e / tot for e in es]
    o_refs = (o0_ref, o1_ref, o2_ref)
    for h in range(n_heads):
        cs = slice(h * HEAD_DIM, (h + 1) * HEAD_DIM)
        mix = ws[0][:, h:h + 1] * o_refs[0][:, cs]
        for p in (1, 2):
            mix = mix + ws[p][:, h:h + 1] * o_refs[p][:, cs]
        ms = jnp.mean(mix * mix, axis=-1, keepdims=True)
        out_ref[:, cs] = (mix * lax.rsqrt(ms + eps) * g_ref[:, cs]).astype(out_ref.dtype)


def _dilated_mixture(outs, lses, gain, *, tm=512):
    t, wa = outs[0].shape
    n_heads = wa // HEAD_DIM
    row = lambda i: (i, 0)
    return pl.pallas_call(
        functools.partial(_mix_kernel, n_heads=n_heads, eps=RMS_EPS),
        grid=(t // tm,),
        in_specs=[pl.BlockSpec((tm, wa), row)] * 3 + [pl.BlockSpec((tm, LANES), row)] * 3
                 + [pl.BlockSpec((1, wa), lambda i: (0, 0))],
        out_specs=pl.BlockSpec((tm, wa), row),
        out_shape=jax.ShapeDtypeStruct((t, wa), BF16),
        compiler_params=_params("arbitrary"),
    )(*outs, *lses, gain.reshape(1, wa))


def _diff_attn_kernel(lam_ref, q_ref, k_ref, v_ref, bias_ref, g_ref, o_ref, s_scr, p_scr,
                      *, tile, n_tiles, eps, post_scale):
    i = pl.program_id(2)
    lane = lax.broadcasted_iota(jnp.int32, (tile, HEAD_DIM), 1)
    qs = q_ref[...] * (DIFF_DIM ** -0.5)
    zero = jnp.zeros_like(qs)
    q_maps = (jnp.where(lane < DIFF_DIM, qs, zero), jnp.where(lane >= DIFF_DIM, qs, zero))
    chunks = tile // LANES
    outs = []
    for mp in range(2):
        run_max = jnp.full((tile, LANES), -jnp.inf, F32)
        for kt in range(n_tiles):
            ks = slice(kt * tile, (kt + 1) * tile)
            rel_class = jnp.clip(kt - i, -2, 2) + 2
            s = lax.dot_general(q_maps[mp], k_ref[ks, :], _NT, preferred_element_type=F32) + bias_ref[0, rel_class]
            s_scr[mp, :, ks] = s
            for c in range(chunks):
                run_max = jnp.maximum(run_max, s[:, c * LANES:(c + 1) * LANES])
        m = jnp.max(run_max, axis=-1, keepdims=True)
        run_sum = jnp.zeros((tile, LANES), F32)
        for kt in range(n_tiles):
            ks = slice(kt * tile, (kt + 1) * tile)
            p = jnp.exp(s_scr[mp, :, ks] - m)
            for c in range(chunks):
                run_sum = run_sum + p[:, c * LANES:(c + 1) * LANES]
            p_scr[mp, :, ks] = p.astype(BF16)
        den = jnp.sum(run_sum, axis=-1, keepdims=True)
        outs.append(jnp.dot(p_scr[mp], v_ref[...], preferred_element_type=F32) / den)
    o = outs[0] - lam_ref[0] * outs[1]
    ms = jnp.mean(o * o, axis=-1, keepdims=True)
    o_ref[...] = ((o * lax.rsqrt(ms + eps) * g_ref[0]) * post_scale).astype(o_ref.dtype)


def _diff_attention(proj, bias_table, gain, lam, lam_init, batch, seq, n_heads_a, n_heads_b, *, tile=256):
    t, pw = proj.shape
    n_tiles = seq // tile
    q0 = 3 * n_heads_a
    k0 = q0 + n_heads_b
    v0 = k0 + n_heads_b
    assert tile + 1 >= REL_MAX_DISTANCE
    rel = (jnp.arange(5)[:, None, None] - 2) * tile + jnp.arange(tile)[None, None, :] - jnp.arange(tile)[None, :, None]
    bias = _rel_bias(bias_table, rel)[n_heads_a:]
    return pl.pallas_call(
        functools.partial(_diff_attn_kernel, tile=tile, n_tiles=n_tiles, eps=SUBLN_EPS,
                          post_scale=1.0 - lam_init),
        grid=(batch, n_heads_b, n_tiles),
        in_specs=[pl.BlockSpec(memory_space=pltpu.SMEM),
                  pl.BlockSpec((tile, HEAD_DIM), lambda b, h, i: (b * n_tiles + i, q0 + h)),
                  pl.BlockSpec((seq, HEAD_DIM), lambda b, h, i: (b, k0 + h)),
                  pl.BlockSpec((seq, HEAD_DIM), lambda b, h, i: (b, v0 + h)),
                  pl.BlockSpec((1, 5, tile, tile), lambda b, h, i: (h, 0, 0, 0)),
                  pl.BlockSpec((1, 1, HEAD_DIM), lambda b, h, i: (h, 0, 0))],
        out_specs=pl.BlockSpec((tile, HEAD_DIM), lambda b, h, i: (b * n_tiles + i, h)),
        out_shape=jax.ShapeDtypeStruct((t, n_heads_b * HEAD_DIM), BF16),
        scratch_shapes=[pltpu.VMEM((2, tile, seq), F32), pltpu.VMEM((2, tile, seq), BF16)],
        compiler_params=_params("arbitrary", "arbitrary", "arbitrary"),
    )(lam.reshape(1), proj, proj, proj, bias, gain.reshape(n_heads_b, 1, HEAD_DIM))


def _out_router_kernel(a_ref, b_ref, x_ref, wo_ref, g_ref, wrh_ref, wrl_ref,
                       x1_ref, hn_ref, route_ref, cnt_ref, *, tm, wa, eps):
    @pl.when(pl.program_id(0) == 0)
    def _():
        cnt_ref[...] = jnp.zeros(cnt_ref.shape, F32)

    y = jnp.dot(a_ref[...], wo_ref[:wa, :], preferred_element_type=F32)
    y = y + jnp.dot(b_ref[...], wo_ref[wa:, :], preferred_element_type=F32)
    x1 = x_ref[...] + y
    x1_ref[...] = x1
    ms = jnp.mean(x1 * x1, axis=-1, keepdims=True)
    hn = x1 * lax.rsqrt(ms + eps) * g_ref[...]
    hn_ref[...] = hn

    hi = hn.astype(BF16)
    lo = (hn - hi.astype(F32)).astype(BF16)
    logits = (jnp.dot(hi, wrh_ref[...], preferred_element_type=F32)
              + jnp.dot(lo, wrh_ref[...], preferred_element_type=F32)
              + jnp.dot(hi, wrl_ref[...], preferred_element_type=F32))

    lane = lax.broadcasted_iota(jnp.int32, (tm, LANES), 1)
    neg_inf = jnp.float32(-jnp.inf)

    def first_max(vals):
        top = jnp.max(vals, axis=-1, keepdims=True)
        idx = jnp.min(jnp.where(vals == top, lane, LANES), axis=-1, keepdims=True)
        return top, idx

    is_grp = lane < N_GROUPS
    g_top, g_idx = first_max(jnp.where(is_grp, logits, neg_inf))
    g_sum = jnp.sum(jnp.where(is_grp, jnp.exp(logits - g_top), 0.0), axis=-1, keepdims=True)
    p_grp = 1.0 / g_sum
    e_lo = N_GROUPS + EXPERTS_PER_GROUP * g_idx
    e_vals = jnp.where((lane >= e_lo) & (lane < e_lo + EXPERTS_PER_GROUP), logits, neg_inf)
    top1, i1 = first_max(e_vals)
    top2, i2 = first_max(jnp.where(lane == i1, neg_inf, e_vals))
    b2 = jnp.exp(top2 - top1)
    w0 = p_grp / (1.0 + b2)
    w1 = p_grp * b2 / (1.0 + b2)

    oh0 = (lane == i1).astype(F32)
    oh1 = (lane == i2).astype(F32)
    both = oh0 + oh1
    r_i = lax.broadcasted_iota(jnp.int32, (tm, tm), 0)
    c_i = lax.broadcasted_iota(jnp.int32, (tm, tm), 1)
    before = jnp.where(r_i > c_i, 1.0, 0.0).astype(BF16)
    base = cnt_ref[...] + jnp.dot(before, both.astype(BF16), preferred_element_type=F32)
    rank0 = jnp.sum(base * oh0, axis=-1, keepdims=True)
    rank1 = jnp.sum(base * oh1, axis=-1, keepdims=True)
    cnt_ref[...] = cnt_ref[...] + jnp.sum(both, axis=0, keepdims=True)

    fields = {_R_E0: (i1 - N_GROUPS).astype(F32), _R_E1: (i2 - N_GROUPS).astype(F32),
              _R_W0: w0, _R_W1: w1, _R_RANK0: rank0, _R_RANK1: rank1}
    route = jnp.zeros((tm, LANES), F32)
    for n, f in fields.items():
        route = jnp.where(lane == n, f, route)
    route_ref[...] = route


def _out_router(out_a, out_b, x, w_out, g, wr_hi, wr_lo, *, tm=256):
    t, d = x.shape
    wa = out_a.shape[1]
    wb = out_b.shape[1]
    row = lambda i: (i, 0)
    const = lambda i: (0, 0)
    return pl.pallas_call(
        functools.partial(_out_router_kernel, tm=tm, wa=wa, eps=RMS_EPS),
        grid=(t // tm,),
        in_specs=[pl.BlockSpec((tm, wa), row), pl.BlockSpec((tm, wb), row), pl.BlockSpec((tm, d), row),
                  pl.BlockSpec((wa + wb, d), const), pl.BlockSpec((1, d), const),
                  pl.BlockSpec((d, LANES), const), pl.BlockSpec((d, LANES), const)],
        out_specs=[pl.BlockSpec((tm, d), row), pl.BlockSpec((tm, d), row),
                   pl.BlockSpec((tm, LANES), row), pl.BlockSpec((1, LANES), const)],
        out_shape=[jax.ShapeDtypeStruct((t, d), F32), jax.ShapeDtypeStruct((t, d), F32),
                   jax.ShapeDtypeStruct((t, LANES), F32), jax.ShapeDtypeStruct((1, LANES), F32)],
        compiler_params=_params("arbitrary"),
    )(out_a, out_b, x, w_out, g.reshape(1, d), wr_hi, wr_lo)


def _moe_kernel(nused_ref, bexp_ref, src_ref, nxt_ref, hn_hbm, wg_ref, wu_ref, wd_ref, out_hbm,
                xbuf, ybuf, gsem, ssem, *, rows, n_tok):
    del bexp_ref
    b = pl.program_id(0)
    n_used = nused_ref[0]
    buf = lax.rem(b, 2)

    def start_gather(idx_ref, s):
        def body(r, carry):
            tok = lax.shift_right_logical(jnp.maximum(idx_ref[0, 0, r], 0), 1)
            pltpu.make_async_copy(hn_hbm.at[pl.ds(tok, 1)], xbuf.at[s, pl.ds(r, 1)], gsem.at[s]).start()
            return carry
        lax.fori_loop(0, rows, body, 0)

    def wait_gather(s):
        pltpu.make_async_copy(hn_hbm.at[pl.ds(0, rows)], xbuf.at[s], gsem.at[s]).wait()

    def start_scatter(s):
        def body(r, carry):
            v = src_ref[0, 0, r]
            real = v >= 0
            plane = jnp.where(real, v & 1, s)
            row = jnp.where(real, lax.shift_right_logical(v, 1), n_tok + r)
            pltpu.make_async_copy(ybuf.at[s, pl.ds(r, 1)], out_hbm.at[plane, pl.ds(row, 1)], ssem.at[s]).start()
            return carry
        lax.fori_loop(0, rows, body, 0)

    def wait_scatter(s):
        pltpu.make_async_copy(ybuf.at[s], out_hbm.at[0, pl.ds(0, rows)], ssem.at[s]).wait()

    @pl.when(b < n_used)
    def _():
        @pl.when(b == 0)
        def _():
            ybuf[1] = jnp.zeros(ybuf.shape[1:], F32)
            for plane in range(2):
                dump = pltpu.make_async_copy(ybuf.at[1], out_hbm.at[plane, pl.ds(n_tok, rows)], ssem.at[1])
                dump.start()
                dump.wait()
            start_gather(src_ref, buf)

        @pl.when(b + 1 < n_used)
        def _():
            start_gather(nxt_ref, 1 - buf)

        wait_gather(buf)

        @pl.when(b >= 2)
        def _():
            wait_scatter(buf)

        xb = xbuf[buf].astype(BF16)
        gate = jnp.dot(xb, wg_ref[0], preferred_element_type=F32)
        up = jnp.dot(xb, wu_ref[0], preferred_element_type=F32)
        hid = (gate * (1.0 / (1.0 + jnp.exp(-gate))) * up).astype(BF16)
        ybuf[buf] = jnp.dot(hid, wd_ref[0], preferred_element_type=F32)
        start_scatter(buf)

        @pl.when(b + 1 >= n_used)
        def _():
            @pl.when(b >= 1)
            def _():
                wait_scatter(1 - buf)
            wait_scatter(buf)


def _moe(hn, row_src, blk_exp, n_used, w_gate, w_up, w_down):
    t, d = hn.shape
    n_rows = row_src.shape[0]
    rows = MOE_BLOCK
    n_blocks = n_rows // rows
    ffn = w_gate.shape[2]
    src3 = row_src.reshape(n_blocks, 1, rows)
    grid_spec = pltpu.PrefetchScalarGridSpec(
        num_scalar_prefetch=2,
        grid=(n_blocks,),
        in_specs=[
            pl.BlockSpec((1, 1, rows), lambda b, nu, be: (b, 0, 0), memory_space=pltpu.SMEM),
            pl.BlockSpec((1, 1, rows), lambda b, nu, be: (jnp.minimum(b + 1, n_blocks - 1), 0, 0),
                         memory_space=pltpu.SMEM),
            pl.BlockSpec(memory_space=pl.ANY),
            pl.BlockSpec((1, d, ffn), lambda b, nu, be: (be[b], 0, 0)),
            pl.BlockSpec((1, d, ffn), lambda b, nu, be: (be[b], 0, 0)),
            pl.BlockSpec((1, ffn, d), lambda b, nu, be: (be[b], 0, 0)),
        ],
        out_specs=pl.BlockSpec(memory_space=pl.ANY),
        scratch_shapes=[pltpu.VMEM((2, rows, d), F32), pltpu.VMEM((2, rows, d), F32),
                        pltpu.SemaphoreType.DMA((2,)), pltpu.SemaphoreType.DMA((2,))],
    )
    return pl.pallas_call(
        functools.partial(_moe_kernel, rows=rows, n_tok=t),
        grid_spec=grid_spec,
        out_shape=jax.ShapeDtypeStruct((2, t + rows, d), F32),
        compiler_params=_params("arbitrary"),
    )(n_used, blk_exp, src3, src3, hn, w_gate, w_up, w_down)


def _dispatch(route, counts, n_tok):
    rows = MOE_BLOCK
    n_blocks = -(-(2 * n_tok) // rows) + N_EXPERTS
    e = route[:, _R_E0:_R_E1 + 1].astype(jnp.int32)
    rank = route[:, _R_RANK0:_R_RANK1 + 1].astype(jnp.int32)
    cnt = counts[0, N_GROUPS:N_GROUPS + N_EXPERTS].astype(jnp.int32)
    padded = (cnt + rows - 1) // rows * rows
    pend = jnp.cumsum(padded)
    pstart = pend - padded
    dest = (pstart[e] + rank).reshape(-1)
    code = (2 * jnp.arange(n_tok, dtype=jnp.int32)[:, None] + jnp.arange(2, dtype=jnp.int32)[None, :]).reshape(-1)
    row_src = jnp.full((n_blocks * rows,), -1, jnp.int32).at[dest].set(code)
    n_used = (pend[-1] // rows).astype(jnp.int32)
    blk = jnp.arange(n_blocks, dtype=jnp.int32)
    blk_exp = jnp.minimum(jnp.sum(pend[None, :] <= (blk * rows)[:, None], axis=1), N_EXPERTS - 1).astype(jnp.int32)
    blk_exp = jnp.where(blk < n_used, blk_exp, blk_exp[jnp.maximum(n_used - 1, 0)])
    return row_src, blk_exp, n_used.reshape(1)


def _final_kernel(x_ref, route_ref, ya_ref, yb_ref, g_ref, o_ref, *, eps):
    x = _add_expert_outputs(x_ref[...], route_ref, ya_ref, yb_ref)
    ms = jnp.mean(x * x, axis=-1, keepdims=True)
    o_ref[...] = x * lax.rsqrt(ms + eps) * g_ref[...]


def _final_norm(x, moe, g, *, tm=512):
    t, d = x.shape
    route, planes = moe
    row = lambda i: (i, 0)
    return pl.pallas_call(
        functools.partial(_final_kernel, eps=RMS_EPS),
        grid=(t // tm,),
        in_specs=[pl.BlockSpec((tm, d), row), pl.BlockSpec((tm, LANES), row)]
                 + _expert_output_specs(tm, d, lambda p: (lambda i: (p, i, 0)))
                 + [pl.BlockSpec((1, d), lambda i: (0, 0))],
        out_specs=pl.BlockSpec((tm, d), row),
        out_shape=jax.ShapeDtypeStruct((t, d), F32),
        compiler_params=_params("arbitrary"),
    )(x, route, planes, planes, g.reshape(1, d))


def kernel(x, rel_bias, norm_attn, w_in, gain_a, gain_b, lam_q1, lam_k1, lam_q2, lam_k2, w_out, norm_ffn,
           w_router_group, w_router_expert, w_gate, w_up, w_down, norm_final):
    batch, seq, d = x.shape
    depth = w_in.shape[0]
    n_heads_a = gain_a.shape[1]
    n_heads_b = gain_b.shape[1]
    t = batch * seq
    xs = x.reshape(t, d)
    moe = None
    for l in range(depth):
        proj, xs = _norm_proj(xs, moe, norm_attn[l], w_in[l].astype(BF16))

        parts = [_dilated_attention(proj, rel_bias, batch, seq, n_heads_a, w, dil) for (w, dil) in DILATED_PATTERNS]
        out_a = _dilated_mixture([p[0] for p in parts], [p[1] for p in parts], gain_a[l])

        lam_init = 0.8 - 0.6 * math.exp(-0.3 * l)
        lam = (jnp.exp(jnp.sum(lam_q1[l].astype(F32) * lam_k1[l].astype(F32)))
               - jnp.exp(jnp.sum(lam_q2[l].astype(F32) * lam_k2[l].astype(F32))) + lam_init)
        out_b = _diff_attention(proj, rel_bias, gain_b[l], lam, lam_init, batch, seq, n_heads_a, n_heads_b)

        w_r = jnp.concatenate([w_router_group[l], w_router_expert[l].transpose(1, 0, 2).reshape(d, N_EXPERTS)], axis=1)
        w_r = jnp.pad(w_r.astype(F32), ((0, 0), (0, LANES - w_r.shape[1])))
        wr_hi = w_r.astype(BF16)
        wr_lo = (w_r - wr_hi.astype(F32)).astype(BF16)
        xs, hn, route, counts = _out_router(out_a, out_b, xs, w_out[l].astype(BF16), norm_ffn[l], wr_hi, wr_lo)

        row_src, blk_exp, n_used = _dispatch(route, counts, t)
        planes = _moe(hn, row_src, blk_exp, n_used,
                      w_gate[l].astype(BF16), w_up[l].astype(BF16), w_down[l].astype(BF16))
        moe = (route, planes)
    out = _final_norm(xs, moe, norm_final)
    return out.reshape(batch, seq, d)
```

```python
import functools
import math

import jax
import jax.numpy as jnp
from jax import lax
from jax.experimental import pallas as pl
from jax.experimental.pallas import tpu as pltpu

F32 = jnp.float32
BF16 = jnp.bfloat16

LANES = 128
HEAD_DIM = 128
DIFF_DIM = HEAD_DIM // 2
DILATED_PATTERNS = ((128, 1), (512, 4), (2048, 16))
DILATED_HALO = 64
N_REL_BUCKETS = 32
REL_MAX_DISTANCE = 128
N_GROUPS = 4
EXPERTS_PER_GROUP = 8
N_EXPERTS = N_GROUPS * EXPERTS_PER_GROUP
MOE_BLOCK = 256
DMA_ISSUE_UNROLL = 8
RMS_EPS = 1e-6
SUBLN_EPS = 1e-5
NEG_BIG = -1e30
VMEM_LIMIT = 56 * 1024 * 1024

_NT = (((1,), (1,)), ((), ()))

_R_E0, _R_E1, _R_W0, _R_W1, _R_RANK0, _R_RANK1 = range(6)


def _t5_bucket(rel):
    nb = N_REL_BUCKETS // 2
    max_exact = nb // 2
    n = -rel
    ret = jnp.where(n < 0, nb, 0)
    n = jnp.abs(n)
    nf = jnp.maximum(n, 1).astype(jnp.float32)
    large = max_exact + (jnp.log(nf / max_exact) / math.log(REL_MAX_DISTANCE / max_exact)
                         * (nb - max_exact)).astype(jnp.int32)
    large = jnp.minimum(large, nb - 1)
    return (ret + jnp.where(n < max_exact, n, large)).astype(jnp.int32)


def _rel_bias(bias_table, rel):
    onehot = (_t5_bucket(rel).reshape(-1)[None, :] == jnp.arange(N_REL_BUCKETS)[:, None]).astype(F32)
    vals = jnp.dot(bias_table.astype(F32).T, onehot, precision=lax.Precision.HIGHEST)
    return vals.reshape((bias_table.shape[1],) + rel.shape)


def _params(*sem, **kwargs):
    return pltpu.CompilerParams(dimension_semantics=sem, vmem_limit_bytes=VMEM_LIMIT, **kwargs)


def _add_expert_outputs(x, route_ref, ya_ref, yb_ref):
    return x + route_ref[:, _R_W0:_R_W0 + 1] * ya_ref[...] + route_ref[:, _R_W1:_R_W1 + 1] * yb_ref[...]


def _expert_output_specs(tm, d, n_tok):
    blocks_per_plane, rem = divmod(n_tok + 2 * MOE_BLOCK, tm)
    assert rem == 0
    return [pl.BlockSpec((tm, d), lambda i, *_, p=p: (p * blocks_per_plane + i, 0)) for p in range(2)]


def _norm_proj_kernel(*refs, has_y, dils, n_split, eps):
    n_in = 6 if has_y else 3
    if has_y:
        x_ref, route_ref, ya_ref, yb_ref, g_ref, w_ref = refs[:n_in]
    else:
        x_ref, g_ref, w_ref = refs[:n_in]
    o_ref = refs[n_in]
    od_refs = refs[n_in + 1:n_in + 1 + len(dils)]
    rest = refs[n_in + 1 + len(dils):]
    xs_ref = rest[0] if has_y else None
    xn_ref, res_ref = rest[-2:]
    j = pl.program_id(1)

    @pl.when(j == 0)
    def _():
        x = x_ref[...]
        if has_y:
            x = _add_expert_outputs(x, route_ref, ya_ref, yb_ref)
            xs_ref[...] = x
        ms = jnp.mean(x * x, axis=-1, keepdims=True)
        xn_ref[...] = (x * lax.rsqrt(ms + eps) * g_ref[...]).astype(BF16)

    res = jnp.dot(xn_ref[...], w_ref[...], preferred_element_type=F32)
    o_ref[...] = res.astype(o_ref.dtype)

    @pl.when(j < n_split)
    def _():
        n_chunks, tm, _ = res_ref.shape
        tn = n_chunks * LANES
        for c in range(n_chunks):
            res_ref[c] = res[:, c * LANES:(c + 1) * LANES]
        for od_ref, dil in zip(od_refs, dils):
            for r in range(dil):
                for c in range(n_chunks):
                    col = r * tn + c * LANES
                    od_ref[:, col:col + LANES] = res_ref[c, pl.ds(r, tm // dil, stride=dil), :].astype(od_ref.dtype)


def _norm_proj(x, moe, g, w, layer, dils, n_split, *, tm=512, tn=1024):
    t, d = x.shape
    n = w.shape[2]
    has_y = moe is not None
    row = lambda i, j: (i, 0)
    in_specs = [pl.BlockSpec((tm, d), row)]
    args = [x]
    if has_y:
        route, planes = moe
        in_specs += [pl.BlockSpec((tm, LANES), row)]
        in_specs += _expert_output_specs(tm, d, t)
        args += [route, planes, planes]
    in_specs += [pl.BlockSpec((1, d), lambda i, j: (0, 0)), pl.BlockSpec((None, d, tn), lambda i, j: (layer, 0, j))]
    args += [g.reshape(1, d), w]
    out_shape = [jax.ShapeDtypeStruct((t, n), BF16)]
    out_specs = [pl.BlockSpec((tm, tn), lambda i, j: (i, j))]
    for dil in dils:
        out_shape.append(jax.ShapeDtypeStruct((t // dil, n_split * dil * tn), BF16))
        out_specs.append(pl.BlockSpec((tm // dil, dil * tn), lambda i, j: (i, jnp.minimum(j, n_split - 1))))
    if has_y:
        out_shape.append(jax.ShapeDtypeStruct((t, d), F32))
        out_specs.append(pl.BlockSpec((tm, d), row))
    res = pl.pallas_call(
        functools.partial(_norm_proj_kernel, has_y=has_y, dils=dils, n_split=n_split, eps=RMS_EPS),
        grid=(t // tm, n // tn),
        in_specs=in_specs, out_specs=out_specs, out_shape=out_shape,
        scratch_shapes=[pltpu.VMEM((tm, d), BF16), pltpu.VMEM((tn // LANES, tm, LANES), F32)],
        compiler_params=_params("arbitrary", "arbitrary"),
    )(*args)
    views = list(res[1:1 + len(dils)])
    return res[0], views, (res[-1] if has_y else x)


def _dilated_kernel(q_ref, kp_ref, kc_ref, kn_ref, vp_ref, vc_ref, vn_ref, bias_ref, o_ref, lse_ref,
                    *, tq, n_heads, seq_sub, scale):
    i = pl.program_id(2)
    halo = DILATED_HALO
    win = tq + 2 * halo
    key_m = i * tq - halo + lax.broadcasted_iota(jnp.int32, (tq, win), 1)
    in_range = (key_m >= 0) & (key_m < seq_sub)
    lane = lax.broadcasted_iota(jnp.int32, (tq, LANES), 1)
    lse_all = jnp.zeros((tq, LANES), F32)
    for h in range(n_heads):
        cs = slice(h * HEAD_DIM, (h + 1) * HEAD_DIM)
        q = q_ref[:, cs]
        k = jnp.concatenate([kp_ref[tq - halo:, cs], kc_ref[:, cs], kn_ref[:halo, cs]], axis=0)
        v = jnp.concatenate([vp_ref[tq - halo:, cs], vc_ref[:, cs], vn_ref[:halo, cs]], axis=0)
        s = lax.dot_general(q, k, _NT, preferred_element_type=F32) * scale + bias_ref[h]
        s = jnp.where(in_range, s, NEG_BIG)
        m = jnp.max(s, axis=-1, keepdims=True)
        p = jnp.exp(s - m)
        den = jnp.sum(p, axis=-1, keepdims=True)
        o_ref[:, cs] = jnp.dot(p.astype(BF16), v, preferred_element_type=F32) / den
        lse_all = jnp.where(lane == h, m + jnp.log(den), lse_all)
    lse_ref[...] = lse_all


def _dilated_bias(bias_table, n_heads, window, dilation, tq):
    radius = window // (2 * dilation)
    assert radius <= DILATED_HALO
    win = tq + 2 * DILATED_HALO
    dm = jnp.arange(win)[None, :] - DILATED_HALO - jnp.arange(tq)[:, None]
    b = _rel_bias(bias_table, dm * dilation)[:n_heads]
    return jnp.where((jnp.abs(dm) <= radius)[None], b, NEG_BIG)


def _dilated_attention(view, bias_table, batch, seq, n_heads, window, dilation, *, tq=128):
    t = view.shape[0] * dilation
    wa = n_heads * HEAD_DIM
    sub = seq // dilation
    nq = sub // tq
    bias = _dilated_bias(bias_table, n_heads, window, dilation, tq)

    def spec(which, shift):
        def imap(b, r, i):
            return (b * nq + jnp.clip(i + shift, 0, nq - 1), which * dilation + r)
        return pl.BlockSpec((tq, wa), imap)

    return pl.pallas_call(
        functools.partial(_dilated_kernel, tq=tq, n_heads=n_heads, seq_sub=sub, scale=HEAD_DIM ** -0.5),
        grid=(batch, dilation, nq),
        in_specs=[spec(0, 0), spec(1, -1), spec(1, 0), spec(1, 1), spec(2, -1), spec(2, 0), spec(2, 1),
                  pl.BlockSpec(bias.shape, lambda b, r, i: (0, 0, 0))],
        out_specs=[pl.BlockSpec((tq, wa), lambda b, r, i: (b * nq + i, r)),
                   pl.BlockSpec((tq, LANES), lambda b, r, i: (b * nq + i, r))],
        out_shape=[jax.ShapeDtypeStruct((t // dilation, dilation * wa), F32),
                   jax.ShapeDtypeStruct((t // dilation, dilation * LANES), F32)],
        compiler_params=_params("arbitrary", "arbitrary", "arbitrary"),
    )(view, view, view, view, view, view, view, bias)


def _mix_kernel(*refs, dils, n_heads, eps):
    n = len(dils)
    o_refs, l_refs = list(refs[:n]), list(refs[n:2 * n])
    g_ref, out_ref = refs[2 * n], refs[2 * n + 1]
    scratch = list(refs[2 * n + 2:])
    tm, wa = out_ref.shape
    def head_reader(ref):
        return lambda h: ref[:, h * HEAD_DIM:(h + 1) * HEAD_DIM]

    heads = [head_reader(o_ref) for o_ref in o_refs]
    for p, dil in enumerate(dils):
        if dil == 1:
            continue
        o_tok, l_tok = scratch.pop(0), scratch.pop(0)
        for r in range(dil):
            rows = pl.ds(r, tm // dil, stride=dil)
            for h in range(n_heads):
                o_tok[h, rows, :] = heads[p](r * n_heads + h)
            l_tok[rows, :] = l_refs[p][:, r * LANES:(r + 1) * LANES]
        heads[p], l_refs[p] = (lambda h, ref=o_tok: ref[h]), l_tok
    ls = [l_ref[...] for l_ref in l_refs]
    top = functools.reduce(jnp.maximum, ls)
    es = [jnp.exp(l - top) for l in ls]
    tot = functools.reduce(lambda a, b: a + b, es)
    ws = [e / tot for e in es]
    for h in range(n_heads):
        cs = slice(h * HEAD_DIM, (h + 1) * HEAD_DIM)
        mix = ws[0][:, h:h + 1] * heads[0](h)
        for p in range(1, n):
            mix = mix + ws[p][:, h:h + 1] * heads[p](h)
        ms = jnp.mean(mix * mix, axis=-1, keepdims=True)
        out_ref[:, cs] = (mix * lax.rsqrt(ms + eps) * g_ref[:, cs]).astype(out_ref.dtype)


def _dilated_mixture(outs, lses, dils, gain, *, tm=512):
    wa = gain.size
    t = outs[0].shape[0] * dils[0]
    n_heads = wa // HEAD_DIM
    row = lambda i: (i, 0)
    scratch = []
    for dil in dils:
        if dil > 1:
            scratch += [pltpu.VMEM((n_heads, tm, HEAD_DIM), F32), pltpu.VMEM((tm, LANES), F32)]
    return pl.pallas_call(
        functools.partial(_mix_kernel, dils=dils, n_heads=n_heads, eps=RMS_EPS),
        grid=(t // tm,),
        in_specs=[pl.BlockSpec((tm // dil, dil * wa), row) for dil in dils]
                 + [pl.BlockSpec((tm // dil, dil * LANES), row) for dil in dils]
                 + [pl.BlockSpec((1, wa), lambda i: (0, 0))],
        out_specs=pl.BlockSpec((tm, wa), row),
        out_shape=jax.ShapeDtypeStruct((t, wa), BF16),
        scratch_shapes=scratch,
        compiler_params=_params("arbitrary"),
    )(*outs, *lses, gain.reshape(1, wa))


def _diff_attn_kernel(lam_ref, q_ref, k_ref, v_ref, bias_ref, g_ref, o_ref, s_scr, p_scr,
                      *, tile, n_tiles, eps, post_scale):
    i = pl.program_id(2)
    lane = lax.broadcasted_iota(jnp.int32, (tile, HEAD_DIM), 1)
    qs = q_ref[...] * (DIFF_DIM ** -0.5)
    zero = jnp.zeros_like(qs)
    q_maps = (jnp.where(lane < DIFF_DIM, qs, zero), jnp.where(lane >= DIFF_DIM, qs, zero))
    chunks = tile // LANES
    outs = []
    for mp in range(2):
        run_max = jnp.full((tile, LANES), -jnp.inf, F32)
        for kt in range(n_tiles):
            ks = slice(kt * tile, (kt + 1) * tile)
            rel_class = jnp.clip(kt - i, -2, 2) + 2
            s = lax.dot_general(q_maps[mp], k_ref[ks, :], _NT, preferred_element_type=F32) + bias_ref[0, rel_class]
            s_scr[mp, :, ks] = s
            for c in range(chunks):
                run_max = jnp.maximum(run_max, s[:, c * LANES:(c + 1) * LANES])
        m = jnp.max(run_max, axis=-1, keepdims=True)
        run_sum = jnp.zeros((tile, LANES), F32)
        for kt in range(n_tiles):
            ks = slice(kt * tile, (kt + 1) * tile)
            p = jnp.exp(s_scr[mp, :, ks] - m)
            for c in range(chunks):
                run_sum = run_sum + p[:, c * LANES:(c + 1) * LANES]
            p_scr[mp, :, ks] = p.astype(BF16)
        den = jnp.sum(run_sum, axis=-1, keepdims=True)
        outs.append(jnp.dot(p_scr[mp], v_ref[...], preferred_element_type=F32) / den)
    o = outs[0] - lam_ref[0] * outs[1]
    ms = jnp.mean(o * o, axis=-1, keepdims=True)
    o_ref[...] = ((o * lax.rsqrt(ms + eps) * g_ref[0]) * post_scale).astype(o_ref.dtype)


def _diff_attention(proj, bias_table, gain, lam, lam_init, batch, seq, n_heads_a, n_heads_b, *, tile=256):
    t, pw = proj.shape
    n_tiles = seq // tile
    q0 = 3 * n_heads_a
    k0 = q0 + n_heads_b
    v0 = k0 + n_heads_b
    assert tile + 1 >= REL_MAX_DISTANCE
    rel = (jnp.arange(5)[:, None, None] - 2) * tile + jnp.arange(tile)[None, None, :] - jnp.arange(tile)[None, :, None]
    bias = _rel_bias(bias_table, rel)[n_heads_a:]
    return pl.pallas_call(
        functools.partial(_diff_attn_kernel, tile=tile, n_tiles=n_tiles, eps=SUBLN_EPS,
                          post_scale=1.0 - lam_init),
        grid=(batch, n_heads_b, n_tiles),
        in_specs=[pl.BlockSpec(memory_space=pltpu.SMEM),
                  pl.BlockSpec((tile, HEAD_DIM), lambda b, h, i: (b * n_tiles + i, q0 + h)),
                  pl.BlockSpec((seq, HEAD_DIM), lambda b, h, i: (b, k0 + h)),
                  pl.BlockSpec((seq, HEAD_DIM), lambda b, h, i: (b, v0 + h)),
                  pl.BlockSpec((1, 5, tile, tile), lambda b, h, i: (h, 0, 0, 0)),
                  pl.BlockSpec((1, 1, HEAD_DIM), lambda b, h, i: (h, 0, 0))],
        out_specs=pl.BlockSpec((tile, HEAD_DIM), lambda b, h, i: (b * n_tiles + i, h)),
        out_shape=jax.ShapeDtypeStruct((t, n_heads_b * HEAD_DIM), BF16),
        scratch_shapes=[pltpu.VMEM((2, tile, seq), F32), pltpu.VMEM((2, tile, seq), BF16)],
        compiler_params=_params("arbitrary", "arbitrary", "arbitrary"),
    )(lam.reshape(1), proj, proj, proj, bias, gain.reshape(n_heads_b, 1, HEAD_DIM))


def _out_router_kernel(a_ref, b_ref, x_ref, wo_ref, g_ref, wrh_ref, wrl_ref,
                       x1_ref, hn_ref, route_ref, cnt_ref, *, tm, wa, eps):
    @pl.when(pl.program_id(0) == 0)
    def _():
        cnt_ref[...] = jnp.zeros(cnt_ref.shape, F32)

    y = jnp.dot(a_ref[...], wo_ref[:wa, :], preferred_element_type=F32)
    y = y + jnp.dot(b_ref[...], wo_ref[wa:, :], preferred_element_type=F32)
    x1 = x_ref[...] + y
    x1_ref[...] = x1
    ms = jnp.mean(x1 * x1, axis=-1, keepdims=True)
    hn = x1 * lax.rsqrt(ms + eps) * g_ref[...]
    hn_ref[...] = hn

    hi = hn.astype(BF16)
    lo = (hn - hi.astype(F32)).astype(BF16)
    logits = (jnp.dot(hi, wrh_ref[...], preferred_element_type=F32)
              + jnp.dot(lo, wrh_ref[...], preferred_element_type=F32)
              + jnp.dot(hi, wrl_ref[...], preferred_element_type=F32))

    lane = lax.broadcasted_iota(jnp.int32, (tm, LANES), 1)
    neg_inf = jnp.float32(-jnp.inf)

    def first_max(vals):
        top = jnp.max(vals, axis=-1, keepdims=True)
        idx = jnp.min(jnp.where(vals == top, lane, LANES), axis=-1, keepdims=True)
        return top, idx

    is_grp = lane < N_GROUPS
    g_top, g_idx = first_max(jnp.where(is_grp, logits, neg_inf))
    g_sum = jnp.sum(jnp.where(is_grp, jnp.exp(logits - g_top), 0.0), axis=-1, keepdims=True)
    p_grp = 1.0 / g_sum
    e_lo = N_GROUPS + EXPERTS_PER_GROUP * g_idx
    e_vals = jnp.where((lane >= e_lo) & (lane < e_lo + EXPERTS_PER_GROUP), logits, neg_inf)
    top1, i1 = first_max(e_vals)
    top2, i2 = first_max(jnp.where(lane == i1, neg_inf, e_vals))
    b2 = jnp.exp(top2 - top1)
    w0 = p_grp / (1.0 + b2)
    w1 = p_grp * b2 / (1.0 + b2)

    oh0 = (lane == i1).astype(F32)
    oh1 = (lane == i2).astype(F32)
    both = oh0 + oh1
    r_i = lax.broadcasted_iota(jnp.int32, (tm, tm), 0)
    c_i = lax.broadcasted_iota(jnp.int32, (tm, tm), 1)
    before = jnp.where(r_i > c_i, 1.0, 0.0).astype(BF16)
    base = cnt_ref[...] + jnp.dot(before, both.astype(BF16), preferred_element_type=F32)
    rank0 = jnp.sum(base * oh0, axis=-1, keepdims=True)
    rank1 = jnp.sum(base * oh1, axis=-1, keepdims=True)
    cnt_ref[...] = cnt_ref[...] + jnp.sum(both, axis=0, keepdims=True)

    fields = {_R_E0: (i1 - N_GROUPS).astype(F32), _R_E1: (i2 - N_GROUPS).astype(F32),
              _R_W0: w0, _R_W1: w1, _R_RANK0: rank0, _R_RANK1: rank1}
    route = jnp.zeros((tm, LANES), F32)
    for n, f in fields.items():
        route = jnp.where(lane == n, f, route)
    route_ref[...] = route


def _out_router(out_a, out_b, x, w_out, layer, g, wr_hi, wr_lo, *, tm=256):
    t, d = x.shape
    wa = out_a.shape[1]
    wb = out_b.shape[1]
    row = lambda i: (i, 0)
    const = lambda i: (0, 0)
    return pl.pallas_call(
        functools.partial(_out_router_kernel, tm=tm, wa=wa, eps=RMS_EPS),
        grid=(t // tm,),
        in_specs=[pl.BlockSpec((tm, wa), row), pl.BlockSpec((tm, wb), row), pl.BlockSpec((tm, d), row),
                  pl.BlockSpec((None, wa + wb, d), lambda i: (layer, 0, 0)), pl.BlockSpec((1, d), const),
                  pl.BlockSpec((d, LANES), const), pl.BlockSpec((d, LANES), const)],
        out_specs=[pl.BlockSpec((tm, d), row), pl.BlockSpec((tm, d), row),
                   pl.BlockSpec((tm, LANES), row), pl.BlockSpec((1, LANES), const)],
        out_shape=[jax.ShapeDtypeStruct((t, d), F32), jax.ShapeDtypeStruct((t, d), F32),
                   jax.ShapeDtypeStruct((t, LANES), F32), jax.ShapeDtypeStruct((1, LANES), F32)],
        compiler_params=_params("arbitrary"),
    )(out_a, out_b, x, w_out, g.reshape(1, d), wr_hi, wr_lo)


def _moe_kernel(nused_ref, bexp_ref, tok_ref, nxt_ref, dst_ref, hn_hbm, wg_ref, wu_ref, wd_ref, out_hbm,
                xbuf, ybuf, gsem, ssem, *, rows, dump_rows):
    del bexp_ref
    b = pl.program_id(0)
    n_used = nused_ref[0]
    buf = lax.rem(b, 2)
    sub = DMA_ISSUE_UNROLL

    def issue_rows(copy_row):
        def body(g, carry):
            base = pl.multiple_of(g * sub, sub)
            for u in range(sub):
                copy_row(base + u).start()
            return carry
        lax.fori_loop(0, rows // sub, body, 0)

    def start_gather(idx_ref, s):
        issue_rows(lambda r: pltpu.make_async_copy(
            hn_hbm.at[pl.ds(idx_ref[0, 0, r], 1)], xbuf.at[s, pl.ds(r, 1)], gsem.at[s]))

    def wait_gather(s):
        pltpu.make_async_copy(hn_hbm.at[pl.ds(0, rows)], xbuf.at[s], gsem.at[s]).wait()

    def start_scatter(s):
        issue_rows(lambda r: pltpu.make_async_copy(
            ybuf.at[s, pl.ds(r, 1)], out_hbm.at[pl.ds(dst_ref[0, 0, r], 1)], ssem.at[s]))

    def wait_scatter(s):
        pltpu.make_async_copy(ybuf.at[s], out_hbm.at[pl.ds(0, rows)], ssem.at[s]).wait()

    @pl.when(b < n_used)
    def _():
        @pl.when(b == 0)
        def _():
            ybuf[1] = jnp.zeros(ybuf.shape[1:], F32)
            for start in dump_rows:
                dump = pltpu.make_async_copy(ybuf.at[1], out_hbm.at[pl.ds(start, rows)], ssem.at[1])
                dump.start()
                dump.wait()
            start_gather(tok_ref, buf)

        @pl.when(b + 1 < n_used)
        def _():
            start_gather(nxt_ref, 1 - buf)

        wait_gather(buf)

        @pl.when(b >= 2)
        def _():
            wait_scatter(buf)

        xb = xbuf[buf].astype(BF16)
        gate = jnp.dot(xb, wg_ref[0], preferred_element_type=F32)
        up = jnp.dot(xb, wu_ref[0], preferred_element_type=F32)
        hid = (gate * (1.0 / (1.0 + jnp.exp(-gate))) * up).astype(BF16)
        ybuf[buf] = jnp.dot(hid, wd_ref[0], preferred_element_type=F32)
        start_scatter(buf)

        @pl.when(b + 1 >= n_used)
        def _():
            @pl.when(b >= 1)
            def _():
                wait_scatter(1 - buf)
            wait_scatter(buf)


def _plane_rows(n_tok):
    return n_tok + 2 * MOE_BLOCK


def _moe(hn, row_tok, row_dst, blk_exp, n_used, w_gate, w_up, w_down, layer):
    t, d = hn.shape
    n_rows = row_tok.shape[0]
    rows = MOE_BLOCK
    n_blocks = n_rows // rows
    ffn = w_gate.shape[3]
    plane = _plane_rows(t)
    tok3 = row_tok.reshape(n_blocks, 1, rows)
    dst3 = row_dst.reshape(n_blocks, 1, rows)
    dump_rows = tuple(p * plane + t + s * rows for p in range(2) for s in range(2))
    grid_spec = pltpu.PrefetchScalarGridSpec(
        num_scalar_prefetch=2,
        grid=(n_blocks,),
        in_specs=[
            pl.BlockSpec((1, 1, rows), lambda b, nu, be: (b, 0, 0), memory_space=pltpu.SMEM),
            pl.BlockSpec((1, 1, rows), lambda b, nu, be: (jnp.minimum(b + 1, n_blocks - 1), 0, 0),
                         memory_space=pltpu.SMEM),
            pl.BlockSpec((1, 1, rows), lambda b, nu, be: (b, 0, 0), memory_space=pltpu.SMEM),
            pl.BlockSpec(memory_space=pl.ANY),
            pl.BlockSpec((None, 1, d, ffn), lambda b, nu, be: (layer, be[b], 0, 0)),
            pl.BlockSpec((None, 1, d, ffn), lambda b, nu, be: (layer, be[b], 0, 0)),
            pl.BlockSpec((None, 1, ffn, d), lambda b, nu, be: (layer, be[b], 0, 0)),
        ],
        out_specs=pl.BlockSpec(memory_space=pl.ANY),
        scratch_shapes=[pltpu.VMEM((2, rows, d), F32), pltpu.VMEM((2, rows, d), F32),
                        pltpu.SemaphoreType.DMA((2,)), pltpu.SemaphoreType.DMA((2,))],
    )
    return pl.pallas_call(
        functools.partial(_moe_kernel, rows=rows, dump_rows=dump_rows),
        grid_spec=grid_spec,
        out_shape=jax.ShapeDtypeStruct((2 * plane, d), F32),
        compiler_params=_params("arbitrary", disable_bounds_checks=True),
    )(n_used, blk_exp, tok3, tok3, dst3, hn, w_gate, w_up, w_down)


def _dispatch(route, counts, n_tok):
    rows = MOE_BLOCK
    plane = _plane_rows(n_tok)
    n_blocks = -(-(2 * n_tok) // rows) + N_EXPERTS
    e = route[:, _R_E0:_R_E1 + 1].astype(jnp.int32)
    rank = route[:, _R_RANK0:_R_RANK1 + 1].astype(jnp.int32)
    cnt = counts[0, N_GROUPS:N_GROUPS + N_EXPERTS].astype(jnp.int32)
    padded = (cnt + rows - 1) // rows * rows
    pend = jnp.cumsum(padded)
    pstart = pend - padded
    dest = (pstart[e] + rank).reshape(-1)
    code = (2 * jnp.arange(n_tok, dtype=jnp.int32)[:, None] + jnp.arange(2, dtype=jnp.int32)[None, :]).reshape(-1)
    row_src = jnp.full((n_blocks * rows,), -1, jnp.int32).at[dest].set(code)
    row = jnp.arange(n_blocks * rows, dtype=jnp.int32)
    dump = ((row // rows) % 2) * plane + n_tok + ((row // rows) % 2) * rows + row % rows
    row_tok = jnp.maximum(row_src, 0) // 2
    row_dst = jnp.where(row_src >= 0, (row_src % 2) * plane + row_src // 2, dump)
    n_used = (pend[-1] // rows).astype(jnp.int32)
    blk = jnp.arange(n_blocks, dtype=jnp.int32)
    blk_exp = jnp.minimum(jnp.sum(pend[None, :] <= (blk * rows)[:, None], axis=1), N_EXPERTS - 1).astype(jnp.int32)
    blk_exp = jnp.where(blk < n_used, blk_exp, blk_exp[jnp.maximum(n_used - 1, 0)])
    return row_tok, row_dst, blk_exp, n_used.reshape(1)


def _final_kernel(x_ref, route_ref, ya_ref, yb_ref, g_ref, o_ref, *, eps):
    x = _add_expert_outputs(x_ref[...], route_ref, ya_ref, yb_ref)
    ms = jnp.mean(x * x, axis=-1, keepdims=True)
    o_ref[...] = x * lax.rsqrt(ms + eps) * g_ref[...]


def _final_norm(x, moe, g, *, tm=512):
    t, d = x.shape
    route, planes = moe
    row = lambda i: (i, 0)
    return pl.pallas_call(
        functools.partial(_final_kernel, eps=RMS_EPS),
        grid=(t // tm,),
        in_specs=[pl.BlockSpec((tm, d), row), pl.BlockSpec((tm, LANES), row)]
                 + _expert_output_specs(tm, d, t)
                 + [pl.BlockSpec((1, d), lambda i: (0, 0))],
        out_specs=pl.BlockSpec((tm, d), row),
        out_shape=jax.ShapeDtypeStruct((t, d), F32),
        compiler_params=_params("arbitrary"),
    )(x, route, planes, planes, g.reshape(1, d))


def kernel(x, rel_bias, norm_attn, w_in, gain_a, gain_b, lam_q1, lam_k1, lam_q2, lam_k2, w_out, norm_ffn,
           w_router_group, w_router_expert, w_gate, w_up, w_down, norm_final):
    batch, seq, d = x.shape
    depth = w_in.shape[0]
    n_heads_a = gain_a.shape[1]
    n_heads_b = gain_b.shape[1]
    t = batch * seq
    xs = x.reshape(t, d)
    w_in, w_out, w_gate, w_up, w_down = (w.astype(BF16) for w in (w_in, w_out, w_gate, w_up, w_down))
    dils = tuple(dil for _, dil in DILATED_PATTERNS)
    extra_dils = tuple(dil for dil in dils if dil > 1)
    moe = None
    for l in range(depth):
        proj, views, xs = _norm_proj(xs, moe, norm_attn[l], w_in, l, extra_dils, 3)
        views = dict(zip(extra_dils, views))
        parts = [_dilated_attention(views.get(dil, proj), rel_bias, batch, seq, n_heads_a, w, dil)
                 for (w, dil) in DILATED_PATTERNS]
        out_a = _dilated_mixture([p[0] for p in parts], [p[1] for p in parts], dils, gain_a[l])

        lam_init = 0.8 - 0.6 * math.exp(-0.3 * l)
        lam = (jnp.exp(jnp.sum(lam_q1[l].astype(F32) * lam_k1[l].astype(F32)))
               - jnp.exp(jnp.sum(lam_q2[l].astype(F32) * lam_k2[l].astype(F32))) + lam_init)
        out_b = _diff_attention(proj, rel_bias, gain_b[l], lam, lam_init, batch, seq, n_heads_a, n_heads_b)

        w_r = jnp.concatenate([w_router_group[l], w_router_expert[l].transpose(1, 0, 2).reshape(d, N_EXPERTS)], axis=1)
        w_r = jnp.pad(w_r.astype(F32), ((0, 0), (0, LANES - w_r.shape[1])))
        wr_hi = w_r.astype(BF16)
        wr_lo = (w_r - wr_hi.astype(F32)).astype(BF16)
        xs, hn, route, counts = _out_router(out_a, out_b, xs, w_out, l, norm_ffn[l], wr_hi, wr_lo)

        row_tok, row_dst, blk_exp, n_used = _dispatch(route, counts, t)
        planes = _moe(hn, row_tok, row_dst, blk_exp, n_used, w_gate, w_up, w_down, l)
        moe = (route, planes)
    out = _final_norm(xs, moe, norm_final)
    return out.reshape(batch, seq, d)
```

```python
import functools
import math

import jax
import jax.numpy as jnp
from jax import lax
from jax.experimental import pallas as pl
from jax.experimental.pallas import tpu as pltpu

F32 = jnp.float32
BF16 = jnp.bfloat16

LANES = 128
HEAD_DIM = 128
DIFF_DIM = HEAD_DIM // 2
DILATED_PATTERNS = ((128, 1), (512, 4), (2048, 16))
DILATED_HALO = 64
N_REL_BUCKETS = 32
REL_MAX_DISTANCE = 128
N_GROUPS = 4
EXPERTS_PER_GROUP = 8
N_EXPERTS = N_GROUPS * EXPERTS_PER_GROUP
MOE_BLOCK = 256
DMA_ISSUE_UNROLL = 8
RMS_EPS = 1e-6
SUBLN_EPS = 1e-5
NEG_BIG = -1e30
VMEM_LIMIT = 56 * 1024 * 1024

_NT = (((1,), (1,)), ((), ()))

_R_E0, _R_E1, _R_W0, _R_W1, _R_RANK0, _R_RANK1 = range(6)


def _t5_bucket(rel):
    nb = N_REL_BUCKETS // 2
    max_exact = nb // 2
    n = -rel
    ret = jnp.where(n < 0, nb, 0)
    n = jnp.abs(n)
    nf = jnp.maximum(n, 1).astype(jnp.float32)
    large = max_exact + (jnp.log(nf / max_exact) / math.log(REL_MAX_DISTANCE / max_exact)
                         * (nb - max_exact)).astype(jnp.int32)
    large = jnp.minimum(large, nb - 1)
    return (ret + jnp.where(n < max_exact, n, large)).astype(jnp.int32)


def _rel_bias(bias_table, rel):
    onehot = (_t5_bucket(rel).reshape(-1)[None, :] == jnp.arange(N_REL_BUCKETS)[:, None]).astype(F32)
    vals = jnp.dot(bias_table.astype(F32).T, onehot, precision=lax.Precision.HIGHEST)
    return vals.reshape((bias_table.shape[1],) + rel.shape)


def _params(*sem, **kwargs):
    return pltpu.CompilerParams(dimension_semantics=sem, vmem_limit_bytes=VMEM_LIMIT, **kwargs)


def _add_expert_outputs(x, route_ref, ya_ref, yb_ref):
    return x + route_ref[:, _R_W0:_R_W0 + 1] * ya_ref[...] + route_ref[:, _R_W1:_R_W1 + 1] * yb_ref[...]


def _expert_output_specs(tm, d, n_tok):
    blocks_per_plane, rem = divmod(n_tok + 2 * MOE_BLOCK, tm)
    assert rem == 0
    return [pl.BlockSpec((tm, d), lambda i, *_, p=p: (p * blocks_per_plane + i, 0)) for p in range(2)]


def _norm_proj_kernel(*refs, has_y, dils, n_split, eps):
    n_in = 6 if has_y else 3
    if has_y:
        x_ref, route_ref, ya_ref, yb_ref, g_ref, w_ref = refs[:n_in]
    else:
        x_ref, g_ref, w_ref = refs[:n_in]
    o_ref = refs[n_in]
    od_refs = refs[n_in + 1:n_in + 1 + len(dils)]
    rest = refs[n_in + 1 + len(dils):]
    xs_ref = rest[0] if has_y else None
    xn_ref, res_ref = rest[-2:]
    j = pl.program_id(1)

    @pl.when(j == 0)
    def _():
        x = x_ref[...]
        if has_y:
            x = _add_expert_outputs(x, route_ref, ya_ref, yb_ref)
            xs_ref[...] = x
        ms = jnp.mean(x * x, axis=-1, keepdims=True)
        xn_ref[...] = (x * lax.rsqrt(ms + eps) * g_ref[...]).astype(BF16)

    res = jnp.dot(xn_ref[...], w_ref[...], preferred_element_type=F32)
    o_ref[...] = res.astype(o_ref.dtype)

    @pl.when(j < n_split)
    def _():
        n_chunks, tm, _ = res_ref.shape
        tn = n_chunks * LANES
        for c in range(n_chunks):
            res_ref[c] = res[:, c * LANES:(c + 1) * LANES]
        for od_ref, dil in zip(od_refs, dils):
            for r in range(dil):
                for c in range(n_chunks):
                    col = r * tn + c * LANES
                    od_ref[:, col:col + LANES] = res_ref[c, pl.ds(r, tm // dil, stride=dil), :].astype(od_ref.dtype)


def _norm_proj(x, moe, g, w, layer, dils, n_split, *, tm=512, tn=1024):
    t, d = x.shape
    n = w.shape[2]
    has_y = moe is not None
    row = lambda i, j: (i, 0)
    in_specs = [pl.BlockSpec((tm, d), row)]
    args = [x]
    if has_y:
        route, planes = moe
        in_specs += [pl.BlockSpec((tm, LANES), row)]
        in_specs += _expert_output_specs(tm, d, t)
        args += [route, planes, planes]
    in_specs += [pl.BlockSpec((1, d), lambda i, j: (0, 0)), pl.BlockSpec((None, d, tn), lambda i, j: (layer, 0, j))]
    args += [g.reshape(1, d), w]
    out_shape = [jax.ShapeDtypeStruct((t, n), BF16)]
    out_specs = [pl.BlockSpec((tm, tn), lambda i, j: (i, j))]
    for dil in dils:
        out_shape.append(jax.ShapeDtypeStruct((t // dil, n_split * dil * tn), BF16))
        out_specs.append(pl.BlockSpec((tm // dil, dil * tn), lambda i, j: (i, jnp.minimum(j, n_split - 1))))
    if has_y:
        out_shape.append(jax.ShapeDtypeStruct((t, d), F32))
        out_specs.append(pl.BlockSpec((tm, d), row))
    res = pl.pallas_call(
        functools.partial(_norm_proj_kernel, has_y=has_y, dils=dils, n_split=n_split, eps=RMS_EPS),
        grid=(t // tm, n // tn),
        in_specs=in_specs, out_specs=out_specs, out_shape=out_shape,
        scratch_shapes=[pltpu.VMEM((tm, d), BF16), pltpu.VMEM((tn // LANES, tm, LANES), F32)],
        compiler_params=_params("arbitrary", "arbitrary"),
    )(*args)
    views = list(res[1:1 + len(dils)])
    return res[0], views, (res[-1] if has_y else x)


def _dilated_kernel(q_ref, kp_ref, kc_ref, kn_ref, vp_ref, vc_ref, vn_ref, bias_ref, o_ref, lse_ref,
                    *, tq, n_heads, seq_sub, scale):
    i = pl.program_id(2)
    halo = DILATED_HALO
    win = tq + 2 * halo
    key_m = i * tq - halo + lax.broadcasted_iota(jnp.int32, (tq, win), 1)
    in_range = (key_m >= 0) & (key_m < seq_sub)
    lane = lax.broadcasted_iota(jnp.int32, (tq, LANES), 1)
    lse_all = jnp.zeros((tq, LANES), F32)
    for h in range(n_heads):
        cs = slice(h * HEAD_DIM, (h + 1) * HEAD_DIM)
        q = q_ref[:, cs]
        k = jnp.concatenate([kp_ref[tq - halo:, cs], kc_ref[:, cs], kn_ref[:halo, cs]], axis=0)
        v = jnp.concatenate([vp_ref[tq - halo:, cs], vc_ref[:, cs], vn_ref[:halo, cs]], axis=0)
        s = lax.dot_general(q, k, _NT, preferred_element_type=F32) * scale + bias_ref[h]
        s = jnp.where(in_range, s, NEG_BIG)
        m = jnp.max(s, axis=-1, keepdims=True)
        p = jnp.exp(s - m)
        den = jnp.sum(p, axis=-1, keepdims=True)
        o_ref[:, cs] = jnp.dot(p.astype(BF16), v, preferred_element_type=F32) / den
        lse_all = jnp.where(lane == h, m + jnp.log(den), lse_all)
    lse_ref[...] = lse_all


def _dilated_bias(bias_table, n_heads, window, dilation, tq):
    radius = window // (2 * dilation)
    assert radius <= DILATED_HALO
    win = tq + 2 * DILATED_HALO
    dm = jnp.arange(win)[None, :] - DILATED_HALO - jnp.arange(tq)[:, None]
    b = _rel_bias(bias_table, dm * dilation)[:n_heads]
    return jnp.where((jnp.abs(dm) <= radius)[None], b, NEG_BIG)


def _dilated_attention(view, bias_table, batch, seq, n_heads, window, dilation, *, tq=128):
    t = view.shape[0] * dilation
    wa = n_heads * HEAD_DIM
    sub = seq // dilation
    nq = sub // tq
    bias = _dilated_bias(bias_table, n_heads, window, dilation, tq)

    def spec(which, shift):
        def imap(b, r, i):
            return (b * nq + jnp.clip(i + shift, 0, nq - 1), which * dilation + r)
        return pl.BlockSpec((tq, wa), imap)

    return pl.pallas_call(
        functools.partial(_dilated_kernel, tq=tq, n_heads=n_heads, seq_sub=sub, scale=HEAD_DIM ** -0.5),
        grid=(batch, dilation, nq),
        in_specs=[spec(0, 0), spec(1, -1), spec(1, 0), spec(1, 1), spec(2, -1), spec(2, 0), spec(2, 1),
                  pl.BlockSpec(bias.shape, lambda b, r, i: (0, 0, 0))],
        out_specs=[pl.BlockSpec((tq, wa), lambda b, r, i: (b * nq + i, r)),
                   pl.BlockSpec((tq, LANES), lambda b, r, i: (b * nq + i, r))],
        out_shape=[jax.ShapeDtypeStruct((t // dilation, dilation * wa), F32),
                   jax.ShapeDtypeStruct((t // dilation, dilation * LANES), F32)],
        compiler_params=_params("arbitrary", "arbitrary", "arbitrary"),
    )(view, view, view, view, view, view, view, bias)


def _mix_kernel(*refs, dils, n_heads, eps):
    n = len(dils)
    o_refs, l_refs = list(refs[:n]), list(refs[n:2 * n])
    g_ref, out_ref = refs[2 * n], refs[2 * n + 1]
    scratch = list(refs[2 * n + 2:])
    tm, wa = out_ref.shape
    def head_reader(ref):
        return lambda h: ref[:, h * HEAD_DIM:(h + 1) * HEAD_DIM]

    heads = [head_reader(o_ref) for o_ref in o_refs]
    for p, dil in enumerate(dils):
        if dil == 1:
            continue
        o_tok, l_tok = scratch.pop(0), scratch.pop(0)
        for r in range(dil):
            rows = pl.ds(r, tm // dil, stride=dil)
            for h in range(n_heads):
                o_tok[h, rows, :] = heads[p](r * n_heads + h)
            l_tok[rows, :] = l_refs[p][:, r * LANES:(r + 1) * LANES]
        heads[p], l_refs[p] = (lambda h, ref=o_tok: ref[h]), l_tok
    ls = [l_ref[...] for l_ref in l_refs]
    top = functools.reduce(jnp.maximum, ls)
    es = [jnp.exp(l - top) for l in ls]
    tot = functools.reduce(lambda a, b: a + b, es)
    ws = [e / tot for e in es]
    for h in range(n_heads):
        cs = slice(h * HEAD_DIM, (h + 1) * HEAD_DIM)
        mix = ws[0][:, h:h + 1] * heads[0](h)
        for p in range(1, n):
            mix = mix + ws[p][:, h:h + 1] * heads[p](h)
        ms = jnp.mean(mix * mix, axis=-1, keepdims=True)
        out_ref[:, cs] = (mix * lax.rsqrt(ms + eps) * g_ref[:, cs]).astype(out_ref.dtype)


def _dilated_mixture(outs, lses, dils, gain, *, tm=512):
    wa = gain.size
    t = outs[0].shape[0] * dils[0]
    n_heads = wa // HEAD_DIM
    row = lambda i: (i, 0)
    scratch = []
    for dil in dils:
        if dil > 1:
            scratch += [pltpu.VMEM((n_heads, tm, HEAD_DIM), F32), pltpu.VMEM((tm, LANES), F32)]
    return pl.pallas_call(
        functools.partial(_mix_kernel, dils=dils, n_heads=n_heads, eps=RMS_EPS),
        grid=(t // tm,),
        in_specs=[pl.BlockSpec((tm // dil, dil * wa), row) for dil in dils]
                 + [pl.BlockSpec((tm // dil, dil * LANES), row) for dil in dils]
                 + [pl.BlockSpec((1, wa), lambda i: (0, 0))],
        out_specs=pl.BlockSpec((tm, wa), row),
        out_shape=jax.ShapeDtypeStruct((t, wa), BF16),
        scratch_shapes=scratch,
        compiler_params=_params("arbitrary"),
    )(*outs, *lses, gain.reshape(1, wa))


def _diff_attn_kernel(lam_ref, q_ref, k_ref, v_ref, bias_ref, g_ref, o_ref, s_scr, p_scr,
                      *, tile, n_tiles, eps, post_scale):
    i = pl.program_id(2)
    lane = lax.broadcasted_iota(jnp.int32, (tile, HEAD_DIM), 1)
    qs = q_ref[...] * (DIFF_DIM ** -0.5)
    zero = jnp.zeros_like(qs)
    q_maps = (jnp.where(lane < DIFF_DIM, qs, zero), jnp.where(lane >= DIFF_DIM, qs, zero))
    chunks = tile // LANES
    outs = []
    for mp in range(2):
        run_max = jnp.full((tile, LANES), -jnp.inf, F32)
        for kt in range(n_tiles):
            ks = slice(kt * tile, (kt + 1) * tile)
            rel_class = jnp.clip(kt - i, -2, 2) + 2
            s = lax.dot_general(q_maps[mp], k_ref[ks, :], _NT, preferred_element_type=F32) + bias_ref[0, rel_class]
            s_scr[mp, :, ks] = s
            for c in range(chunks):
                run_max = jnp.maximum(run_max, s[:, c * LANES:(c + 1) * LANES])
        m = jnp.max(run_max, axis=-1, keepdims=True)
        run_sum = jnp.zeros((tile, LANES), F32)
        for kt in range(n_tiles):
            ks = slice(kt * tile, (kt + 1) * tile)
            p = jnp.exp(s_scr[mp, :, ks] - m)
            for c in range(chunks):
                run_sum = run_sum + p[:, c * LANES:(c + 1) * LANES]
            p_scr[mp, :, ks] = p.astype(BF16)
        den = jnp.sum(run_sum, axis=-1, keepdims=True)
        outs.append(jnp.dot(p_scr[mp], v_ref[...], preferred_element_type=F32) / den)
    o = outs[0] - lam_ref[0] * outs[1]
    ms = jnp.mean(o * o, axis=-1, keepdims=True)
    o_ref[...] = ((o * lax.rsqrt(ms + eps) * g_ref[0]) * post_scale).astype(o_ref.dtype)


def _diff_attention(proj, bias_table, gain, lam, lam_init, batch, seq, n_heads_a, n_heads_b, *, tile=256):
    t, pw = proj.shape
    n_tiles = seq // tile
    q0 = 3 * n_heads_a
    k0 = q0 + n_heads_b
    v0 = k0 + n_heads_b
    assert tile + 1 >= REL_MAX_DISTANCE
    rel = (jnp.arange(5)[:, None, None] - 2) * tile + jnp.arange(tile)[None, None, :] - jnp.arange(tile)[None, :, None]
    bias = _rel_bias(bias_table, rel)[n_heads_a:]
    return pl.pallas_call(
        functools.partial(_diff_attn_kernel, tile=tile, n_tiles=n_tiles, eps=SUBLN_EPS,
                          post_scale=1.0 - lam_init),
        grid=(batch, n_heads_b, n_tiles),
        in_specs=[pl.BlockSpec(memory_space=pltpu.SMEM),
                  pl.BlockSpec((tile, HEAD_DIM), lambda b, h, i: (b * n_tiles + i, q0 + h)),
                  pl.BlockSpec((seq, HEAD_DIM), lambda b, h, i: (b, k0 + h)),
                  pl.BlockSpec((seq, HEAD_DIM), lambda b, h, i: (b, v0 + h)),
                  pl.BlockSpec((1, 5, tile, tile), lambda b, h, i: (h, 0, 0, 0)),
                  pl.BlockSpec((1, 1, HEAD_DIM), lambda b, h, i: (h, 0, 0))],
        out_specs=pl.BlockSpec((tile, HEAD_DIM), lambda b, h, i: (b * n_tiles + i, h)),
        out_shape=jax.ShapeDtypeStruct((t, n_heads_b * HEAD_DIM), BF16),
        scratch_shapes=[pltpu.VMEM((2, tile, seq), F32), pltpu.VMEM((2, tile, seq), BF16)],
        compiler_params=_params("arbitrary", "arbitrary", "arbitrary"),
    )(lam.reshape(1), proj, proj, proj, bias, gain.reshape(n_heads_b, 1, HEAD_DIM))


def _out_router_kernel(a_ref, b_ref, x_ref, wo_ref, g_ref, wrh_ref, wrl_ref,
                       x1_ref, hn_ref, route_ref, cnt_ref, *, tm, wa, eps):
    @pl.when(pl.program_id(0) == 0)
    def _():
        cnt_ref[...] = jnp.zeros(cnt_ref.shape, F32)

    y = jnp.dot(a_ref[...], wo_ref[:wa, :], preferred_element_type=F32)
    y = y + jnp.dot(b_ref[...], wo_ref[wa:, :], preferred_element_type=F32)
    x1 = x_ref[...] + y
    x1_ref[...] = x1
    ms = jnp.mean(x1 * x1, axis=-1, keepdims=True)
    hn = x1 * lax.rsqrt(ms + eps) * g_ref[...]
    hn_ref[...] = hn

    hi = hn.astype(BF16)
    lo = (hn - hi.astype(F32)).astype(BF16)
    logits = (jnp.dot(hi, wrh_ref[...], preferred_element_type=F32)
              + jnp.dot(lo, wrh_ref[...], preferred_element_type=F32)
              + jnp.dot(hi, wrl_ref[...], preferred_element_type=F32))

    lane = lax.broadcasted_iota(jnp.int32, (tm, LANES), 1)
    neg_inf = jnp.float32(-jnp.inf)

    def first_max(vals):
        top = jnp.max(vals, axis=-1, keepdims=True)
        idx = jnp.min(jnp.where(vals == top, lane, LANES), axis=-1, keepdims=True)
        return top, idx

    is_grp = lane < N_GROUPS
    g_top, g_idx = first_max(jnp.where(is_grp, logits, neg_inf))
    g_sum = jnp.sum(jnp.where(is_grp, jnp.exp(logits - g_top), 0.0), axis=-1, keepdims=True)
    p_grp = 1.0 / g_sum
    e_lo = N_GROUPS + EXPERTS_PER_GROUP * g_idx
    e_vals = jnp.where((lane >= e_lo) & (lane < e_lo + EXPERTS_PER_GROUP), logits, neg_inf)
    top1, i1 = first_max(e_vals)
    top2, i2 = first_max(jnp.where(lane == i1, neg_inf, e_vals))
    b2 = jnp.exp(top2 - top1)
    w0 = p_grp / (1.0 + b2)
    w1 = p_grp * b2 / (1.0 + b2)

    oh0 = (lane == i1).astype(F32)
    oh1 = (lane == i2).astype(F32)
    both = oh0 + oh1
    r_i = lax.broadcasted_iota(jnp.int32, (tm, tm), 0)
    c_i = lax.broadcasted_iota(jnp.int32, (tm, tm), 1)
    before = jnp.where(r_i > c_i, 1.0, 0.0).astype(BF16)
    base = cnt_ref[...] + jnp.dot(before, both.astype(BF16), preferred_element_type=F32)
    rank0 = jnp.sum(base * oh0, axis=-1, keepdims=True)
    rank1 = jnp.sum(base * oh1, axis=-1, keepdims=True)
    cnt_ref[...] = cnt_ref[...] + jnp.sum(both, axis=0, keepdims=True)

    fields = {_R_E0: (i1 - N_GROUPS).astype(F32), _R_E1: (i2 - N_GROUPS).astype(F32),
              _R_W0: w0, _R_W1: w1, _R_RANK0: rank0, _R_RANK1: rank1}
    route = jnp.zeros((tm, LANES), F32)
    for n, f in fields.items():
        route = jnp.where(lane == n, f, route)
    route_ref[...] = route


def _out_router(out_a, out_b, x, w_out, layer, g, wr_hi, wr_lo, *, tm=256):
    t, d = x.shape
    wa = out_a.shape[1]
    wb = out_b.shape[1]
    row = lambda i: (i, 0)
    const = lambda i: (0, 0)
    return pl.pallas_call(
        functools.partial(_out_router_kernel, tm=tm, wa=wa, eps=RMS_EPS),
        grid=(t // tm,),
        in_specs=[pl.BlockSpec((tm, wa), row), pl.BlockSpec((tm, wb), row), pl.BlockSpec((tm, d), row),
                  pl.BlockSpec((None, wa + wb, d), lambda i: (layer, 0, 0)), pl.BlockSpec((1, d), const),
                  pl.BlockSpec((d, LANES), const), pl.BlockSpec((d, LANES), const)],
        out_specs=[pl.BlockSpec((tm, d), row), pl.BlockSpec((tm, d), row),
                   pl.BlockSpec((tm, LANES), row), pl.BlockSpec((1, LANES), const)],
        out_shape=[jax.ShapeDtypeStruct((t, d), F32), jax.ShapeDtypeStruct((t, d), F32),
                   jax.ShapeDtypeStruct((t, LANES), F32), jax.ShapeDtypeStruct((1, LANES), F32)],
        compiler_params=_params("arbitrary"),
    )(out_a, out_b, x, w_out, g.reshape(1, d), wr_hi, wr_lo)


def _moe_kernel(nused_ref, bexp_ref, tok_ref, nxt_ref, dst_ref, hn_hbm, wg_ref, wu_ref, wd_ref, out_hbm,
                xbuf, ybuf, gsem, ssem, *, rows, dump_rows):
    del bexp_ref
    b = pl.program_id(0)
    first_half = pl.program_id(1) == 0
    n_used = nused_ref[0]
    buf = lax.rem(b, 2)
    sub = DMA_ISSUE_UNROLL

    def issue_rows(copy_row):
        def body(g, carry):
            base = pl.multiple_of(g * sub, sub)
            for u in range(sub):
                copy_row(base + u).start()
            return carry
        lax.fori_loop(0, rows // sub, body, 0)

    def start_gather(idx_ref, s):
        issue_rows(lambda r: pltpu.make_async_copy(
            hn_hbm.at[pl.ds(idx_ref[0, 0, r], 1)], xbuf.at[s, pl.ds(r, 1)], gsem.at[s]))

    def wait_gather(s):
        pltpu.make_async_copy(hn_hbm.at[pl.ds(0, rows)], xbuf.at[s], gsem.at[s]).wait()

    def start_scatter(s):
        issue_rows(lambda r: pltpu.make_async_copy(
            ybuf.at[s, pl.ds(r, 1)], out_hbm.at[pl.ds(dst_ref[0, 0, r], 1)], ssem.at[s]))

    def wait_scatter(s):
        pltpu.make_async_copy(ybuf.at[s], out_hbm.at[pl.ds(0, rows)], ssem.at[s]).wait()

    @pl.when(b < n_used)
    def _():
        @pl.when(jnp.logical_and(b == 0, first_half))
        def _():
            ybuf[1] = jnp.zeros(ybuf.shape[1:], F32)
            for start in dump_rows:
                dump = pltpu.make_async_copy(ybuf.at[1], out_hbm.at[pl.ds(start, rows)], ssem.at[1])
                dump.start()
                dump.wait()
            start_gather(tok_ref, buf)

        @pl.when(first_half)
        def _():
            @pl.when(b + 1 < n_used)
            def _():
                start_gather(nxt_ref, 1 - buf)

            wait_gather(buf)

            @pl.when(b >= 2)
            def _():
                wait_scatter(buf)

        x = xbuf[buf]
        gate = jnp.dot(x, wg_ref[0], preferred_element_type=F32)
        up = jnp.dot(x, wu_ref[0], preferred_element_type=F32)
        hid = gate * (1.0 / (1.0 + jnp.exp(-gate))) * up
        y = jnp.dot(hid, wd_ref[0], preferred_element_type=F32)

        @pl.when(first_half)
        def _():
            ybuf[buf] = y

        @pl.when(jnp.logical_not(first_half))
        def _():
            ybuf[buf] = ybuf[buf] + y
            start_scatter(buf)

            @pl.when(b + 1 >= n_used)
            def _():
                @pl.when(b >= 1)
                def _():
                    wait_scatter(1 - buf)
                wait_scatter(buf)


def _plane_rows(n_tok):
    return n_tok + 2 * MOE_BLOCK


def _moe(hn, row_tok, row_dst, blk_exp, n_used, w_gate, w_up, w_down, layer):
    t, d = hn.shape
    n_rows = row_tok.shape[0]
    rows = MOE_BLOCK
    n_blocks = n_rows // rows
    ffn = w_gate.shape[3]
    plane = _plane_rows(t)
    tok3 = row_tok.reshape(n_blocks, 1, rows)
    dst3 = row_dst.reshape(n_blocks, 1, rows)
    dump_rows = tuple(p * plane + t + s * rows for p in range(2) for s in range(2))
    half_ffn = ffn // 2

    def half(b, h, nu):
        last = nu[0] - 1
        return jnp.where(b <= last, (b + h) % 2, (last + 1) % 2)

    row_blk = lambda b, h, nu, be: (b, 0, 0)
    grid_spec = pltpu.PrefetchScalarGridSpec(
        num_scalar_prefetch=2,
        grid=(n_blocks, 2),
        in_specs=[
            pl.BlockSpec((1, 1, rows), row_blk, memory_space=pltpu.SMEM),
            pl.BlockSpec((1, 1, rows), lambda b, h, nu, be: (jnp.minimum(b + 1, n_blocks - 1), 0, 0),
                         memory_space=pltpu.SMEM),
            pl.BlockSpec((1, 1, rows), row_blk, memory_space=pltpu.SMEM),
            pl.BlockSpec(memory_space=pl.ANY),
            pl.BlockSpec((None, 1, d, half_ffn), lambda b, h, nu, be: (layer, be[b], 0, half(b, h, nu))),
            pl.BlockSpec((None, 1, d, half_ffn), lambda b, h, nu, be: (layer, be[b], 0, half(b, h, nu))),
            pl.BlockSpec((None, 1, half_ffn, d), lambda b, h, nu, be: (layer, be[b], half(b, h, nu), 0)),
        ],
        out_specs=pl.BlockSpec(memory_space=pl.ANY),
        scratch_shapes=[pltpu.VMEM((2, rows, d), F32), pltpu.VMEM((2, rows, d), F32),
                        pltpu.SemaphoreType.DMA((2,)), pltpu.SemaphoreType.DMA((2,))],
    )
    return pl.pallas_call(
        functools.partial(_moe_kernel, rows=rows, dump_rows=dump_rows),
        grid_spec=grid_spec,
        out_shape=jax.ShapeDtypeStruct((2 * plane, d), F32),
        compiler_params=_params("arbitrary", "arbitrary", disable_bounds_checks=True),
    )(n_used, blk_exp, tok3, tok3, dst3, hn, w_gate, w_up, w_down)


def _dispatch(route, counts, n_tok):
    rows = MOE_BLOCK
    plane = _plane_rows(n_tok)
    n_blocks = -(-(2 * n_tok) // rows) + N_EXPERTS
    e = route[:, _R_E0:_R_E1 + 1].astype(jnp.int32)
    rank = route[:, _R_RANK0:_R_RANK1 + 1].astype(jnp.int32)
    cnt = counts[0, N_GROUPS:N_GROUPS + N_EXPERTS].astype(jnp.int32)
    padded = (cnt + rows - 1) // rows * rows
    pend = jnp.cumsum(padded)
    pstart = pend - padded
    dest = (pstart[e] + rank).reshape(-1)
    code = (2 * jnp.arange(n_tok, dtype=jnp.int32)[:, None] + jnp.arange(2, dtype=jnp.int32)[None, :]).reshape(-1)
    row_src = jnp.full((n_blocks * rows,), -1, jnp.int32).at[dest].set(code)
    row = jnp.arange(n_blocks * rows, dtype=jnp.int32)
    dump = ((row // rows) % 2) * plane + n_tok + ((row // rows) % 2) * rows + row % rows
    row_tok = jnp.maximum(row_src, 0) // 2
    row_dst = jnp.where(row_src >= 0, (row_src % 2) * plane + row_src // 2, dump)
    n_used = (pend[-1] // rows).astype(jnp.int32)
    blk = jnp.arange(n_blocks, dtype=jnp.int32)
    blk_exp = jnp.minimum(jnp.sum(pend[None, :] <= (blk * rows)[:, None], axis=1), N_EXPERTS - 1).astype(jnp.int32)
    blk_exp = jnp.where(blk < n_used, blk_exp, blk_exp[jnp.maximum(n_used - 1, 0)])
    return row_tok, row_dst, blk_exp, n_used.reshape(1)


def _final_kernel(x_ref, route_ref, ya_ref, yb_ref, g_ref, o_ref, *, eps):
    x = _add_expert_outputs(x_ref[...], route_ref, ya_ref, yb_ref)
    ms = jnp.mean(x * x, axis=-1, keepdims=True)
    o_ref[...] = x * lax.rsqrt(ms + eps) * g_ref[...]


def _final_norm(x, moe, g, *, tm=512):
    t, d = x.shape
    route, planes = moe
    row = lambda i: (i, 0)
    return pl.pallas_call(
        functools.partial(_final_kernel, eps=RMS_EPS),
        grid=(t // tm,),
        in_specs=[pl.BlockSpec((tm, d), row), pl.BlockSpec((tm, LANES), row)]
                 + _expert_output_specs(tm, d, t)
                 + [pl.BlockSpec((1, d), lambda i: (0, 0))],
        out_specs=pl.BlockSpec((tm, d), row),
        out_shape=jax.ShapeDtypeStruct((t, d), F32),
        compiler_params=_params("arbitrary"),
    )(x, route, planes, planes, g.reshape(1, d))


def kernel(x, rel_bias, norm_attn, w_in, gain_a, gain_b, lam_q1, lam_k1, lam_q2, lam_k2, w_out, norm_ffn,
           w_router_group, w_router_expert, w_gate, w_up, w_down, norm_final):
    batch, seq, d = x.shape
    depth = w_in.shape[0]
    n_heads_a = gain_a.shape[1]
    n_heads_b = gain_b.shape[1]
    t = batch * seq
    xs = x.reshape(t, d)
    w_in, w_out = w_in.astype(BF16), w_out.astype(BF16)
    dils = tuple(dil for _, dil in DILATED_PATTERNS)
    extra_dils = tuple(dil for dil in dils if dil > 1)
    moe = None
    for l in range(depth):
        proj, views, xs = _norm_proj(xs, moe, norm_attn[l], w_in, l, extra_dils, 3)
        views = dict(zip(extra_dils, views))
        parts = [_dilated_attention(views.get(dil, proj), rel_bias, batch, seq, n_heads_a, w, dil)
                 for (w, dil) in DILATED_PATTERNS]
        out_a = _dilated_mixture([p[0] for p in parts], [p[1] for p in parts], dils, gain_a[l])

        lam_init = 0.8 - 0.6 * math.exp(-0.3 * l)
        lam = (jnp.exp(jnp.sum(lam_q1[l].astype(F32) * lam_k1[l].astype(F32)))
               - jnp.exp(jnp.sum(lam_q2[l].astype(F32) * lam_k2[l].astype(F32))) + lam_init)
        out_b = _diff_attention(proj, rel_bias, gain_b[l], lam, lam_init, batch, seq, n_heads_a, n_heads_b)

        w_r = jnp.concatenate([w_router_group[l], w_router_expert[l].transpose(1, 0, 2).reshape(d, N_EXPERTS)], axis=1)
        w_r = jnp.pad(w_r.astype(F32), ((0, 0), (0, LANES - w_r.shape[1])))
        wr_hi = w_r.astype(BF16)
        wr_lo = (w_r - wr_hi.astype(F32)).astype(BF16)
        xs, hn, route, counts = _out_router(out_a, out_b, xs, w_out, l, norm_ffn[l], wr_hi, wr_lo)

        row_tok, row_dst, blk_exp, n_used = _dispatch(route, counts, t)
        planes = _moe(hn, row_tok, row_dst, blk_exp, n_used, w_gate, w_up, w_down, l)
        moe = (route, planes)
    out = _final_norm(xs, moe, norm_final)
    return out.reshape(batch, seq, d)
```

```python
import functools
import math

import jax
import jax.numpy as jnp
from jax import lax
from jax.experimental import pallas as pl
from jax.experimental.pallas import tpu as pltpu

F32 = jnp.float32
BF16 = jnp.bfloat16

LANES = 128
HEAD_DIM = 128
DIFF_DIM = HEAD_DIM // 2
DILATED_PATTERNS = ((128, 1), (512, 4), (2048, 16))
DILATED_HALO = 64
N_REL_BUCKETS = 32
REL_MAX_DISTANCE = 128
N_GROUPS = 4
EXPERTS_PER_GROUP = 8
N_EXPERTS = N_GROUPS * EXPERTS_PER_GROUP
MOE_BLOCK = 256
DMA_ISSUE_UNROLL = 8
RMS_EPS = 1e-6
SUBLN_EPS = 1e-5
NEG_BIG = -1e30
VMEM_LIMIT = 56 * 1024 * 1024

_NT = (((1,), (1,)), ((), ()))

_R_E0, _R_E1, _R_W0, _R_W1, _R_RANK0, _R_RANK1 = range(6)


def _t5_bucket(rel):
    nb = N_REL_BUCKETS // 2
    max_exact = nb // 2
    n = -rel
    ret = jnp.where(n < 0, nb, 0)
    n = jnp.abs(n)
    nf = jnp.maximum(n, 1).astype(jnp.float32)
    large = max_exact + (jnp.log(nf / max_exact) / math.log(REL_MAX_DISTANCE / max_exact)
                         * (nb - max_exact)).astype(jnp.int32)
    large = jnp.minimum(large, nb - 1)
    return (ret + jnp.where(n < max_exact, n, large)).astype(jnp.int32)


def _rel_bias(bias_table, rel):
    onehot = (_t5_bucket(rel).reshape(-1)[None, :] == jnp.arange(N_REL_BUCKETS)[:, None]).astype(F32)
    vals = jnp.dot(bias_table.astype(F32).T, onehot, precision=lax.Precision.HIGHEST)
    return vals.reshape((bias_table.shape[1],) + rel.shape)


def _params(*sem, **kwargs):
    return pltpu.CompilerParams(dimension_semantics=sem, vmem_limit_bytes=VMEM_LIMIT, **kwargs)


def _add_expert_outputs(x, route_ref, ya_ref, yb_ref):
    return x + route_ref[:, _R_W0:_R_W0 + 1] * ya_ref[...] + route_ref[:, _R_W1:_R_W1 + 1] * yb_ref[...]


def _expert_output_specs(tm, d, n_tok):
    blocks_per_plane, rem = divmod(n_tok + 2 * MOE_BLOCK, tm)
    assert rem == 0
    return [pl.BlockSpec((tm, d), lambda i, *_, p=p: (p * blocks_per_plane + i, 0)) for p in range(2)]


def _norm_proj_kernel(*refs, has_y, dils, n_split, eps):
    n_in = 6 if has_y else 3
    if has_y:
        x_ref, route_ref, ya_ref, yb_ref, g_ref, w_ref = refs[:n_in]
    else:
        x_ref, g_ref, w_ref = refs[:n_in]
    o_ref = refs[n_in]
    od_refs = refs[n_in + 1:n_in + 1 + len(dils)]
    rest = refs[n_in + 1 + len(dils):]
    xs_ref = rest[0] if has_y else None
    xn_ref, res_ref = rest[-2:]
    j = pl.program_id(1)

    @pl.when(j == 0)
    def _():
        x = x_ref[...]
        if has_y:
            x = _add_expert_outputs(x, route_ref, ya_ref, yb_ref)
            xs_ref[...] = x
        ms = jnp.mean(x * x, axis=-1, keepdims=True)
        xn_ref[...] = (x * lax.rsqrt(ms + eps) * g_ref[...]).astype(BF16)

    res = jnp.dot(xn_ref[...], w_ref[...], preferred_element_type=F32)
    o_ref[...] = res.astype(o_ref.dtype)

    @pl.when(j < n_split)
    def _():
        n_chunks, tm, _ = res_ref.shape
        tn = n_chunks * LANES
        for c in range(n_chunks):
            res_ref[c] = res[:, c * LANES:(c + 1) * LANES]
        for od_ref, dil in zip(od_refs, dils):
            for r in range(dil):
                for c in range(n_chunks):
                    col = r * tn + c * LANES
                    od_ref[:, col:col + LANES] = res_ref[c, pl.ds(r, tm // dil, stride=dil), :].astype(od_ref.dtype)


def _norm_proj(x, moe, g, w, layer, dils, n_split, *, tm=512, tn=1024):
    t, d = x.shape
    n = w.shape[2]
    has_y = moe is not None
    row = lambda i, j: (i, 0)
    in_specs = [pl.BlockSpec((tm, d), row)]
    args = [x]
    if has_y:
        route, planes = moe
        in_specs += [pl.BlockSpec((tm, LANES), row)]
        in_specs += _expert_output_specs(tm, d, t)
        args += [route, planes, planes]
    in_specs += [pl.BlockSpec((1, d), lambda i, j: (0, 0)), pl.BlockSpec((None, d, tn), lambda i, j: (layer, 0, j))]
    args += [g.reshape(1, d), w]
    out_shape = [jax.ShapeDtypeStruct((t, n), BF16)]
    out_specs = [pl.BlockSpec((tm, tn), lambda i, j: (i, j))]
    for dil in dils:
        out_shape.append(jax.ShapeDtypeStruct((t // dil, n_split * dil * tn), BF16))
        out_specs.append(pl.BlockSpec((tm // dil, dil * tn), lambda i, j: (i, jnp.minimum(j, n_split - 1))))
    if has_y:
        out_shape.append(jax.ShapeDtypeStruct((t, d), F32))
        out_specs.append(pl.BlockSpec((tm, d), row))
    res = pl.pallas_call(
        functools.partial(_norm_proj_kernel, has_y=has_y, dils=dils, n_split=n_split, eps=RMS_EPS),
        grid=(t // tm, n // tn),
        in_specs=in_specs, out_specs=out_specs, out_shape=out_shape,
        scratch_shapes=[pltpu.VMEM((tm, d), BF16), pltpu.VMEM((tn // LANES, tm, LANES), F32)],
        compiler_params=_params("arbitrary", "arbitrary"),
    )(*args)
    views = list(res[1:1 + len(dils)])
    return res[0], views, (res[-1] if has_y else x)


def _dilated_kernel(q_ref, kp_ref, kc_ref, kn_ref, vp_ref, vc_ref, vn_ref, bias_ref, o_ref, lse_ref,
                    *, tq, n_heads, seq_sub, scale):
    i = pl.program_id(2)
    halo = DILATED_HALO
    win = tq + 2 * halo
    key_m = i * tq - halo + lax.broadcasted_iota(jnp.int32, (tq, win), 1)
    in_range = (key_m >= 0) & (key_m < seq_sub)
    lane = lax.broadcasted_iota(jnp.int32, (tq, LANES), 1)
    lse_all = jnp.zeros((tq, LANES), F32)
    for h in range(n_heads):
        cs = slice(h * HEAD_DIM, (h + 1) * HEAD_DIM)
        q = q_ref[:, cs]
        k = jnp.concatenate([kp_ref[tq - halo:, cs], kc_ref[:, cs], kn_ref[:halo, cs]], axis=0)
        v = jnp.concatenate([vp_ref[tq - halo:, cs], vc_ref[:, cs], vn_ref[:halo, cs]], axis=0)
        s = lax.dot_general(q, k, _NT, preferred_element_type=F32) * scale + bias_ref[h]
        s = jnp.where(in_range, s, NEG_BIG)
        m = jnp.max(s, axis=-1, keepdims=True)
        p = jnp.exp(s - m)
        den = jnp.sum(p, axis=-1, keepdims=True)
        o_ref[:, cs] = jnp.dot(p.astype(BF16), v, preferred_element_type=F32) / den
        lse_all = jnp.where(lane == h, m + jnp.log(den), lse_all)
    lse_ref[...] = lse_all


def _dilated_bias(bias_table, n_heads, window, dilation, tq):
    radius = window // (2 * dilation)
    assert radius <= DILATED_HALO
    win = tq + 2 * DILATED_HALO
    dm = jnp.arange(win)[None, :] - DILATED_HALO - jnp.arange(tq)[:, None]
    b = _rel_bias(bias_table, dm * dilation)[:n_heads]
    return jnp.where((jnp.abs(dm) <= radius)[None], b, NEG_BIG)


def _dilated_attention(view, bias_table, batch, seq, n_heads, window, dilation, *, tq=128):
    t = view.shape[0] * dilation
    wa = n_heads * HEAD_DIM
    sub = seq // dilation
    nq = sub // tq
    bias = _dilated_bias(bias_table, n_heads, window, dilation, tq)

    def spec(which, shift):
        def imap(b, r, i):
            return (b * nq + jnp.clip(i + shift, 0, nq - 1), which * dilation + r)
        return pl.BlockSpec((tq, wa), imap)

    return pl.pallas_call(
        functools.partial(_dilated_kernel, tq=tq, n_heads=n_heads, seq_sub=sub, scale=HEAD_DIM ** -0.5),
        grid=(batch, dilation, nq),
        in_specs=[spec(0, 0), spec(1, -1), spec(1, 0), spec(1, 1), spec(2, -1), spec(2, 0), spec(2, 1),
                  pl.BlockSpec(bias.shape, lambda b, r, i: (0, 0, 0))],
        out_specs=[pl.BlockSpec((tq, wa), lambda b, r, i: (b * nq + i, r)),
                   pl.BlockSpec((tq, LANES), lambda b, r, i: (b * nq + i, r))],
        out_shape=[jax.ShapeDtypeStruct((t // dilation, dilation * wa), F32),
                   jax.ShapeDtypeStruct((t // dilation, dilation * LANES), F32)],
        compiler_params=_params("arbitrary", "arbitrary", "arbitrary"),
    )(view, view, view, view, view, view, view, bias)


def _mix_kernel(*refs, dils, n_heads, eps):
    n = len(dils)
    o_refs, l_refs = list(refs[:n]), list(refs[n:2 * n])
    g_ref, out_ref = refs[2 * n], refs[2 * n + 1]
    scratch = list(refs[2 * n + 2:])
    tm, wa = out_ref.shape
    def head_reader(ref):
        return lambda h: ref[:, h * HEAD_DIM:(h + 1) * HEAD_DIM]

    heads = [head_reader(o_ref) for o_ref in o_refs]
    for p, dil in enumerate(dils):
        if dil == 1:
            continue
        o_tok, l_tok = scratch.pop(0), scratch.pop(0)
        for r in range(dil):
            rows = pl.ds(r, tm // dil, stride=dil)
            for h in range(n_heads):
                o_tok[h, rows, :] = heads[p](r * n_heads + h)
            l_tok[rows, :] = l_refs[p][:, r * LANES:(r + 1) * LANES]
        heads[p], l_refs[p] = (lambda h, ref=o_tok: ref[h]), l_tok
    ls = [l_ref[...] for l_ref in l_refs]
    top = functools.reduce(jnp.maximum, ls)
    es = [jnp.exp(l - top) for l in ls]
    tot = functools.reduce(lambda a, b: a + b, es)
    ws = [e / tot for e in es]
    for h in range(n_heads):
        cs = slice(h * HEAD_DIM, (h + 1) * HEAD_DIM)
        mix = ws[0][:, h:h + 1] * heads[0](h)
        for p in range(1, n):
            mix = mix + ws[p][:, h:h + 1] * heads[p](h)
        ms = jnp.mean(mix * mix, axis=-1, keepdims=True)
        out_ref[:, cs] = (mix * lax.rsqrt(ms + eps) * g_ref[:, cs]).astype(out_ref.dtype)


def _dilated_mixture(outs, lses, dils, gain, *, tm=512):
    wa = gain.size
    t = outs[0].shape[0] * dils[0]
    n_heads = wa // HEAD_DIM
    row = lambda i: (i, 0)
    scratch = []
    for dil in dils:
        if dil > 1:
            scratch += [pltpu.VMEM((n_heads, tm, HEAD_DIM), F32), pltpu.VMEM((tm, LANES), F32)]
    return pl.pallas_call(
        functools.partial(_mix_kernel, dils=dils, n_heads=n_heads, eps=RMS_EPS),
        grid=(t // tm,),
        in_specs=[pl.BlockSpec((tm // dil, dil * wa), row) for dil in dils]
                 + [pl.BlockSpec((tm // dil, dil * LANES), row) for dil in dils]
                 + [pl.BlockSpec((1, wa), lambda i: (0, 0))],
        out_specs=pl.BlockSpec((tm, wa), row),
        out_shape=jax.ShapeDtypeStruct((t, wa), BF16),
        scratch_shapes=scratch,
        compiler_params=_params("arbitrary"),
    )(*outs, *lses, gain.reshape(1, wa))


def _diff_attn_kernel(lam_ref, q_ref, k_ref, v_ref, bias_ref, g_ref, o_ref, s_scr, p_scr, v1_scr,
                      *, tile, n_tiles, eps, post_scale):
    i = pl.program_id(2)
    lane = lax.broadcasted_iota(jnp.int32, (tile, HEAD_DIM), 1)
    qs = q_ref[...] * (DIFF_DIM ** -0.5)
    zero = jnp.zeros_like(qs)
    q_maps = (jnp.where(lane < DIFF_DIM, qs, zero), jnp.where(lane >= DIFF_DIM, qs, zero))
    chunks = tile // LANES

    def fold(op, acc, vals):
        for c in range(chunks):
            acc = op(acc, vals[:, c * LANES:(c + 1) * LANES])
        return acc

    def scores(mp, kt, run_max):
        ks = slice(kt * tile, (kt + 1) * tile)
        rel_class = jnp.clip(kt - i, -2, 2) + 2
        s = lax.dot_general(q_maps[mp], k_ref[ks, :], _NT, preferred_element_type=F32) + bias_ref[0, rel_class]
        s_scr[mp, :, ks] = s
        return fold(jnp.maximum, run_max, s)

    def weights(mp, kt, m):
        ks = slice(kt * tile, (kt + 1) * tile)
        p_scr[mp, :, ks] = jnp.exp(s_scr[mp, :, ks] - m).astype(BF16)

    def attend(mp):
        acc = jnp.dot(p_scr[mp], v1_scr[...], preferred_element_type=F32)
        return acc[:, :HEAD_DIM] / acc[:, HEAD_DIM:]

    @pl.when(i == 0)
    def _():
        v1_scr[:, :HEAD_DIM] = v_ref[...]
        v1_scr[:, HEAD_DIM:] = jnp.ones((v1_scr.shape[0], HEAD_DIM), BF16)

    neg_inf = jnp.full((tile, LANES), -jnp.inf, F32)
    run_max = neg_inf
    for kt in range(n_tiles):
        run_max = scores(0, kt, run_max)
    m0 = jnp.max(run_max, axis=-1, keepdims=True)
    run_max = neg_inf
    for kt in range(n_tiles):
        weights(0, kt, m0)
        run_max = scores(1, kt, run_max)
    out0 = attend(0)
    m1 = jnp.max(run_max, axis=-1, keepdims=True)
    for kt in range(n_tiles):
        weights(1, kt, m1)
    o = out0 - lam_ref[0] * attend(1)
    ms = jnp.mean(o * o, axis=-1, keepdims=True)
    o_ref[...] = ((o * lax.rsqrt(ms + eps) * g_ref[0]) * post_scale).astype(o_ref.dtype)


def _diff_attention(proj, bias_table, gain, lam, lam_init, batch, seq, n_heads_a, n_heads_b, *, tile=256):
    t, pw = proj.shape
    n_tiles = seq // tile
    q0 = 3 * n_heads_a
    k0 = q0 + n_heads_b
    v0 = k0 + n_heads_b
    assert tile + 1 >= REL_MAX_DISTANCE
    rel = (jnp.arange(5)[:, None, None] - 2) * tile + jnp.arange(tile)[None, None, :] - jnp.arange(tile)[None, :, None]
    bias = _rel_bias(bias_table, rel)[n_heads_a:]
    return pl.pallas_call(
        functools.partial(_diff_attn_kernel, tile=tile, n_tiles=n_tiles, eps=SUBLN_EPS,
                          post_scale=1.0 - lam_init),
        grid=(batch, n_heads_b, n_tiles),
        in_specs=[pl.BlockSpec(memory_space=pltpu.SMEM),
                  pl.BlockSpec((tile, HEAD_DIM), lambda b, h, i: (b * n_tiles + i, q0 + h)),
                  pl.BlockSpec((seq, HEAD_DIM), lambda b, h, i: (b, k0 + h)),
                  pl.BlockSpec((seq, HEAD_DIM), lambda b, h, i: (b, v0 + h)),
                  pl.BlockSpec((1, 5, tile, tile), lambda b, h, i: (h, 0, 0, 0)),
                  pl.BlockSpec((1, 1, HEAD_DIM), lambda b, h, i: (h, 0, 0))],
        out_specs=pl.BlockSpec((tile, HEAD_DIM), lambda b, h, i: (b * n_tiles + i, h)),
        out_shape=jax.ShapeDtypeStruct((t, n_heads_b * HEAD_DIM), BF16),
        scratch_shapes=[pltpu.VMEM((2, tile, seq), F32), pltpu.VMEM((2, tile, seq), BF16),
                        pltpu.VMEM((seq, 2 * HEAD_DIM), BF16)],
        compiler_params=_params("arbitrary", "arbitrary", "arbitrary"),
    )(lam.reshape(1), proj, proj, proj, bias, gain.reshape(n_heads_b, 1, HEAD_DIM))


def _out_router_kernel(a_ref, b_ref, x_ref, wo_ref, g_ref, wrh_ref, wrl_ref,
                       x1_ref, hn_ref, route_ref, cnt_ref, *, tm, wa, eps):
    @pl.when(pl.program_id(0) == 0)
    def _():
        cnt_ref[...] = jnp.zeros(cnt_ref.shape, F32)

    y = jnp.dot(a_ref[...], wo_ref[:wa, :], preferred_element_type=F32)
    y = y + jnp.dot(b_ref[...], wo_ref[wa:, :], preferred_element_type=F32)
    x1 = x_ref[...] + y
    x1_ref[...] = x1
    ms = jnp.mean(x1 * x1, axis=-1, keepdims=True)
    hn = x1 * lax.rsqrt(ms + eps) * g_ref[...]
    hn_ref[...] = hn

    hi = hn.astype(BF16)
    lo = (hn - hi.astype(F32)).astype(BF16)
    logits = (jnp.dot(hi, wrh_ref[...], preferred_element_type=F32)
              + jnp.dot(lo, wrh_ref[...], preferred_element_type=F32)
              + jnp.dot(hi, wrl_ref[...], preferred_element_type=F32))

    lane = lax.broadcasted_iota(jnp.int32, (tm, LANES), 1)
    neg_inf = jnp.float32(-jnp.inf)

    def first_max(vals):
        top = jnp.max(vals, axis=-1, keepdims=True)
        idx = jnp.min(jnp.where(vals == top, lane, LANES), axis=-1, keepdims=True)
        return top, idx

    is_grp = lane < N_GROUPS
    g_top, g_idx = first_max(jnp.where(is_grp, logits, neg_inf))
    g_sum = jnp.sum(jnp.where(is_grp, jnp.exp(logits - g_top), 0.0), axis=-1, keepdims=True)
    p_grp = 1.0 / g_sum
    e_lo = N_GROUPS + EXPERTS_PER_GROUP * g_idx
    e_vals = jnp.where((lane >= e_lo) & (lane < e_lo + EXPERTS_PER_GROUP), logits, neg_inf)
    top1, i1 = first_max(e_vals)
    top2, i2 = first_max(jnp.where(lane == i1, neg_inf, e_vals))
    b2 = jnp.exp(top2 - top1)
    w0 = p_grp / (1.0 + b2)
    w1 = p_grp * b2 / (1.0 + b2)

    oh0 = (lane == i1).astype(F32)
    oh1 = (lane == i2).astype(F32)
    both = oh0 + oh1
    r_i = lax.broadcasted_iota(jnp.int32, (tm, tm), 0)
    c_i = lax.broadcasted_iota(jnp.int32, (tm, tm), 1)
    before = jnp.where(r_i > c_i, 1.0, 0.0).astype(BF16)
    base = cnt_ref[...] + jnp.dot(before, both.astype(BF16), preferred_element_type=F32)
    rank0 = jnp.sum(base * oh0, axis=-1, keepdims=True)
    rank1 = jnp.sum(base * oh1, axis=-1, keepdims=True)
    cnt_ref[...] = cnt_ref[...] + jnp.sum(both, axis=0, keepdims=True)

    fields = {_R_E0: (i1 - N_GROUPS).astype(F32), _R_E1: (i2 - N_GROUPS).astype(F32),
              _R_W0: w0, _R_W1: w1, _R_RANK0: rank0, _R_RANK1: rank1}
    route = jnp.zeros((tm, LANES), F32)
    for n, f in fields.items():
        route = jnp.where(lane == n, f, route)
    route_ref[...] = route


def _out_router(out_a, out_b, x, w_out, layer, g, wr_hi, wr_lo, *, tm=256):
    t, d = x.shape
    wa = out_a.shape[1]
    wb = out_b.shape[1]
    row = lambda i: (i, 0)
    const = lambda i: (0, 0)
    return pl.pallas_call(
        functools.partial(_out_router_kernel, tm=tm, wa=wa, eps=RMS_EPS),
        grid=(t // tm,),
        in_specs=[pl.BlockSpec((tm, wa), row), pl.BlockSpec((tm, wb), row), pl.BlockSpec((tm, d), row),
                  pl.BlockSpec((None, wa + wb, d), lambda i: (layer, 0, 0)), pl.BlockSpec((1, d), const),
                  pl.BlockSpec((d, LANES), const), pl.BlockSpec((d, LANES), const)],
        out_specs=[pl.BlockSpec((tm, d), row), pl.BlockSpec((tm, d), row),
                   pl.BlockSpec((tm, LANES), row), pl.BlockSpec((1, LANES), const)],
        out_shape=[jax.ShapeDtypeStruct((t, d), F32), jax.ShapeDtypeStruct((t, d), F32),
                   jax.ShapeDtypeStruct((t, LANES), F32), jax.ShapeDtypeStruct((1, LANES), F32)],
        compiler_params=_params("arbitrary"),
    )(out_a, out_b, x, w_out, g.reshape(1, d), wr_hi, wr_lo)


def _moe_kernel(nused_ref, bexp_ref, tok_ref, nxt_ref, dst_ref, hn_hbm, wg_ref, wu_ref, wd_ref, out_hbm,
                xbuf, ybuf, gsem, ssem, *, rows, dump_rows):
    del bexp_ref
    b = pl.program_id(0)
    first_half = pl.program_id(1) == 0
    n_used = nused_ref[0]
    buf = lax.rem(b, 2)
    sub = DMA_ISSUE_UNROLL

    def issue_rows(copy_row):
        def body(g, carry):
            base = pl.multiple_of(g * sub, sub)
            for u in range(sub):
                copy_row(base + u).start()
            return carry
        lax.fori_loop(0, rows // sub, body, 0)

    def start_gather(idx_ref, s):
        issue_rows(lambda r: pltpu.make_async_copy(
            hn_hbm.at[pl.ds(idx_ref[0, 0, r], 1)], xbuf.at[s, pl.ds(r, 1)], gsem.at[s]))

    def wait_gather(s):
        pltpu.make_async_copy(hn_hbm.at[pl.ds(0, rows)], xbuf.at[s], gsem.at[s]).wait()

    def start_scatter(s):
        issue_rows(lambda r: pltpu.make_async_copy(
            ybuf.at[s, pl.ds(r, 1)], out_hbm.at[pl.ds(dst_ref[0, 0, r], 1)], ssem.at[s]))

    def wait_scatter(s):
        pltpu.make_async_copy(ybuf.at[s], out_hbm.at[pl.ds(0, rows)], ssem.at[s]).wait()

    @pl.when(b < n_used)
    def _():
        @pl.when(jnp.logical_and(b == 0, first_half))
        def _():
            ybuf[1] = jnp.zeros(ybuf.shape[1:], F32)
            for start in dump_rows:
                dump = pltpu.make_async_copy(ybuf.at[1], out_hbm.at[pl.ds(start, rows)], ssem.at[1])
                dump.start()
                dump.wait()
            start_gather(tok_ref, buf)

        @pl.when(first_half)
        def _():
            @pl.when(b + 1 < n_used)
            def _():
                start_gather(nxt_ref, 1 - buf)

            wait_gather(buf)

            @pl.when(b >= 2)
            def _():
                wait_scatter(buf)

        x = xbuf[buf]
        gate = jnp.dot(x, wg_ref[0], preferred_element_type=F32)
        up = jnp.dot(x, wu_ref[0], preferred_element_type=F32)
        hid = gate * (1.0 / (1.0 + jnp.exp(-gate))) * up
        y = jnp.dot(hid, wd_ref[0], preferred_element_type=F32)

        @pl.when(first_half)
        def _():
            ybuf[buf] = y

        @pl.when(jnp.logical_not(first_half))
        def _():
            ybuf[buf] = ybuf[buf] + y
            start_scatter(buf)

            @pl.when(b + 1 >= n_used)
            def _():
                @pl.when(b >= 1)
                def _():
                    wait_scatter(1 - buf)
                wait_scatter(buf)


def _plane_rows(n_tok):
    return n_tok + 2 * MOE_BLOCK


def _moe(hn, row_tok, row_dst, blk_exp, n_used, w_gate, w_up, w_down, layer):
    t, d = hn.shape
    n_rows = row_tok.shape[0]
    rows = MOE_BLOCK
    n_blocks = n_rows // rows
    ffn = w_gate.shape[3]
    plane = _plane_rows(t)
    tok3 = row_tok.reshape(n_blocks, 1, rows)
    dst3 = row_dst.reshape(n_blocks, 1, rows)
    dump_rows = tuple(p * plane + t + s * rows for p in range(2) for s in range(2))
    half_ffn = ffn // 2

    def half(b, h, nu):
        last = nu[0] - 1
        return jnp.where(b <= last, (b + h) % 2, (last + 1) % 2)

    row_blk = lambda b, h, nu, be: (b, 0, 0)
    grid_spec = pltpu.PrefetchScalarGridSpec(
        num_scalar_prefetch=2,
        grid=(n_blocks, 2),
        in_specs=[
            pl.BlockSpec((1, 1, rows), row_blk, memory_space=pltpu.SMEM),
            pl.BlockSpec((1, 1, rows), lambda b, h, nu, be: (jnp.minimum(b + 1, n_blocks - 1), 0, 0),
                         memory_space=pltpu.SMEM),
            pl.BlockSpec((1, 1, rows), row_blk, memory_space=pltpu.SMEM),
            pl.BlockSpec(memory_space=pl.ANY),
            pl.BlockSpec((None, 1, d, half_ffn), lambda b, h, nu, be: (layer, be[b], 0, half(b, h, nu))),
            pl.BlockSpec((None, 1, d, half_ffn), lambda b, h, nu, be: (layer, be[b], 0, half(b, h, nu))),
            pl.BlockSpec((None, 1, half_ffn, d), lambda b, h, nu, be: (layer, be[b], half(b, h, nu), 0)),
        ],
        out_specs=pl.BlockSpec(memory_space=pl.ANY),
        scratch_shapes=[pltpu.VMEM((2, rows, d), F32), pltpu.VMEM((2, rows, d), F32),
                        pltpu.SemaphoreType.DMA((2,)), pltpu.SemaphoreType.DMA((2,))],
    )
    return pl.pallas_call(
        functools.partial(_moe_kernel, rows=rows, dump_rows=dump_rows),
        grid_spec=grid_spec,
        out_shape=jax.ShapeDtypeStruct((2 * plane, d), F32),
        compiler_params=_params("arbitrary", "arbitrary", disable_bounds_checks=True),
    )(n_used, blk_exp, tok3, tok3, dst3, hn, w_gate, w_up, w_down)


def _dispatch(route, counts, n_tok):
    rows = MOE_BLOCK
    plane = _plane_rows(n_tok)
    n_blocks = -(-(2 * n_tok) // rows) + N_EXPERTS
    e = route[:, _R_E0:_R_E1 + 1].astype(jnp.int32)
    rank = route[:, _R_RANK0:_R_RANK1 + 1].astype(jnp.int32)
    cnt = counts[0, N_GROUPS:N_GROUPS + N_EXPERTS].astype(jnp.int32)
    padded = (cnt + rows - 1) // rows * rows
    pend = jnp.cumsum(padded)
    pstart = pend - padded
    dest = (pstart[e] + rank).reshape(-1)
    code = (2 * jnp.arange(n_tok, dtype=jnp.int32)[:, None] + jnp.arange(2, dtype=jnp.int32)[None, :]).reshape(-1)
    row_src = jnp.full((n_blocks * rows,), -1, jnp.int32).at[dest].set(code)
    row = jnp.arange(n_blocks * rows, dtype=jnp.int32)
    dump = ((row // rows) % 2) * plane + n_tok + ((row // rows) % 2) * rows + row % rows
    row_tok = jnp.maximum(row_src, 0) // 2
    row_dst = jnp.where(row_src >= 0, (row_src % 2) * plane + row_src // 2, dump)
    n_used = (pend[-1] // rows).astype(jnp.int32)
    blk = jnp.arange(n_blocks, dtype=jnp.int32)
    blk_exp = jnp.minimum(jnp.sum(pend[None, :] <= (blk * rows)[:, None], axis=1), N_EXPERTS - 1).astype(jnp.int32)
    blk_exp = jnp.where(blk < n_used, blk_exp, blk_exp[jnp.maximum(n_used - 1, 0)])
    return row_tok, row_dst, blk_exp, n_used.reshape(1)


def _final_kernel(x_ref, route_ref, ya_ref, yb_ref, g_ref, o_ref, *, eps):
    x = _add_expert_outputs(x_ref[...], route_ref, ya_ref, yb_ref)
    ms = jnp.mean(x * x, axis=-1, keepdims=True)
    o_ref[...] = x * lax.rsqrt(ms + eps) * g_ref[...]


def _final_norm(x, moe, g, *, tm=512):
    t, d = x.shape
    route, planes = moe
    row = lambda i: (i, 0)
    return pl.pallas_call(
        functools.partial(_final_kernel, eps=RMS_EPS),
        grid=(t // tm,),
        in_specs=[pl.BlockSpec((tm, d), row), pl.BlockSpec((tm, LANES), row)]
                 + _expert_output_specs(tm, d, t)
                 + [pl.BlockSpec((1, d), lambda i: (0, 0))],
        out_specs=pl.BlockSpec((tm, d), row),
        out_shape=jax.ShapeDtypeStruct((t, d), F32),
        compiler_params=_params("arbitrary"),
    )(x, route, planes, planes, g.reshape(1, d))


def kernel(x, rel_bias, norm_attn, w_in, gain_a, gain_b, lam_q1, lam_k1, lam_q2, lam_k2, w_out, norm_ffn,
           w_router_group, w_router_expert, w_gate, w_up, w_down, norm_final):
    batch, seq, d = x.shape
    depth = w_in.shape[0]
    n_heads_a = gain_a.shape[1]
    n_heads_b = gain_b.shape[1]
    t = batch * seq
    xs = x.reshape(t, d)
    w_in, w_out = w_in.astype(BF16), w_out.astype(BF16)
    dils = tuple(dil for _, dil in DILATED_PATTERNS)
    extra_dils = tuple(dil for dil in dils if dil > 1)
    moe = None
    for l in range(depth):
        proj, views, xs = _norm_proj(xs, moe, norm_attn[l], w_in, l, extra_dils, 3)
        views = dict(zip(extra_dils, views))
        parts = [_dilated_attention(views.get(dil, proj), rel_bias, batch, seq, n_heads_a, w, dil)
                 for (w, dil) in DILATED_PATTERNS]
        out_a = _dilated_mixture([p[0] for p in parts], [p[1] for p in parts], dils, gain_a[l])

        lam_init = 0.8 - 0.6 * math.exp(-0.3 * l)
        lam = (jnp.exp(jnp.sum(lam_q1[l].astype(F32) * lam_k1[l].astype(F32)))
               - jnp.exp(jnp.sum(lam_q2[l].astype(F32) * lam_k2[l].astype(F32))) + lam_init)
        out_b = _diff_attention(proj, rel_bias, gain_b[l], lam, lam_init, batch, seq, n_heads_a, n_heads_b)

        w_r = jnp.concatenate([w_router_group[l], w_router_expert[l].transpose(1, 0, 2).reshape(d, N_EXPERTS)], axis=1)
        w_r = jnp.pad(w_r.astype(F32), ((0, 0), (0, LANES - w_r.shape[1])))
        wr_hi = w_r.astype(BF16)
        wr_lo = (w_r - wr_hi.astype(F32)).astype(BF16)
        xs, hn, route, counts = _out_router(out_a, out_b, xs, w_out, l, norm_ffn[l], wr_hi, wr_lo)

        row_tok, row_dst, blk_exp, n_used = _dispatch(route, counts, t)
        planes = _moe(hn, row_tok, row_dst, blk_exp, n_used, w_gate, w_up, w_down, l)
        moe = (route, planes)
    out = _final_norm(xs, moe, norm_final)
    return out.reshape(batch, seq, d)
```

```python
import functools
import math

import jax
import jax.numpy as jnp
from jax import lax
from jax.experimental import pallas as pl
from jax.experimental.pallas import tpu as pltpu

F32 = jnp.float32
BF16 = jnp.bfloat16

LANES = 128
HEAD_DIM = 128
DIFF_DIM = HEAD_DIM // 2
DILATED_PATTERNS = ((128, 1), (512, 4), (2048, 16))
DILATED_HALO = 64
N_REL_BUCKETS = 32
REL_MAX_DISTANCE = 128
N_GROUPS = 4
EXPERTS_PER_GROUP = 8
N_EXPERTS = N_GROUPS * EXPERTS_PER_GROUP
MOE_BLOCK = 256
MOE_GROUP = 3
DMA_ISSUE_UNROLL = 8
RMS_EPS = 1e-6
SUBLN_EPS = 1e-5
NEG_BIG = -1e30
VMEM_LIMIT = 56 * 1024 * 1024

_NT = (((1,), (1,)), ((), ()))

_R_E0, _R_E1, _R_W0, _R_W1, _R_RANK0, _R_RANK1 = range(6)


def _t5_bucket(rel):
    nb = N_REL_BUCKETS // 2
    max_exact = nb // 2
    n = -rel
    ret = jnp.where(n < 0, nb, 0)
    n = jnp.abs(n)
    nf = jnp.maximum(n, 1).astype(jnp.float32)
    large = max_exact + (jnp.log(nf / max_exact) / math.log(REL_MAX_DISTANCE / max_exact)
                         * (nb - max_exact)).astype(jnp.int32)
    large = jnp.minimum(large, nb - 1)
    return (ret + jnp.where(n < max_exact, n, large)).astype(jnp.int32)


def _rel_bias(bias_table, rel):
    onehot = (_t5_bucket(rel).reshape(-1)[None, :] == jnp.arange(N_REL_BUCKETS)[:, None]).astype(F32)
    vals = jnp.dot(bias_table.astype(F32).T, onehot, precision=lax.Precision.HIGHEST)
    return vals.reshape((bias_table.shape[1],) + rel.shape)


def _params(*sem, **kwargs):
    return pltpu.CompilerParams(dimension_semantics=sem, vmem_limit_bytes=VMEM_LIMIT, **kwargs)


def _add_expert_outputs(x, route_ref, ya_ref, yb_ref):
    return x + route_ref[:, _R_W0:_R_W0 + 1] * ya_ref[...] + route_ref[:, _R_W1:_R_W1 + 1] * yb_ref[...]


def _expert_output_specs(tm, d, n_tok):
    blocks_per_plane, rem = divmod(n_tok + 2 * MOE_BLOCK, tm)
    assert rem == 0
    return [pl.BlockSpec((tm, d), lambda i, *_, p=p: (p * blocks_per_plane + i, 0)) for p in range(2)]


def _norm_proj_kernel(*refs, has_y, dils, n_split, eps):
    n_in = 6 if has_y else 3
    if has_y:
        x_ref, route_ref, ya_ref, yb_ref, g_ref, w_ref = refs[:n_in]
    else:
        x_ref, g_ref, w_ref = refs[:n_in]
    o_ref = refs[n_in]
    od_refs = refs[n_in + 1:n_in + 1 + len(dils)]
    rest = refs[n_in + 1 + len(dils):]
    xs_ref = rest[0] if has_y else None
    xn_ref, res_ref = rest[-2:]
    j = pl.program_id(1)

    @pl.when(j == 0)
    def _():
        x = x_ref[...]
        if has_y:
            x = _add_expert_outputs(x, route_ref, ya_ref, yb_ref)
            xs_ref[...] = x
        ms = jnp.mean(x * x, axis=-1, keepdims=True)
        xn_ref[...] = (x * lax.rsqrt(ms + eps) * g_ref[...]).astype(BF16)

    res = jnp.dot(xn_ref[...], w_ref[...], preferred_element_type=F32)
    o_ref[...] = res.astype(o_ref.dtype)

    @pl.when(j < n_split)
    def _():
        n_chunks, tm, _ = res_ref.shape
        tn = n_chunks * LANES
        for c in range(n_chunks):
            res_ref[c] = res[:, c * LANES:(c + 1) * LANES]
        for od_ref, dil in zip(od_refs, dils):
            for r in range(dil):
                for c in range(n_chunks):
                    col = r * tn + c * LANES
                    od_ref[:, col:col + LANES] = res_ref[c, pl.ds(r, tm // dil, stride=dil), :].astype(od_ref.dtype)


def _norm_proj(x, moe, g, w, layer, dils, n_split, *, tm=512, tn=1024):
    t, d = x.shape
    n = w.shape[2]
    has_y = moe is not None
    row = lambda i, j: (i, 0)
    in_specs = [pl.BlockSpec((tm, d), row)]
    args = [x]
    if has_y:
        route, planes = moe
        in_specs += [pl.BlockSpec((tm, LANES), row)]
        in_specs += _expert_output_specs(tm, d, t)
        args += [route, planes, planes]
    in_specs += [pl.BlockSpec((1, d), lambda i, j: (0, 0)), pl.BlockSpec((None, d, tn), lambda i, j: (layer, 0, j))]
    args += [g.reshape(1, d), w]
    out_shape = [jax.ShapeDtypeStruct((t, n), BF16)]
    out_specs = [pl.BlockSpec((tm, tn), lambda i, j: (i, j))]
    for dil in dils:
        out_shape.append(jax.ShapeDtypeStruct((t // dil, n_split * dil * tn), BF16))
        out_specs.append(pl.BlockSpec((tm // dil, dil * tn), lambda i, j: (i, jnp.minimum(j, n_split - 1))))
    if has_y:
        out_shape.append(jax.ShapeDtypeStruct((t, d), F32))
        out_specs.append(pl.BlockSpec((tm, d), row))
    res = pl.pallas_call(
        functools.partial(_norm_proj_kernel, has_y=has_y, dils=dils, n_split=n_split, eps=RMS_EPS),
        grid=(t // tm, n // tn),
        in_specs=in_specs, out_specs=out_specs, out_shape=out_shape,
        scratch_shapes=[pltpu.VMEM((tm, d), BF16), pltpu.VMEM((tn // LANES, tm, LANES), F32)],
        compiler_params=_params("arbitrary", "arbitrary"),
    )(*args)
    views = list(res[1:1 + len(dils)])
    return res[0], views, (res[-1] if has_y else x)


def _dilated_kernel(q_ref, kp_ref, kc_ref, kn_ref, vp_ref, vc_ref, vn_ref, bias_ref, o_ref, lse_ref,
                    *, tq, n_heads, seq_sub, scale):
    i = pl.program_id(2)
    halo = DILATED_HALO
    win = tq + 2 * halo
    key_m = i * tq - halo + lax.broadcasted_iota(jnp.int32, (tq, win), 1)
    in_range = (key_m >= 0) & (key_m < seq_sub)
    lane = lax.broadcasted_iota(jnp.int32, (tq, LANES), 1)
    lse_all = jnp.zeros((tq, LANES), F32)
    for h in range(n_heads):
        cs = slice(h * HEAD_DIM, (h + 1) * HEAD_DIM)
        q = q_ref[:, cs]
        k = jnp.concatenate([kp_ref[tq - halo:, cs], kc_ref[:, cs], kn_ref[:halo, cs]], axis=0)
        v = jnp.concatenate([vp_ref[tq - halo:, cs], vc_ref[:, cs], vn_ref[:halo, cs]], axis=0)
        s = lax.dot_general(q, k, _NT, preferred_element_type=F32) * scale + bias_ref[h]
        s = jnp.where(in_range, s, NEG_BIG)
        m = jnp.max(s, axis=-1, keepdims=True)
        p = jnp.exp(s - m)
        den = jnp.sum(p, axis=-1, keepdims=True)
        o_ref[:, cs] = jnp.dot(p.astype(BF16), v, preferred_element_type=F32) / den
        lse_all = jnp.where(lane == h, m + jnp.log(den), lse_all)
    lse_ref[...] = lse_all


def _dilated_bias(bias_table, n_heads, window, dilation, tq):
    radius = window // (2 * dilation)
    assert radius <= DILATED_HALO
    win = tq + 2 * DILATED_HALO
    dm = jnp.arange(win)[None, :] - DILATED_HALO - jnp.arange(tq)[:, None]
    b = _rel_bias(bias_table, dm * dilation)[:n_heads]
    return jnp.where((jnp.abs(dm) <= radius)[None], b, NEG_BIG)


def _dilated_attention(view, bias_table, batch, seq, n_heads, window, dilation, *, tq=128):
    t = view.shape[0] * dilation
    wa = n_heads * HEAD_DIM
    sub = seq // dilation
    nq = sub // tq
    bias = _dilated_bias(bias_table, n_heads, window, dilation, tq)

    def spec(which, shift):
        def imap(b, r, i):
            return (b * nq + jnp.clip(i + shift, 0, nq - 1), which * dilation + r)
        return pl.BlockSpec((tq, wa), imap)

    return pl.pallas_call(
        functools.partial(_dilated_kernel, tq=tq, n_heads=n_heads, seq_sub=sub, scale=HEAD_DIM ** -0.5),
        grid=(batch, dilation, nq),
        in_specs=[spec(0, 0), spec(1, -1), spec(1, 0), spec(1, 1), spec(2, -1), spec(2, 0), spec(2, 1),
                  pl.BlockSpec(bias.shape, lambda b, r, i: (0, 0, 0))],
        out_specs=[pl.BlockSpec((tq, wa), lambda b, r, i: (b * nq + i, r)),
                   pl.BlockSpec((tq, LANES), lambda b, r, i: (b * nq + i, r))],
        out_shape=[jax.ShapeDtypeStruct((t // dilation, dilation * wa), F32),
                   jax.ShapeDtypeStruct((t // dilation, dilation * LANES), F32)],
        compiler_params=_params("arbitrary", "arbitrary", "arbitrary"),
    )(view, view, view, view, view, view, view, bias)


def _mix_kernel(*refs, dils, n_heads, eps):
    n = len(dils)
    o_refs, l_refs = list(refs[:n]), list(refs[n:2 * n])
    g_ref, out_ref = refs[2 * n], refs[2 * n + 1]
    scratch = list(refs[2 * n + 2:])
    tm, wa = out_ref.shape
    def head_reader(ref):
        return lambda h: ref[:, h * HEAD_DIM:(h + 1) * HEAD_DIM]

    heads = [head_reader(o_ref) for o_ref in o_refs]
    for p, dil in enumerate(dils):
        if dil == 1:
            continue
        o_tok, l_tok = scratch.pop(0), scratch.pop(0)
        for r in range(dil):
            rows = pl.ds(r, tm // dil, stride=dil)
            for h in range(n_heads):
                o_tok[h, rows, :] = heads[p](r * n_heads + h)
            l_tok[rows, :] = l_refs[p][:, r * LANES:(r + 1) * LANES]
        heads[p], l_refs[p] = (lambda h, ref=o_tok: ref[h]), l_tok
    ls = [l_ref[...] for l_ref in l_refs]
    top = functools.reduce(jnp.maximum, ls)
    es = [jnp.exp(l - top) for l in ls]
    tot = functools.reduce(lambda a, b: a + b, es)
    ws = [e / tot for e in es]
    for h in range(n_heads):
        cs = slice(h * HEAD_DIM, (h + 1) * HEAD_DIM)
        mix = ws[0][:, h:h + 1] * heads[0](h)
        for p in range(1, n):
            mix = mix + ws[p][:, h:h + 1] * heads[p](h)
        ms = jnp.mean(mix * mix, axis=-1, keepdims=True)
        out_ref[:, cs] = (mix * lax.rsqrt(ms + eps) * g_ref[:, cs]).astype(out_ref.dtype)


def _dilated_mixture(outs, lses, dils, gain, *, tm=512):
    wa = gain.size
    t = outs[0].shape[0] * dils[0]
    n_heads = wa // HEAD_DIM
    row = lambda i: (i, 0)
    scratch = []
    for dil in dils:
        if dil > 1:
            scratch += [pltpu.VMEM((n_heads, tm, HEAD_DIM), F32), pltpu.VMEM((tm, LANES), F32)]
    return pl.pallas_call(
        functools.partial(_mix_kernel, dils=dils, n_heads=n_heads, eps=RMS_EPS),
        grid=(t // tm,),
        in_specs=[pl.BlockSpec((tm // dil, dil * wa), row) for dil in dils]
                 + [pl.BlockSpec((tm // dil, dil * LANES), row) for dil in dils]
                 + [pl.BlockSpec((1, wa), lambda i: (0, 0))],
        out_specs=pl.BlockSpec((tm, wa), row),
        out_shape=jax.ShapeDtypeStruct((t, wa), BF16),
        scratch_shapes=scratch,
        compiler_params=_params("arbitrary"),
    )(*outs, *lses, gain.reshape(1, wa))


def _diff_attn_kernel(lam_ref, q_ref, k_ref, v_ref, bias_ref, g_ref, o_ref, s_scr, p_scr, v1_scr,
                      *, tile, n_tiles, eps, post_scale):
    i = pl.program_id(2)
    lane = lax.broadcasted_iota(jnp.int32, (tile, HEAD_DIM), 1)
    qs = q_ref[...] * (DIFF_DIM ** -0.5)
    zero = jnp.zeros_like(qs)
    q_maps = (jnp.where(lane < DIFF_DIM, qs, zero), jnp.where(lane >= DIFF_DIM, qs, zero))
    chunks = tile // LANES

    def fold(op, acc, vals):
        for c in range(chunks):
            acc = op(acc, vals[:, c * LANES:(c + 1) * LANES])
        return acc

    def scores(mp, kt, run_max):
        ks = slice(kt * tile, (kt + 1) * tile)
        rel_class = jnp.clip(kt - i, -2, 2) + 2
        s = lax.dot_general(q_maps[mp], k_ref[ks, :], _NT, preferred_element_type=F32) + bias_ref[0, rel_class]
        s_scr[mp, :, ks] = s
        return fold(jnp.maximum, run_max, s)

    def weights(mp, kt, m):
        ks = slice(kt * tile, (kt + 1) * tile)
        p_scr[mp, :, ks] = jnp.exp(s_scr[mp, :, ks] - m).astype(BF16)

    def attend(mp):
        acc = jnp.dot(p_scr[mp], v1_scr[...], preferred_element_type=F32)
        return acc[:, :HEAD_DIM] / acc[:, HEAD_DIM:]

    @pl.when(i == 0)
    def _():
        v1_scr[:, :HEAD_DIM] = v_ref[...]
        v1_scr[:, HEAD_DIM:] = jnp.ones((v1_scr.shape[0], HEAD_DIM), BF16)

    neg_inf = jnp.full((tile, LANES), -jnp.inf, F32)
    run_max = neg_inf
    for kt in range(n_tiles):
        run_max = scores(0, kt, run_max)
    m0 = jnp.max(run_max, axis=-1, keepdims=True)
    run_max = neg_inf
    for kt in range(n_tiles):
        weights(0, kt, m0)
        run_max = scores(1, kt, run_max)
    out0 = attend(0)
    m1 = jnp.max(run_max, axis=-1, keepdims=True)
    for kt in range(n_tiles):
        weights(1, kt, m1)
    o = out0 - lam_ref[0] * attend(1)
    ms = jnp.mean(o * o, axis=-1, keepdims=True)
    o_ref[...] = ((o * lax.rsqrt(ms + eps) * g_ref[0]) * post_scale).astype(o_ref.dtype)


def _diff_attention(proj, bias_table, gain, lam, lam_init, batch, seq, n_heads_a, n_heads_b, *, tile=256):
    t, pw = proj.shape
    n_tiles = seq // tile
    q0 = 3 * n_heads_a
    k0 = q0 + n_heads_b
    v0 = k0 + n_heads_b
    assert tile + 1 >= REL_MAX_DISTANCE
    rel = (jnp.arange(5)[:, None, None] - 2) * tile + jnp.arange(tile)[None, None, :] - jnp.arange(tile)[None, :, None]
    bias = _rel_bias(bias_table, rel)[n_heads_a:]
    return pl.pallas_call(
        functools.partial(_diff_attn_kernel, tile=tile, n_tiles=n_tiles, eps=SUBLN_EPS,
                          post_scale=1.0 - lam_init),
        grid=(batch, n_heads_b, n_tiles),
        in_specs=[pl.BlockSpec(memory_space=pltpu.SMEM),
                  pl.BlockSpec((tile, HEAD_DIM), lambda b, h, i: (b * n_tiles + i, q0 + h)),
                  pl.BlockSpec((seq, HEAD_DIM), lambda b, h, i: (b, k0 + h)),
                  pl.BlockSpec((seq, HEAD_DIM), lambda b, h, i: (b, v0 + h)),
                  pl.BlockSpec((1, 5, tile, tile), lambda b, h, i: (h, 0, 0, 0)),
                  pl.BlockSpec((1, 1, HEAD_DIM), lambda b, h, i: (h, 0, 0))],
        out_specs=pl.BlockSpec((tile, HEAD_DIM), lambda b, h, i: (b * n_tiles + i, h)),
        out_shape=jax.ShapeDtypeStruct((t, n_heads_b * HEAD_DIM), BF16),
        scratch_shapes=[pltpu.VMEM((2, tile, seq), F32), pltpu.VMEM((2, tile, seq), BF16),
                        pltpu.VMEM((seq, 2 * HEAD_DIM), BF16)],
        compiler_params=_params("arbitrary", "arbitrary", "arbitrary"),
    )(lam.reshape(1), proj, proj, proj, bias, gain.reshape(n_heads_b, 1, HEAD_DIM))


def _out_router_kernel(a_ref, b_ref, x_ref, wo_ref, g_ref, wrh_ref, wrl_ref,
                       x1_ref, hn_ref, route_ref, cnt_ref, *, tm, wa, eps):
    @pl.when(pl.program_id(0) == 0)
    def _():
        cnt_ref[...] = jnp.zeros(cnt_ref.shape, F32)

    y = jnp.dot(a_ref[...], wo_ref[:wa, :], preferred_element_type=F32)
    y = y + jnp.dot(b_ref[...], wo_ref[wa:, :], preferred_element_type=F32)
    x1 = x_ref[...] + y
    x1_ref[...] = x1
    ms = jnp.mean(x1 * x1, axis=-1, keepdims=True)
    hn = x1 * lax.rsqrt(ms + eps) * g_ref[...]
    hn_ref[...] = hn

    hi = hn.astype(BF16)
    lo = (hn - hi.astype(F32)).astype(BF16)
    logits = (jnp.dot(hi, wrh_ref[...], preferred_element_type=F32)
              + jnp.dot(lo, wrh_ref[...], preferred_element_type=F32)
              + jnp.dot(hi, wrl_ref[...], preferred_element_type=F32))

    lane = lax.broadcasted_iota(jnp.int32, (tm, LANES), 1)
    neg_inf = jnp.float32(-jnp.inf)

    def first_max(vals):
        top = jnp.max(vals, axis=-1, keepdims=True)
        idx = jnp.min(jnp.where(vals == top, lane, LANES), axis=-1, keepdims=True)
        return top, idx

    is_grp = lane < N_GROUPS
    g_top, g_idx = first_max(jnp.where(is_grp, logits, neg_inf))
    g_sum = jnp.sum(jnp.where(is_grp, jnp.exp(logits - g_top), 0.0), axis=-1, keepdims=True)
    p_grp = 1.0 / g_sum
    e_lo = N_GROUPS + EXPERTS_PER_GROUP * g_idx
    e_vals = jnp.where((lane >= e_lo) & (lane < e_lo + EXPERTS_PER_GROUP), logits, neg_inf)
    top1, i1 = first_max(e_vals)
    top2, i2 = first_max(jnp.where(lane == i1, neg_inf, e_vals))
    b2 = jnp.exp(top2 - top1)
    w0 = p_grp / (1.0 + b2)
    w1 = p_grp * b2 / (1.0 + b2)

    oh0 = (lane == i1).astype(F32)
    oh1 = (lane == i2).astype(F32)
    both = oh0 + oh1
    r_i = lax.broadcasted_iota(jnp.int32, (tm, tm), 0)
    c_i = lax.broadcasted_iota(jnp.int32, (tm, tm), 1)
    before = jnp.where(r_i > c_i, 1.0, 0.0).astype(BF16)
    base = cnt_ref[...] + jnp.dot(before, both.astype(BF16), preferred_element_type=F32)
    rank0 = jnp.sum(base * oh0, axis=-1, keepdims=True)
    rank1 = jnp.sum(base * oh1, axis=-1, keepdims=True)
    cnt_ref[...] = cnt_ref[...] + jnp.sum(both, axis=0, keepdims=True)

    fields = {_R_E0: (i1 - N_GROUPS).astype(F32), _R_E1: (i2 - N_GROUPS).astype(F32),
              _R_W0: w0, _R_W1: w1, _R_RANK0: rank0, _R_RANK1: rank1}
    route = jnp.zeros((tm, LANES), F32)
    for n, f in fields.items():
        route = jnp.where(lane == n, f, route)
    route_ref[...] = route


def _out_router(out_a, out_b, x, w_out, layer, g, wr_hi, wr_lo, *, tm=256):
    t, d = x.shape
    wa = out_a.shape[1]
    wb = out_b.shape[1]
    row = lambda i: (i, 0)
    const = lambda i: (0, 0)
    return pl.pallas_call(
        functools.partial(_out_router_kernel, tm=tm, wa=wa, eps=RMS_EPS),
        grid=(t // tm,),
        in_specs=[pl.BlockSpec((tm, wa), row), pl.BlockSpec((tm, wb), row), pl.BlockSpec((tm, d), row),
                  pl.BlockSpec((None, wa + wb, d), lambda i: (layer, 0, 0)), pl.BlockSpec((1, d), const),
                  pl.BlockSpec((d, LANES), const), pl.BlockSpec((d, LANES), const)],
        out_specs=[pl.BlockSpec((tm, d), row), pl.BlockSpec((tm, d), row),
                   pl.BlockSpec((tm, LANES), row), pl.BlockSpec((1, LANES), const)],
        out_shape=[jax.ShapeDtypeStruct((t, d), F32), jax.ShapeDtypeStruct((t, d), F32),
                   jax.ShapeDtypeStruct((t, LANES), F32), jax.ShapeDtypeStruct((1, LANES), F32)],
        compiler_params=_params("arbitrary"),
    )(out_a, out_b, x, w_out, g.reshape(1, d), wr_hi, wr_lo)


def _moe_kernel(nitems_ref, iexp_ref, icnt_ref, tok_ref, nxt_ref, dst_ref, hn_hbm, wg_ref, wu_ref, wd_ref, out_hbm,
                xbuf, ybuf, gsem, ssem, *, rows, dump_rows):
    del iexp_ref
    it = pl.program_id(0)
    first_half = pl.program_id(1) == 0
    n_items = nitems_ref[0]
    cnt = icnt_ref[it]
    buf = lax.rem(it, 2)
    sub = DMA_ISSUE_UNROLL

    def issue_rows(first_row, n_rows, copy_row):
        def body(g, carry):
            base = pl.multiple_of(first_row + g * sub, sub)
            for u in range(sub):
                copy_row(base + u).start()
            return carry
        lax.fori_loop(0, n_rows // sub, body, 0)

    def start_gather(idx_ref, s, n_blocks):
        issue_rows(0, n_blocks * rows, lambda r: pltpu.make_async_copy(
            hn_hbm.at[pl.ds(idx_ref[0, 0, r], 1)], xbuf.at[s, pl.ds(r, 1)], gsem.at[s]))

    def start_scatter(j):
        issue_rows(j * rows, rows, lambda r: pltpu.make_async_copy(
            ybuf.at[pl.ds(r, 1)], out_hbm.at[pl.ds(dst_ref[0, 0, r], 1)], ssem.at[0]))

    def wait_blocks(n_blocks, make_block_copy):
        def body(j, carry):
            make_block_copy().wait()
            return carry
        lax.fori_loop(0, n_blocks, body, 0)

    def wait_gather(s, n_blocks):
        wait_blocks(n_blocks, lambda: pltpu.make_async_copy(
            hn_hbm.at[pl.ds(0, rows)], xbuf.at[s, pl.ds(0, rows)], gsem.at[s]))

    def wait_scatter(n_blocks):
        wait_blocks(n_blocks, lambda: pltpu.make_async_copy(
            ybuf.at[pl.ds(0, rows)], out_hbm.at[pl.ds(0, rows)], ssem.at[0]))

    def ffn_half(j):
        x = xbuf[buf, pl.ds(pl.multiple_of(j * rows, rows), rows), :]
        gate = jnp.dot(x, wg_ref[0], preferred_element_type=F32)
        up = jnp.dot(x, wu_ref[0], preferred_element_type=F32)
        hid = gate * (1.0 / (1.0 + jnp.exp(-gate))) * up
        return jnp.dot(hid, wd_ref[0], preferred_element_type=F32)

    def block_rows(j):
        return pl.ds(pl.multiple_of(j * rows, rows), rows)

    @pl.when(jnp.logical_and(it < n_items, first_half))
    def _():
        @pl.when(it == 0)
        def _():
            ybuf[pl.ds(0, rows), :] = jnp.zeros((rows, ybuf.shape[1]), F32)
            for start in dump_rows:
                dump = pltpu.make_async_copy(ybuf.at[pl.ds(0, rows)], out_hbm.at[pl.ds(start, rows)], ssem.at[0])
                dump.start()
                dump.wait()
            start_gather(tok_ref, buf, cnt)

        wait_gather(buf, cnt)

        @pl.when(it >= 1)
        def _():
            wait_scatter(icnt_ref[jnp.maximum(it - 1, 0)])

        def body(j, carry):
            ybuf[block_rows(j), :] = ffn_half(j)
            return carry
        lax.fori_loop(0, cnt, body, 0)

    @pl.when(jnp.logical_and(it < n_items, jnp.logical_not(first_half)))
    def _():
        @pl.when(it + 1 < n_items)
        def _():
            start_gather(nxt_ref, 1 - buf, icnt_ref[jnp.minimum(it + 1, icnt_ref.shape[0] - 1)])

        def body(j, carry):
            ybuf[block_rows(j), :] = ybuf[block_rows(j), :] + ffn_half(j)
            start_scatter(j)
            return carry
        lax.fori_loop(0, cnt, body, 0)

        @pl.when(it + 1 >= n_items)
        def _():
            wait_scatter(cnt)


def _plane_rows(n_tok):
    return n_tok + 2 * MOE_BLOCK


def _moe(hn, item_tok, item_dst, item_exp, item_cnt, n_items, w_gate, w_up, w_down, layer):
    t, d = hn.shape
    max_items = item_tok.shape[0]
    rows = MOE_BLOCK
    item_rows = MOE_GROUP * rows
    ffn = w_gate.shape[3]
    plane = _plane_rows(t)
    dump_rows = tuple(p * plane + t + s * rows for p in range(2) for s in range(2))
    half_ffn = ffn // 2

    def half(it, h, ni):
        return jnp.where(it < ni[0], h, 1)

    grid_spec = pltpu.PrefetchScalarGridSpec(
        num_scalar_prefetch=3,
        grid=(max_items, 2),
        in_specs=[
            pl.BlockSpec((1, 1, item_rows), lambda it, h, ni, ie, ic: (it, 0, 0), memory_space=pltpu.SMEM),
            pl.BlockSpec((1, 1, item_rows), lambda it, h, ni, ie, ic: (jnp.minimum(it + 1, max_items - 1), 0, 0),
                         memory_space=pltpu.SMEM),
            pl.BlockSpec((1, 1, item_rows), lambda it, h, ni, ie, ic: (it, 0, 0), memory_space=pltpu.SMEM),
            pl.BlockSpec(memory_space=pl.ANY),
            pl.BlockSpec((None, 1, d, half_ffn), lambda it, h, ni, ie, ic: (layer, ie[it], 0, half(it, h, ni))),
            pl.BlockSpec((None, 1, d, half_ffn), lambda it, h, ni, ie, ic: (layer, ie[it], 0, half(it, h, ni))),
            pl.BlockSpec((None, 1, half_ffn, d), lambda it, h, ni, ie, ic: (layer, ie[it], half(it, h, ni), 0)),
        ],
        out_specs=pl.BlockSpec(memory_space=pl.ANY),
        scratch_shapes=[pltpu.VMEM((2, item_rows, d), F32), pltpu.VMEM((item_rows, d), F32),
                        pltpu.SemaphoreType.DMA((2,)), pltpu.SemaphoreType.DMA((1,))],
    )
    return pl.pallas_call(
        functools.partial(_moe_kernel, rows=rows, dump_rows=dump_rows),
        grid_spec=grid_spec,
        out_shape=jax.ShapeDtypeStruct((2 * plane, d), F32),
        compiler_params=_params("arbitrary", "arbitrary", disable_bounds_checks=True),
    )(n_items, item_exp, item_cnt, item_tok, item_tok, item_dst, hn, w_gate, w_up, w_down)


def _dispatch(route, counts, n_tok):
    rows = MOE_BLOCK
    group = MOE_GROUP
    plane = _plane_rows(n_tok)
    n_blocks = -(-(2 * n_tok) // rows) + N_EXPERTS
    max_items = N_EXPERTS + n_blocks // group
    e = route[:, _R_E0:_R_E1 + 1].astype(jnp.int32)
    rank = route[:, _R_RANK0:_R_RANK1 + 1].astype(jnp.int32)
    cnt = counts[0, N_GROUPS:N_GROUPS + N_EXPERTS].astype(jnp.int32)
    padded = (cnt + rows - 1) // rows * rows
    pend = jnp.cumsum(padded)
    pstart = pend - padded
    dest = (pstart[e] + rank).reshape(-1)
    code = (2 * jnp.arange(n_tok, dtype=jnp.int32)[:, None] + jnp.arange(2, dtype=jnp.int32)[None, :]).reshape(-1)
    row_src = jnp.full((n_blocks * rows,), -1, jnp.int32).at[dest].set(code)
    row = jnp.arange(n_blocks * rows, dtype=jnp.int32)
    row_tok = jnp.maximum(row_src, 0) // 2
    row_dst = jnp.where(row_src >= 0, (row_src % 2) * plane + row_src // 2, n_tok + row % rows)
    n_blk = padded // rows
    first_blk = pstart // rows
    items_end = jnp.cumsum((n_blk + group - 1) // group)
    items_start = items_end - (n_blk + group - 1) // group
    n_items = items_end[-1].astype(jnp.int32)
    it = jnp.arange(max_items, dtype=jnp.int32)
    item_exp = jnp.minimum(jnp.sum(items_end[None, :] <= it[:, None], axis=1), N_EXPERTS - 1).astype(jnp.int32)
    run = it - items_start[item_exp]
    item_blk = first_blk[item_exp] + run * group
    item_cnt = jnp.clip(n_blk[item_exp] - run * group, 0, group)
    live = it < n_items
    item_cnt = jnp.where(live, item_cnt, 0).astype(jnp.int32)
    item_exp = jnp.where(live, item_exp, item_exp[jnp.maximum(n_items - 1, 0)])
    blk_ids = jnp.clip(item_blk[:, None] + jnp.arange(group, dtype=jnp.int32)[None, :], 0, n_blocks - 1)
    item_tok = row_tok.reshape(n_blocks, rows)[blk_ids].reshape(max_items, 1, group * rows)
    item_dst = row_dst.reshape(n_blocks, rows)[blk_ids].reshape(max_items, 1, group * rows)
    return item_tok, item_dst, item_exp, item_cnt, n_items.reshape(1)


def _final_kernel(x_ref, route_ref, ya_ref, yb_ref, g_ref, o_ref, *, eps):
    x = _add_expert_outputs(x_ref[...], route_ref, ya_ref, yb_ref)
    ms = jnp.mean(x * x, axis=-1, keepdims=True)
    o_ref[...] = x * lax.rsqrt(ms + eps) * g_ref[...]


def _final_norm(x, moe, g, *, tm=512):
    t, d = x.shape
    route, planes = moe
    row = lambda i: (i, 0)
    return pl.pallas_call(
        functools.partial(_final_kernel, eps=RMS_EPS),
        grid=(t // tm,),
        in_specs=[pl.BlockSpec((tm, d), row), pl.BlockSpec((tm, LANES), row)]
                 + _expert_output_specs(tm, d, t)
                 + [pl.BlockSpec((1, d), lambda i: (0, 0))],
        out_specs=pl.BlockSpec((tm, d), row),
        out_shape=jax.ShapeDtypeStruct((t, d), F32),
        compiler_params=_params("arbitrary"),
    )(x, route, planes, planes, g.reshape(1, d))


def kernel(x, rel_bias, norm_attn, w_in, gain_a, gain_b, lam_q1, lam_k1, lam_q2, lam_k2, w_out, norm_ffn,
           w_router_group, w_router_expert, w_gate, w_up, w_down, norm_final):
    batch, seq, d = x.shape
    depth = w_in.shape[0]
    n_heads_a = gain_a.shape[1]
    n_heads_b = gain_b.shape[1]
    t = batch * seq
    xs = x.reshape(t, d)
    w_in, w_out = w_in.astype(BF16), w_out.astype(BF16)
    dils = tuple(dil for _, dil in DILATED_PATTERNS)
    extra_dils = tuple(dil for dil in dils if dil > 1)
    moe = None
    for l in range(depth):
        proj, views, xs = _norm_proj(xs, moe, norm_attn[l], w_in, l, extra_dils, 3)
        views = dict(zip(extra_dils, views))
        parts = [_dilated_attention(views.get(dil, proj), rel_bias, batch, seq, n_heads_a, w, dil)
                 for (w, dil) in DILATED_PATTERNS]
        out_a = _dilated_mixture([p[0] for p in parts], [p[1] for p in parts], dils, gain_a[l])

        lam_init = 0.8 - 0.6 * math.exp(-0.3 * l)
        lam = (jnp.exp(jnp.sum(lam_q1[l].astype(F32) * lam_k1[l].astype(F32)))
               - jnp.exp(jnp.sum(lam_q2[l].astype(F32) * lam_k2[l].astype(F32))) + lam_init)
        out_b = _diff_attention(proj, rel_bias, gain_b[l], lam, lam_init, batch, seq, n_heads_a, n_heads_b)

        w_r = jnp.concatenate([w_router_group[l], w_router_expert[l].transpose(1, 0, 2).reshape(d, N_EXPERTS)], axis=1)
        w_r = jnp.pad(w_r.astype(F32), ((0, 0), (0, LANES - w_r.shape[1])))
        wr_hi = w_r.astype(BF16)
        wr_lo = (w_r - wr_hi.astype(F32)).astype(BF16)
        xs, hn, route, counts = _out_router(out_a, out_b, xs, w_out, l, norm_ffn[l], wr_hi, wr_lo)

        planes = _moe(hn, *_dispatch(route, counts, t), w_gate, w_up, w_down, l)
        moe = (route, planes)
    out = _final_norm(xs, moe, norm_final)
    return out.reshape(batch, seq, d)
```

```python
import functools
import math

import jax
import jax.numpy as jnp
from jax import lax
from jax.experimental import pallas as pl
from jax.experimental.pallas import tpu as pltpu

F32 = jnp.float32
BF16 = jnp.bfloat16

LANES = 128
HEAD_DIM = 128
DIFF_DIM = HEAD_DIM // 2
DILATED_PATTERNS = ((128, 1), (512, 4), (2048, 16))
DILATED_HALO = 64
N_REL_BUCKETS = 32
REL_MAX_DISTANCE = 128
N_GROUPS = 4
EXPERTS_PER_GROUP = 8
N_EXPERTS = N_GROUPS * EXPERTS_PER_GROUP
MOE_BLOCK = 256
MOE_GROUP = 3
SUBLANES = 8
RMS_EPS = 1e-6
SUBLN_EPS = 1e-5
NEG_BIG = -1e30
VMEM_LIMIT = 56 * 1024 * 1024

_NT = (((1,), (1,)), ((), ()))

_R_E0, _R_E1, _R_W0, _R_W1, _R_RANK0, _R_RANK1 = range(6)


def _t5_bucket(rel):
    nb = N_REL_BUCKETS // 2
    max_exact = nb // 2
    n = -rel
    ret = jnp.where(n < 0, nb, 0)
    n = jnp.abs(n)
    nf = jnp.maximum(n, 1).astype(jnp.float32)
    large = max_exact + (jnp.log(nf / max_exact) / math.log(REL_MAX_DISTANCE / max_exact)
                         * (nb - max_exact)).astype(jnp.int32)
    large = jnp.minimum(large, nb - 1)
    return (ret + jnp.where(n < max_exact, n, large)).astype(jnp.int32)


def _rel_bias(bias_table, rel):
    onehot = (_t5_bucket(rel).reshape(-1)[None, :] == jnp.arange(N_REL_BUCKETS)[:, None]).astype(F32)
    vals = jnp.dot(bias_table.astype(F32).T, onehot, precision=lax.Precision.HIGHEST)
    return vals.reshape((bias_table.shape[1],) + rel.shape)


def _params(*sem, **kwargs):
    return pltpu.CompilerParams(dimension_semantics=sem, vmem_limit_bytes=VMEM_LIMIT, **kwargs)


def _add_expert_outputs(x, route_ref, ya_ref, yb_ref):
    return x + route_ref[:, _R_W0:_R_W0 + 1] * ya_ref[...] + route_ref[:, _R_W1:_R_W1 + 1] * yb_ref[...]


def _expert_output_specs(tm, d, n_tok):
    blocks_per_plane, rem = divmod(n_tok + 2 * MOE_BLOCK, tm)
    assert rem == 0
    return [pl.BlockSpec((tm, d), lambda i, *_, p=p: (p * blocks_per_plane + i, 0)) for p in range(2)]


def _norm_proj_kernel(*refs, has_y, dils, n_split, eps):
    n_in = 6 if has_y else 3
    if has_y:
        x_ref, route_ref, ya_ref, yb_ref, g_ref, w_ref = refs[:n_in]
    else:
        x_ref, g_ref, w_ref = refs[:n_in]
    o_ref = refs[n_in]
    od_refs = refs[n_in + 1:n_in + 1 + len(dils)]
    rest = refs[n_in + 1 + len(dils):]
    xs_ref = rest[0] if has_y else None
    xn_ref, res_ref = rest[-2:]
    j = pl.program_id(1)

    @pl.when(j == 0)
    def _():
        x = x_ref[...]
        if has_y:
            x = _add_expert_outputs(x, route_ref, ya_ref, yb_ref)
            xs_ref[...] = x
        ms = jnp.mean(x * x, axis=-1, keepdims=True)
        xn_ref[...] = (x * lax.rsqrt(ms + eps) * g_ref[...]).astype(BF16)

    res = jnp.dot(xn_ref[...], w_ref[...], preferred_element_type=F32)
    o_ref[...] = res.astype(o_ref.dtype)

    @pl.when(j < n_split)
    def _():
        n_chunks, tm, _ = res_ref.shape
        tn = n_chunks * LANES
        for c in range(n_chunks):
            res_ref[c] = res[:, c * LANES:(c + 1) * LANES]
        for od_ref, dil in zip(od_refs, dils):
            for r in range(dil):
                for c in range(n_chunks):
                    col = r * tn + c * LANES
                    od_ref[:, col:col + LANES] = res_ref[c, pl.ds(r, tm // dil, stride=dil), :].astype(od_ref.dtype)


def _norm_proj(x, moe, g, w, layer, dils, n_split, *, tm=512, tn=1024):
    t, d = x.shape
    n = w.shape[2]
    has_y = moe is not None
    row = lambda i, j: (i, 0)
    in_specs = [pl.BlockSpec((tm, d), row)]
    args = [x]
    if has_y:
        route, planes = moe
        in_specs += [pl.BlockSpec((tm, LANES), row)]
        in_specs += _expert_output_specs(tm, d, t)
        args += [route, planes, planes]
    in_specs += [pl.BlockSpec((1, d), lambda i, j: (0, 0)), pl.BlockSpec((None, d, tn), lambda i, j: (layer, 0, j))]
    args += [g.reshape(1, d), w]
    out_shape = [jax.ShapeDtypeStruct((t, n), BF16)]
    out_specs = [pl.BlockSpec((tm, tn), lambda i, j: (i, j))]
    for dil in dils:
        out_shape.append(jax.ShapeDtypeStruct((t // dil, n_split * dil * tn), BF16))
        out_specs.append(pl.BlockSpec((tm // dil, dil * tn), lambda i, j: (i, jnp.minimum(j, n_split - 1))))
    if has_y:
        out_shape.append(jax.ShapeDtypeStruct((t, d), F32))
        out_specs.append(pl.BlockSpec((tm, d), row))
    res = pl.pallas_call(
        functools.partial(_norm_proj_kernel, has_y=has_y, dils=dils, n_split=n_split, eps=RMS_EPS),
        grid=(t // tm, n // tn),
        in_specs=in_specs, out_specs=out_specs, out_shape=out_shape,
        scratch_shapes=[pltpu.VMEM((tm, d), BF16), pltpu.VMEM((tn // LANES, tm, LANES), F32)],
        compiler_params=_params("arbitrary", "arbitrary"),
    )(*args)
    views = list(res[1:1 + len(dils)])
    return res[0], views, (res[-1] if has_y else x)


def _dilated_kernel(q_ref, kp_ref, kc_ref, kn_ref, vp_ref, vc_ref, vn_ref, bias_ref, o_ref, lse_ref,
                    *, tq, n_sub, n_heads, seq_sub, scale):
    i = pl.program_id(2)
    halo = DILATED_HALO
    win = tq + 2 * halo
    lane = lax.broadcasted_iota(jnp.int32, (tq, LANES), 1)
    col = lax.broadcasted_iota(jnp.int32, (tq, win), 1)
    in_range, lse_all = [], []
    for t in range(n_sub):
        key_m = (i * n_sub + t) * tq - halo + col
        in_range.append((key_m >= 0) & (key_m < seq_sub))
        lse_all.append(jnp.zeros((tq, LANES), F32))
    for h in range(n_heads):
        cs = slice(h * HEAD_DIM, (h + 1) * HEAD_DIM)
        k = jnp.concatenate([kp_ref[:, cs], kc_ref[:, cs], kn_ref[:, cs]], axis=0)
        v = jnp.concatenate([vp_ref[:, cs], vc_ref[:, cs], vn_ref[:, cs]], axis=0)
        for t in range(n_sub):
            rows = slice(t * tq, (t + 1) * tq)
            keys = slice(t * tq, t * tq + win)
            s = lax.dot_general(q_ref[rows, cs], k[keys], _NT, preferred_element_type=F32) * scale + bias_ref[h]
            s = jnp.where(in_range[t], s, NEG_BIG)
            m = jnp.max(s, axis=-1, keepdims=True)
            p = jnp.exp(s - m)
            den = jnp.sum(p, axis=-1, keepdims=True)
            o_ref[rows, cs] = jnp.dot(p.astype(BF16), v[keys], preferred_element_type=F32) / den
            lse_all[t] = jnp.where(lane == h, m + jnp.log(den), lse_all[t])
    for t in range(n_sub):
        lse_ref[t * tq:(t + 1) * tq, :] = lse_all[t]


def _dilated_bias(bias_table, n_heads, window, dilation, tq):
    radius = window // (2 * dilation)
    assert radius <= DILATED_HALO
    win = tq + 2 * DILATED_HALO
    dm = jnp.arange(win)[None, :] - DILATED_HALO - jnp.arange(tq)[:, None]
    b = _rel_bias(bias_table, dm * dilation)[:n_heads]
    return jnp.where((jnp.abs(dm) <= radius)[None], b, NEG_BIG)


def _dilated_attention(view, bias_table, batch, seq, n_heads, window, dilation, *, tq=128, n_sub=2):
    t = view.shape[0] * dilation
    wa = n_heads * HEAD_DIM
    sub = seq // dilation
    n_sub = min(n_sub, sub // tq)
    step = n_sub * tq
    nq = sub // step
    halo = DILATED_HALO
    per_step = step // halo
    n_halo = sub // halo
    bias = _dilated_bias(bias_table, n_heads, window, dilation, tq)

    def main(which):
        return pl.BlockSpec((step, wa), lambda b, r, i: (b * nq + i, which * dilation + r))

    def edge(which, side):
        def imap(b, r, i):
            blk = i * per_step - 1 if side < 0 else (i + 1) * per_step
            return (b * n_halo + jnp.clip(blk, 0, n_halo - 1), which * dilation + r)
        return pl.BlockSpec((halo, wa), imap)

    return pl.pallas_call(
        functools.partial(_dilated_kernel, tq=tq, n_sub=n_sub, n_heads=n_heads, seq_sub=sub,
                          scale=HEAD_DIM ** -0.5),
        grid=(batch, dilation, nq),
        in_specs=[main(0), edge(1, -1), main(1), edge(1, 1), edge(2, -1), main(2), edge(2, 1),
                  pl.BlockSpec(bias.shape, lambda b, r, i: (0, 0, 0))],
        out_specs=[pl.BlockSpec((step, wa), lambda b, r, i: (b * nq + i, r)),
                   pl.BlockSpec((step, LANES), lambda b, r, i: (b * nq + i, r))],
        out_shape=[jax.ShapeDtypeStruct((t // dilation, dilation * wa), F32),
                   jax.ShapeDtypeStruct((t // dilation, dilation * LANES), F32)],
        compiler_params=_params("arbitrary", "arbitrary", "arbitrary"),
    )(view, view, view, view, view, view, view, bias)


def _mix_kernel(*refs, dils, n_heads, eps):
    n = len(dils)
    o_refs, l_refs = list(refs[:n]), list(refs[n:2 * n])
    g_ref, out_ref = refs[2 * n], refs[2 * n + 1]
    scratch = list(refs[2 * n + 2:])
    tm, wa = out_ref.shape
    def head_reader(ref):
        return lambda h: ref[:, h * HEAD_DIM:(h + 1) * HEAD_DIM]

    heads = [head_reader(o_ref) for o_ref in o_refs]
    for p, dil in enumerate(dils):
        if dil == 1:
            continue
        o_tok, l_tok = scratch.pop(0), scratch.pop(0)
        for r in range(dil):
            rows = pl.ds(r, tm // dil, stride=dil)
            for h in range(n_heads):
                o_tok[h, rows, :] = heads[p](r * n_heads + h)
            l_tok[rows, :] = l_refs[p][:, r * LANES:(r + 1) * LANES]
        heads[p], l_refs[p] = (lambda h, ref=o_tok: ref[h]), l_tok
    ls = [l_ref[...] for l_ref in l_refs]
    top = functools.reduce(jnp.maximum, ls)
    es = [jnp.exp(l - top) for l in ls]
    tot = functools.reduce(lambda a, b: a + b, es)
    ws = [e / tot for e in es]
    for h in range(n_heads):
        cs = slice(h * HEAD_DIM, (h + 1) * HEAD_DIM)
        mix = ws[0][:, h:h + 1] * heads[0](h)
        for p in range(1, n):
            mix = mix + ws[p][:, h:h + 1] * heads[p](h)
        ms = jnp.mean(mix * mix, axis=-1, keepdims=True)
        out_ref[:, cs] = (mix * lax.rsqrt(ms + eps) * g_ref[:, cs]).astype(out_ref.dtype)


def _dilated_mixture(outs, lses, dils, gain, *, tm=512):
    wa = gain.size
    t = outs[0].shape[0] * dils[0]
    n_heads = wa // HEAD_DIM
    row = lambda i: (i, 0)
    scratch = []
    for dil in dils:
        if dil > 1:
            scratch += [pltpu.VMEM((n_heads, tm, HEAD_DIM), F32), pltpu.VMEM((tm, LANES), F32)]
    return pl.pallas_call(
        functools.partial(_mix_kernel, dils=dils, n_heads=n_heads, eps=RMS_EPS),
        grid=(t // tm,),
        in_specs=[pl.BlockSpec((tm // dil, dil * wa), row) for dil in dils]
                 + [pl.BlockSpec((tm // dil, dil * LANES), row) for dil in dils]
                 + [pl.BlockSpec((1, wa), lambda i: (0, 0))],
        out_specs=pl.BlockSpec((tm, wa), row),
        out_shape=jax.ShapeDtypeStruct((t, wa), BF16),
        scratch_shapes=scratch,
        compiler_params=_params("arbitrary"),
    )(*outs, *lses, gain.reshape(1, wa))


def _diff_attn_kernel(lam_ref, q_ref, k_ref, v_ref, bias_ref, g_ref, o_ref, s_scr, p_scr, v1_scr,
                      *, tile, n_tiles, eps, post_scale):
    i = pl.program_id(2)
    lane = lax.broadcasted_iota(jnp.int32, (tile, HEAD_DIM), 1)
    qs = q_ref[...] * (DIFF_DIM ** -0.5)
    zero = jnp.zeros_like(qs)
    q_maps = (jnp.where(lane < DIFF_DIM, qs, zero), jnp.where(lane >= DIFF_DIM, qs, zero))
    chunks = tile // LANES

    def fold(op, acc, vals):
        for c in range(chunks):
            acc = op(acc, vals[:, c * LANES:(c + 1) * LANES])
        return acc

    def scores(mp, kt, run_max):
        ks = slice(kt * tile, (kt + 1) * tile)
        rel_class = jnp.clip(kt - i, -2, 2) + 2
        s = lax.dot_general(q_maps[mp], k_ref[ks, :], _NT, preferred_element_type=F32) + bias_ref[0, rel_class]
        s_scr[mp, :, ks] = s
        return fold(jnp.maximum, run_max, s)

    def weights(mp, kt, m):
        ks = slice(kt * tile, (kt + 1) * tile)
        p_scr[mp, :, ks] = jnp.exp(s_scr[mp, :, ks] - m).astype(BF16)

    def attend(mp):
        acc = jnp.dot(p_scr[mp], v1_scr[...], preferred_element_type=F32)
        return acc[:, :HEAD_DIM] / acc[:, HEAD_DIM:]

    @pl.when(i == 0)
    def _():
        v1_scr[:, :HEAD_DIM] = v_ref[...]
        v1_scr[:, HEAD_DIM:] = jnp.ones((v1_scr.shape[0], HEAD_DIM), BF16)

    neg_inf = jnp.full((tile, LANES), -jnp.inf, F32)
    run_max = neg_inf
    for kt in range(n_tiles):
        run_max = scores(0, kt, run_max)
    m0 = jnp.max(run_max, axis=-1, keepdims=True)
    run_max = neg_inf
    for kt in range(n_tiles):
        weights(0, kt, m0)
        run_max = scores(1, kt, run_max)
    out0 = attend(0)
    m1 = jnp.max(run_max, axis=-1, keepdims=True)
    for kt in range(n_tiles):
        weights(1, kt, m1)
    o = out0 - lam_ref[0] * attend(1)
    ms = jnp.mean(o * o, axis=-1, keepdims=True)
    o_ref[...] = ((o * lax.rsqrt(ms + eps) * g_ref[0]) * post_scale).astype(o_ref.dtype)


def _diff_attention(proj, bias_table, gain, lam, lam_init, batch, seq, n_heads_a, n_heads_b, *, tile=256):
    t, pw = proj.shape
    n_tiles = seq // tile
    q0 = 3 * n_heads_a
    k0 = q0 + n_heads_b
    v0 = k0 + n_heads_b
    assert tile + 1 >= REL_MAX_DISTANCE
    rel = (jnp.arange(5)[:, None, None] - 2) * tile + jnp.arange(tile)[None, None, :] - jnp.arange(tile)[None, :, None]
    bias = _rel_bias(bias_table, rel)[n_heads_a:]
    return pl.pallas_call(
        functools.partial(_diff_attn_kernel, tile=tile, n_tiles=n_tiles, eps=SUBLN_EPS,
                          post_scale=1.0 - lam_init),
        grid=(batch, n_heads_b, n_tiles),
        in_specs=[pl.BlockSpec(memory_space=pltpu.SMEM),
                  pl.BlockSpec((tile, HEAD_DIM), lambda b, h, i: (b * n_tiles + i, q0 + h)),
                  pl.BlockSpec((seq, HEAD_DIM), lambda b, h, i: (b, k0 + h)),
                  pl.BlockSpec((seq, HEAD_DIM), lambda b, h, i: (b, v0 + h)),
                  pl.BlockSpec((1, 5, tile, tile), lambda b, h, i: (h, 0, 0, 0)),
                  pl.BlockSpec((1, 1, HEAD_DIM), lambda b, h, i: (h, 0, 0))],
        out_specs=pl.BlockSpec((tile, HEAD_DIM), lambda b, h, i: (b * n_tiles + i, h)),
        out_shape=jax.ShapeDtypeStruct((t, n_heads_b * HEAD_DIM), BF16),
        scratch_shapes=[pltpu.VMEM((2, tile, seq), F32), pltpu.VMEM((2, tile, seq), BF16),
                        pltpu.VMEM((seq, 2 * HEAD_DIM), BF16)],
        compiler_params=_params("arbitrary", "arbitrary", "arbitrary"),
    )(lam.reshape(1), proj, proj, proj, bias, gain.reshape(n_heads_b, 1, HEAD_DIM))


def _out_router_kernel(a_ref, b_ref, x_ref, wo_ref, g_ref, wrh_ref, wrl_ref,
                       x1_ref, hn_ref, route_ref, cnt_ref, *, tm, wa, eps):
    @pl.when(pl.program_id(0) == 0)
    def _():
        cnt_ref[...] = jnp.zeros(cnt_ref.shape, F32)

    y = jnp.dot(a_ref[...], wo_ref[:wa, :], preferred_element_type=F32)
    y = y + jnp.dot(b_ref[...], wo_ref[wa:, :], preferred_element_type=F32)
    x1 = x_ref[...] + y
    x1_ref[...] = x1
    ms = jnp.mean(x1 * x1, axis=-1, keepdims=True)
    hn = x1 * lax.rsqrt(ms + eps) * g_ref[...]
    hn_ref[...] = hn

    hi = hn.astype(BF16)
    lo = (hn - hi.astype(F32)).astype(BF16)
    logits = (jnp.dot(hi, wrh_ref[...], preferred_element_type=F32)
              + jnp.dot(lo, wrh_ref[...], preferred_element_type=F32)
              + jnp.dot(hi, wrl_ref[...], preferred_element_type=F32))

    lane = lax.broadcasted_iota(jnp.int32, (tm, LANES), 1)
    neg_inf = jnp.float32(-jnp.inf)

    def first_max(vals):
        top = jnp.max(vals, axis=-1, keepdims=True)
        idx = jnp.min(jnp.where(vals == top, lane, LANES), axis=-1, keepdims=True)
        return top, idx

    is_grp = lane < N_GROUPS
    g_top, g_idx = first_max(jnp.where(is_grp, logits, neg_inf))
    g_sum = jnp.sum(jnp.where(is_grp, jnp.exp(logits - g_top), 0.0), axis=-1, keepdims=True)
    p_grp = 1.0 / g_sum
    e_lo = N_GROUPS + EXPERTS_PER_GROUP * g_idx
    e_vals = jnp.where((lane >= e_lo) & (lane < e_lo + EXPERTS_PER_GROUP), logits, neg_inf)
    top1, i1 = first_max(e_vals)
    top2, i2 = first_max(jnp.where(lane == i1, neg_inf, e_vals))
    b2 = jnp.exp(top2 - top1)
    w0 = p_grp / (1.0 + b2)
    w1 = p_grp * b2 / (1.0 + b2)

    oh0 = (lane == i1).astype(F32)
    oh1 = (lane == i2).astype(F32)
    both = oh0 + oh1
    r_i = lax.broadcasted_iota(jnp.int32, (tm, tm), 0)
    c_i = lax.broadcasted_iota(jnp.int32, (tm, tm), 1)
    before = jnp.where(r_i > c_i, 1.0, 0.0).astype(BF16)
    base = cnt_ref[...] + jnp.dot(before, both.astype(BF16), preferred_element_type=F32)
    rank0 = jnp.sum(base * oh0, axis=-1, keepdims=True)
    rank1 = jnp.sum(base * oh1, axis=-1, keepdims=True)
    cnt_ref[...] = cnt_ref[...] + jnp.sum(both, axis=0, keepdims=True)

    fields = {_R_E0: (i1 - N_GROUPS).astype(F32), _R_E1: (i2 - N_GROUPS).astype(F32),
              _R_W0: w0, _R_W1: w1, _R_RANK0: rank0, _R_RANK1: rank1}
    route = jnp.zeros((tm, LANES), F32)
    for n, f in fields.items():
        route = jnp.where(lane == n, f, route)
    route_ref[...] = route


def _out_router(out_a, out_b, x, w_out, layer, g, wr_hi, wr_lo, *, tm=256):
    t, d = x.shape
    wa = out_a.shape[1]
    wb = out_b.shape[1]
    row = lambda i: (i, 0)
    const = lambda i: (0, 0)
    return pl.pallas_call(
        functools.partial(_out_router_kernel, tm=tm, wa=wa, eps=RMS_EPS),
        grid=(t // tm,),
        in_specs=[pl.BlockSpec((tm, wa), row), pl.BlockSpec((tm, wb), row), pl.BlockSpec((tm, d), row),
                  pl.BlockSpec((None, wa + wb, d), lambda i: (layer, 0, 0)), pl.BlockSpec((1, d), const),
                  pl.BlockSpec((d, LANES), const), pl.BlockSpec((d, LANES), const)],
        out_specs=[pl.BlockSpec((tm, d), row), pl.BlockSpec((tm, d), row),
                   pl.BlockSpec((tm, LANES), row), pl.BlockSpec((1, LANES), const)],
        out_shape=[jax.ShapeDtypeStruct((t, d), F32), jax.ShapeDtypeStruct((t, d), F32),
                   jax.ShapeDtypeStruct((t, LANES), F32), jax.ShapeDtypeStruct((1, LANES), F32)],
        compiler_params=_params("arbitrary"),
    )(out_a, out_b, x, w_out, g.reshape(1, d), wr_hi, wr_lo)


def _moe_kernel(nitems_ref, iexp_ref, icnt_ref, tok_ref, nxt_ref, dst_ref, hn_hbm, wg_ref, wu_ref, wd_ref, out_hbm,
                xbuf, ybuf, gsem, ssem, *, rows, dump_rows):
    del iexp_ref
    it = pl.program_id(0)
    first_half = pl.program_id(1) == 0
    n_items = nitems_ref[0]
    cnt = icnt_ref[it]
    buf = lax.rem(it, 2)
    tiles = rows // SUBLANES
    d = ybuf.shape[-1]

    def issue_rows(first_tile, n_tiles, copy_row):
        def body(g, carry):
            for u in range(SUBLANES):
                copy_row(first_tile + g, u).start()
            return carry
        lax.fori_loop(0, n_tiles, body, 0)

    def hbm_row(ref, row):
        return ref.at[lax.shift_right_logical(row, 3), pl.ds(row & (SUBLANES - 1), 1)]

    def start_gather(idx_ref, s, n_blocks):
        issue_rows(0, n_blocks * tiles, lambda g, u: pltpu.make_async_copy(
            hbm_row(hn_hbm, idx_ref[0, 0, g * SUBLANES + u]), xbuf.at[s, g, pl.ds(u, 1)], gsem.at[s]))

    def start_scatter(j):
        issue_rows(j * tiles, tiles, lambda g, u: pltpu.make_async_copy(
            ybuf.at[g, pl.ds(u, 1)], hbm_row(out_hbm, dst_ref[0, 0, g * SUBLANES + u]), ssem.at[0]))

    def wait_blocks(n_blocks, make_block_copy):
        def body(j, carry):
            make_block_copy().wait()
            return carry
        lax.fori_loop(0, n_blocks, body, 0)

    def wait_gather(s, n_blocks):
        wait_blocks(n_blocks, lambda: pltpu.make_async_copy(
            hn_hbm.at[pl.ds(0, tiles)], xbuf.at[s, pl.ds(0, tiles)], gsem.at[s]))

    def wait_scatter(n_blocks):
        wait_blocks(n_blocks, lambda: pltpu.make_async_copy(
            ybuf.at[pl.ds(0, tiles)], out_hbm.at[pl.ds(0, tiles)], ssem.at[0]))

    def block_tiles(j):
        return pl.ds(pl.multiple_of(j * tiles, tiles), tiles)

    def ffn_half(j):
        x = xbuf[buf, block_tiles(j)].reshape(rows, d)
        gate = jnp.dot(x, wg_ref[0], preferred_element_type=F32)
        up = jnp.dot(x, wu_ref[0], preferred_element_type=F32)
        hid = gate * (1.0 / (1.0 + jnp.exp(-gate))) * up
        return jnp.dot(hid, wd_ref[0], preferred_element_type=F32).reshape(tiles, SUBLANES, d)

    @pl.when(jnp.logical_and(it < n_items, first_half))
    def _():
        @pl.when(it == 0)
        def _():
            ybuf[pl.ds(0, tiles)] = jnp.zeros((tiles, SUBLANES, d), F32)
            for start in dump_rows:
                dump = pltpu.make_async_copy(ybuf.at[pl.ds(0, tiles)],
                                             out_hbm.at[pl.ds(start // SUBLANES, tiles)], ssem.at[0])
                dump.start()
                dump.wait()
            start_gather(tok_ref, buf, cnt)

        wait_gather(buf, cnt)

        def body(j, carry):
            y = ffn_half(j)

            @pl.when(jnp.logical_and(j == 0, it >= 1))
            def _():
                wait_scatter(icnt_ref[jnp.maximum(it - 1, 0)])

            ybuf[block_tiles(j)] = y
            return carry
        lax.fori_loop(0, cnt, body, 0)

    @pl.when(jnp.logical_and(it < n_items, jnp.logical_not(first_half)))
    def _():
        @pl.when(it + 1 < n_items)
        def _():
            start_gather(nxt_ref, 1 - buf, icnt_ref[jnp.minimum(it + 1, icnt_ref.shape[0] - 1)])

        def body(j, carry):
            ybuf[block_tiles(j)] = ybuf[block_tiles(j)] + ffn_half(j)
            start_scatter(j)
            return carry
        lax.fori_loop(0, cnt, body, 0)

        @pl.when(it + 1 >= n_items)
        def _():
            wait_scatter(cnt)


def _plane_rows(n_tok):
    return n_tok + 2 * MOE_BLOCK


def _moe(hn, item_tok, item_dst, item_exp, item_cnt, n_items, w_gate, w_up, w_down, layer):
    t, d = hn.shape
    max_items = item_tok.shape[0]
    rows = MOE_BLOCK
    item_rows = MOE_GROUP * rows
    ffn = w_gate.shape[3]
    plane = _plane_rows(t)
    dump_rows = tuple(p * plane + t + s * rows for p in range(2) for s in range(2))
    half_ffn = ffn // 2

    def half(it, h, ni):
        return jnp.where(it < ni[0], h, 1)

    grid_spec = pltpu.PrefetchScalarGridSpec(
        num_scalar_prefetch=3,
        grid=(max_items, 2),
        in_specs=[
            pl.BlockSpec((1, 1, item_rows), lambda it, h, ni, ie, ic: (it, 0, 0), memory_space=pltpu.SMEM),
            pl.BlockSpec((1, 1, item_rows), lambda it, h, ni, ie, ic: (jnp.minimum(it + 1, max_items - 1), 0, 0),
                         memory_space=pltpu.SMEM),
            pl.BlockSpec((1, 1, item_rows), lambda it, h, ni, ie, ic: (it, 0, 0), memory_space=pltpu.SMEM),
            pl.BlockSpec(memory_space=pl.ANY),
            pl.BlockSpec((None, 1, d, half_ffn), lambda it, h, ni, ie, ic: (layer, ie[it], 0, half(it, h, ni))),
            pl.BlockSpec((None, 1, d, half_ffn), lambda it, h, ni, ie, ic: (layer, ie[it], 0, half(it, h, ni))),
            pl.BlockSpec((None, 1, half_ffn, d), lambda it, h, ni, ie, ic: (layer, ie[it], half(it, h, ni), 0)),
        ],
        out_specs=pl.BlockSpec(memory_space=pl.ANY),
        scratch_shapes=[pltpu.VMEM((2, item_rows // SUBLANES, SUBLANES, d), F32),
                        pltpu.VMEM((item_rows // SUBLANES, SUBLANES, d), F32),
                        pltpu.SemaphoreType.DMA((2,)), pltpu.SemaphoreType.DMA((1,))],
    )
    out = pl.pallas_call(
        functools.partial(_moe_kernel, rows=rows, dump_rows=dump_rows),
        grid_spec=grid_spec,
        out_shape=jax.ShapeDtypeStruct((2 * plane // SUBLANES, SUBLANES, d), F32),
        compiler_params=_params("arbitrary", "arbitrary", disable_bounds_checks=True),
    )(n_items, item_exp, item_cnt, item_tok, item_tok, item_dst, hn.reshape(t // SUBLANES, SUBLANES, d),
      w_gate, w_up, w_down)
    return out.reshape(2 * plane, d)


def _dispatch(route, counts, n_tok):
    rows = MOE_BLOCK
    group = MOE_GROUP
    plane = _plane_rows(n_tok)
    n_blocks = -(-(2 * n_tok) // rows) + N_EXPERTS
    max_items = N_EXPERTS + n_blocks // group
    e = route[:, _R_E0:_R_E1 + 1].astype(jnp.int32)
    rank = route[:, _R_RANK0:_R_RANK1 + 1].astype(jnp.int32)
    cnt = counts[0, N_GROUPS:N_GROUPS + N_EXPERTS].astype(jnp.int32)
    padded = (cnt + rows - 1) // rows * rows
    pend = jnp.cumsum(padded)
    pstart = pend - padded
    dest = (pstart[e] + rank).reshape(-1)
    code = (2 * jnp.arange(n_tok, dtype=jnp.int32)[:, None] + jnp.arange(2, dtype=jnp.int32)[None, :]).reshape(-1)
    row_src = jnp.full((n_blocks * rows,), -1, jnp.int32).at[dest].set(code)
    row = jnp.arange(n_blocks * rows, dtype=jnp.int32)
    row_tok = jnp.maximum(row_src, 0) // 2
    row_dst = jnp.where(row_src >= 0, (row_src % 2) * plane + row_src // 2, n_tok + row % rows)
    n_blk = padded // rows
    first_blk = pstart // rows
    items_end = jnp.cumsum((n_blk + group - 1) // group)
    items_start = items_end - (n_blk + group - 1) // group
    n_items = items_end[-1].astype(jnp.int32)
    it = jnp.arange(max_items, dtype=jnp.int32)
    item_exp = jnp.minimum(jnp.sum(items_end[None, :] <= it[:, None], axis=1), N_EXPERTS - 1).astype(jnp.int32)
    run = it - items_start[item_exp]
    item_blk = first_blk[item_exp] + run * group
    item_cnt = jnp.clip(n_blk[item_exp] - run * group, 0, group)
    live = it < n_items
    item_cnt = jnp.where(live, item_cnt, 0).astype(jnp.int32)
    item_exp = jnp.where(live, item_exp, item_exp[jnp.maximum(n_items - 1, 0)])
    blk_ids = jnp.clip(item_blk[:, None] + jnp.arange(group, dtype=jnp.int32)[None, :], 0, n_blocks - 1)
    item_tok = row_tok.reshape(n_blocks, rows)[blk_ids].reshape(max_items, 1, group * rows)
    item_dst = row_dst.reshape(n_blocks, rows)[blk_ids].reshape(max_items, 1, group * rows)
    return item_tok, item_dst, item_exp, item_cnt, n_items.reshape(1)


def _final_kernel(x_ref, route_ref, ya_ref, yb_ref, g_ref, o_ref, *, eps):
    x = _add_expert_outputs(x_ref[...], route_ref, ya_ref, yb_ref)
    ms = jnp.mean(x * x, axis=-1, keepdims=True)
    o_ref[...] = x * lax.rsqrt(ms + eps) * g_ref[...]


def _final_norm(x, moe, g, *, tm=512):
    t, d = x.shape
    route, planes = moe
    row = lambda i: (i, 0)
    return pl.pallas_call(
        functools.partial(_final_kernel, eps=RMS_EPS),
        grid=(t // tm,),
        in_specs=[pl.BlockSpec((tm, d), row), pl.BlockSpec((tm, LANES), row)]
                 + _expert_output_specs(tm, d, t)
                 + [pl.BlockSpec((1, d), lambda i: (0, 0))],
        out_specs=pl.BlockSpec((tm, d), row),
        out_shape=jax.ShapeDtypeStruct((t, d), F32),
        compiler_params=_params("arbitrary"),
    )(x, route, planes, planes, g.reshape(1, d))


def kernel(x, rel_bias, norm_attn, w_in, gain_a, gain_b, lam_q1, lam_k1, lam_q2, lam_k2, w_out, norm_ffn,
           w_router_group, w_router_expert, w_gate, w_up, w_down, norm_final):
    batch, seq, d = x.shape
    depth = w_in.shape[0]
    n_heads_a = gain_a.shape[1]
    n_heads_b = gain_b.shape[1]
    t = batch * seq
    xs = x.reshape(t, d)
    w_in, w_out = w_in.astype(BF16), w_out.astype(BF16)
    dils = tuple(dil for _, dil in DILATED_PATTERNS)
    extra_dils = tuple(dil for dil in dils if dil > 1)
    moe = None
    for l in range(depth):
        proj, views, xs = _norm_proj(xs, moe, norm_attn[l], w_in, l, extra_dils, 3)
        views = dict(zip(extra_dils, views))
        parts = [_dilated_attention(views.get(dil, proj), rel_bias, batch, seq, n_heads_a, w, dil)
                 for (w, dil) in DILATED_PATTERNS]
        out_a = _dilated_mixture([p[0] for p in parts], [p[1] for p in parts], dils, gain_a[l])

        lam_init = 0.8 - 0.6 * math.exp(-0.3 * l)
        lam = (jnp.exp(jnp.sum(lam_q1[l].astype(F32) * lam_k1[l].astype(F32)))
               - jnp.exp(jnp.sum(lam_q2[l].astype(F32) * lam_k2[l].astype(F32))) + lam_init)
        out_b = _diff_attention(proj, rel_bias, gain_b[l], lam, lam_init, batch, seq, n_heads_a, n_heads_b)

        w_r = jnp.concatenate([w_router_group[l], w_router_expert[l].transpose(1, 0, 2).reshape(d, N_EXPERTS)], axis=1)
        w_r = jnp.pad(w_r.astype(F32), ((0, 0), (0, LANES - w_r.shape[1])))
        wr_hi = w_r.astype(BF16)
        wr_lo = (w_r - wr_hi.astype(F32)).astype(BF16)
        xs, hn, route, counts = _out_router(out_a, out_b, xs, w_out, l, norm_ffn[l], wr_hi, wr_lo)

        planes = _moe(hn, *_dispatch(route, counts, t), w_gate, w_up, w_down, l)
        moe = (route, planes)
    out = _final_norm(xs, moe, norm_final)
    return out.reshape(batch, seq, d)
```

```python
import functools
import math

import jax
import jax.numpy as jnp
from jax import lax
from jax.experimental import pallas as pl
from jax.experimental.pallas import tpu as pltpu

F32 = jnp.float32
BF16 = jnp.bfloat16

LANES = 128
HEAD_DIM = 128
DIFF_DIM = HEAD_DIM // 2
DILATED_PATTERNS = ((128, 1), (512, 4), (2048, 16))
DILATED_HALO = 64
N_REL_BUCKETS = 32
REL_MAX_DISTANCE = 128
N_GROUPS = 4
EXPERTS_PER_GROUP = 8
N_EXPERTS = N_GROUPS * EXPERTS_PER_GROUP
MOE_BLOCK = 256
MOE_GROUP = 3
SUBLANES = 8
ROW_COPY_PRIORITY = 1
RMS_EPS = 1e-6
SUBLN_EPS = 1e-5
NEG_BIG = -1e30
VMEM_LIMIT = 56 * 1024 * 1024

_NT = (((1,), (1,)), ((), ()))

_R_E0, _R_E1, _R_W0, _R_W1, _R_RANK0, _R_RANK1 = range(6)


def _t5_bucket(rel):
    nb = N_REL_BUCKETS // 2
    max_exact = nb // 2
    n = -rel
    ret = jnp.where(n < 0, nb, 0)
    n = jnp.abs(n)
    nf = jnp.maximum(n, 1).astype(jnp.float32)
    large = max_exact + (jnp.log(nf / max_exact) / math.log(REL_MAX_DISTANCE / max_exact)
                         * (nb - max_exact)).astype(jnp.int32)
    large = jnp.minimum(large, nb - 1)
    return (ret + jnp.where(n < max_exact, n, large)).astype(jnp.int32)


def _rel_bias(bias_table, rel):
    onehot = (_t5_bucket(rel).reshape(-1)[None, :] == jnp.arange(N_REL_BUCKETS)[:, None]).astype(F32)
    vals = jnp.dot(bias_table.astype(F32).T, onehot, precision=lax.Precision.HIGHEST)
    return vals.reshape((bias_table.shape[1],) + rel.shape)


def _params(*sem, **kwargs):
    return pltpu.CompilerParams(dimension_semantics=sem, vmem_limit_bytes=VMEM_LIMIT, **kwargs)


def _add_expert_outputs(x, route_ref, ya_ref, yb_ref):
    return x + route_ref[:, _R_W0:_R_W0 + 1] * ya_ref[...] + route_ref[:, _R_W1:_R_W1 + 1] * yb_ref[...]


def _expert_output_specs(tm, d, n_tok):
    blocks_per_plane, rem = divmod(n_tok + 2 * MOE_BLOCK, tm)
    assert rem == 0
    return [pl.BlockSpec((tm, d), lambda i, *_, p=p: (p * blocks_per_plane + i, 0)) for p in range(2)]


def _norm_proj_kernel(*refs, has_y, dils, n_split, eps):
    n_in = 6 if has_y else 3
    if has_y:
        x_ref, route_ref, ya_ref, yb_ref, g_ref, w_ref = refs[:n_in]
    else:
        x_ref, g_ref, w_ref = refs[:n_in]
    o_ref = refs[n_in]
    od_refs = refs[n_in + 1:n_in + 1 + len(dils)]
    rest = refs[n_in + 1 + len(dils):]
    xs_ref = rest[0] if has_y else None
    xn_ref, res_ref = rest[-2:]
    j = pl.program_id(1)

    @pl.when(j == 0)
    def _():
        x = x_ref[...]
        if has_y:
            x = _add_expert_outputs(x, route_ref, ya_ref, yb_ref)
            xs_ref[...] = x
        ms = jnp.mean(x * x, axis=-1, keepdims=True)
        xn_ref[...] = (x * lax.rsqrt(ms + eps) * g_ref[...]).astype(BF16)

    res = jnp.dot(xn_ref[...], w_ref[...], preferred_element_type=F32)
    o_ref[...] = res.astype(o_ref.dtype)

    @pl.when(j < n_split)
    def _():
        n_chunks, tm, _ = res_ref.shape
        tn = n_chunks * LANES
        for c in range(n_chunks):
            res_ref[c] = res[:, c * LANES:(c + 1) * LANES]
        for od_ref, dil in zip(od_refs, dils):
            for r in range(dil):
                for c in range(n_chunks):
                    col = r * tn + c * LANES
                    od_ref[:, col:col + LANES] = res_ref[c, pl.ds(r, tm // dil, stride=dil), :].astype(od_ref.dtype)


def _norm_proj(x, moe, g, w, layer, dils, n_split, *, tm=512, tn=1024):
    t, d = x.shape
    n = w.shape[2]
    has_y = moe is not None
    row = lambda i, j: (i, 0)
    in_specs = [pl.BlockSpec((tm, d), row)]
    args = [x]
    if has_y:
        route, planes = moe
        in_specs += [pl.BlockSpec((tm, LANES), row)]
        in_specs += _expert_output_specs(tm, d, t)
        args += [route, planes, planes]
    in_specs += [pl.BlockSpec((1, d), lambda i, j: (0, 0)), pl.BlockSpec((None, d, tn), lambda i, j: (layer, 0, j))]
    args += [g.reshape(1, d), w]
    out_shape = [jax.ShapeDtypeStruct((t, n), BF16)]
    out_specs = [pl.BlockSpec((tm, tn), lambda i, j: (i, j))]
    for dil in dils:
        out_shape.append(jax.ShapeDtypeStruct((t // dil, n_split * dil * tn), BF16))
        out_specs.append(pl.BlockSpec((tm // dil, dil * tn), lambda i, j: (i, jnp.minimum(j, n_split - 1))))
    if has_y:
        out_shape.append(jax.ShapeDtypeStruct((t, d), F32))
        out_specs.append(pl.BlockSpec((tm, d), row))
    res = pl.pallas_call(
        functools.partial(_norm_proj_kernel, has_y=has_y, dils=dils, n_split=n_split, eps=RMS_EPS),
        grid=(t // tm, n // tn),
        in_specs=in_specs, out_specs=out_specs, out_shape=out_shape,
        scratch_shapes=[pltpu.VMEM((tm, d), BF16), pltpu.VMEM((tn // LANES, tm, LANES), F32)],
        compiler_params=_params("arbitrary", "arbitrary"),
    )(*args)
    views = list(res[1:1 + len(dils)])
    return res[0], views, (res[-1] if has_y else x)


def _dilated_kernel(q_ref, kp_ref, kc_ref, kn_ref, vp_ref, vc_ref, vn_ref, bias_ref, o_ref, lse_ref,
                    *, tq, n_sub, n_heads, seq_sub, scale):
    i = pl.program_id(2)
    halo = DILATED_HALO
    win = tq + 2 * halo
    lane = lax.broadcasted_iota(jnp.int32, (tq, LANES), 1)
    col = lax.broadcasted_iota(jnp.int32, (tq, win), 1)
    in_range, lse_all = [], []
    for t in range(n_sub):
        key_m = (i * n_sub + t) * tq - halo + col
        in_range.append((key_m >= 0) & (key_m < seq_sub))
        lse_all.append(jnp.zeros((tq, LANES), F32))
    for h in range(n_heads):
        cs = slice(h * HEAD_DIM, (h + 1) * HEAD_DIM)
        k = jnp.concatenate([kp_ref[:, cs], kc_ref[:, cs], kn_ref[:, cs]], axis=0)
        v = jnp.concatenate([vp_ref[:, cs], vc_ref[:, cs], vn_ref[:, cs]], axis=0)
        for t in range(n_sub):
            rows = slice(t * tq, (t + 1) * tq)
            keys = slice(t * tq, t * tq + win)
            s = lax.dot_general(q_ref[rows, cs], k[keys], _NT, preferred_element_type=F32) * scale + bias_ref[h]
            s = jnp.where(in_range[t], s, NEG_BIG)
            m = jnp.max(s, axis=-1, keepdims=True)
            p = jnp.exp(s - m)
            den = jnp.sum(p, axis=-1, keepdims=True)
            o_ref[rows, cs] = jnp.dot(p.astype(BF16), v[keys], preferred_element_type=F32) / den
            lse_all[t] = jnp.where(lane == h, m + jnp.log(den), lse_all[t])
    for t in range(n_sub):
        lse_ref[t * tq:(t + 1) * tq, :] = lse_all[t]


def _dilated_bias(bias_table, n_heads, window, dilation, tq):
    radius = window // (2 * dilation)
    assert radius <= DILATED_HALO
    win = tq + 2 * DILATED_HALO
    dm = jnp.arange(win)[None, :] - DILATED_HALO - jnp.arange(tq)[:, None]
    b = _rel_bias(bias_table, dm * dilation)[:n_heads]
    return jnp.where((jnp.abs(dm) <= radius)[None], b, NEG_BIG)


def _dilated_attention(view, bias_table, batch, seq, n_heads, window, dilation, *, tq=128, n_sub=2):
    t = view.shape[0] * dilation
    wa = n_heads * HEAD_DIM
    sub = seq // dilation
    n_sub = min(n_sub, sub // tq)
    step = n_sub * tq
    nq = sub // step
    halo = DILATED_HALO
    per_step = step // halo
    n_halo = sub // halo
    bias = _dilated_bias(bias_table, n_heads, window, dilation, tq)

    def main(which):
        return pl.BlockSpec((step, wa), lambda b, r, i: (b * nq + i, which * dilation + r))

    def edge(which, side):
        def imap(b, r, i):
            blk = i * per_step - 1 if side < 0 else (i + 1) * per_step
            return (b * n_halo + jnp.clip(blk, 0, n_halo - 1), which * dilation + r)
        return pl.BlockSpec((halo, wa), imap)

    return pl.pallas_call(
        functools.partial(_dilated_kernel, tq=tq, n_sub=n_sub, n_heads=n_heads, seq_sub=sub,
                          scale=HEAD_DIM ** -0.5),
        grid=(batch, dilation, nq),
        in_specs=[main(0), edge(1, -1), main(1), edge(1, 1), edge(2, -1), main(2), edge(2, 1),
                  pl.BlockSpec(bias.shape, lambda b, r, i: (0, 0, 0))],
        out_specs=[pl.BlockSpec((step, wa), lambda b, r, i: (b * nq + i, r)),
                   pl.BlockSpec((step, LANES), lambda b, r, i: (b * nq + i, r))],
        out_shape=[jax.ShapeDtypeStruct((t // dilation, dilation * wa), F32),
                   jax.ShapeDtypeStruct((t // dilation, dilation * LANES), F32)],
        compiler_params=_params("arbitrary", "arbitrary", "arbitrary"),
    )(view, view, view, view, view, view, view, bias)


def _mix_kernel(*refs, dils, n_heads, eps):
    n = len(dils)
    o_refs, l_refs = list(refs[:n]), list(refs[n:2 * n])
    g_ref, out_ref = refs[2 * n], refs[2 * n + 1]
    scratch = list(refs[2 * n + 2:])
    tm, wa = out_ref.shape
    def head_reader(ref):
        return lambda h: ref[:, h * HEAD_DIM:(h + 1) * HEAD_DIM]

    heads = [head_reader(o_ref) for o_ref in o_refs]
    for p, dil in enumerate(dils):
        if dil == 1:
            continue
        o_tok, l_tok = scratch.pop(0), scratch.pop(0)
        for r in range(dil):
            rows = pl.ds(r, tm // dil, stride=dil)
            for h in range(n_heads):
                o_tok[h, rows, :] = heads[p](r * n_heads + h)
            l_tok[rows, :] = l_refs[p][:, r * LANES:(r + 1) * LANES]
        heads[p], l_refs[p] = (lambda h, ref=o_tok: ref[h]), l_tok
    ls = [l_ref[...] for l_ref in l_refs]
    top = functools.reduce(jnp.maximum, ls)
    es = [jnp.exp(l - top) for l in ls]
    tot = functools.reduce(lambda a, b: a + b, es)
    ws = [e / tot for e in es]
    for h in range(n_heads):
        cs = slice(h * HEAD_DIM, (h + 1) * HEAD_DIM)
        mix = ws[0][:, h:h + 1] * heads[0](h)
        for p in range(1, n):
            mix = mix + ws[p][:, h:h + 1] * heads[p](h)
        ms = jnp.mean(mix * mix, axis=-1, keepdims=True)
        out_ref[:, cs] = (mix * lax.rsqrt(ms + eps) * g_ref[:, cs]).astype(out_ref.dtype)


def _dilated_mixture(outs, lses, dils, gain, *, tm=512):
    wa = gain.size
    t = outs[0].shape[0] * dils[0]
    n_heads = wa // HEAD_DIM
    row = lambda i: (i, 0)
    scratch = []
    for dil in dils:
        if dil > 1:
            scratch += [pltpu.VMEM((n_heads, tm, HEAD_DIM), F32), pltpu.VMEM((tm, LANES), F32)]
    return pl.pallas_call(
        functools.partial(_mix_kernel, dils=dils, n_heads=n_heads, eps=RMS_EPS),
        grid=(t // tm,),
        in_specs=[pl.BlockSpec((tm // dil, dil * wa), row) for dil in dils]
                 + [pl.BlockSpec((tm // dil, dil * LANES), row) for dil in dils]
                 + [pl.BlockSpec((1, wa), lambda i: (0, 0))],
        out_specs=pl.BlockSpec((tm, wa), row),
        out_shape=jax.ShapeDtypeStruct((t, wa), BF16),
        scratch_shapes=scratch,
        compiler_params=_params("arbitrary"),
    )(*outs, *lses, gain.reshape(1, wa))


def _diff_attn_kernel(lam_ref, q_ref, k_ref, v_ref, bias_ref, g_ref, o_ref, s_scr, p_scr, v1_scr,
                      *, tile, q_tiles, n_tiles, eps, post_scale):
    step = pl.program_id(2)
    lane = lax.broadcasted_iota(jnp.int32, (q_tiles * tile, HEAD_DIM), 1)
    qs = q_ref[...] * (DIFF_DIM ** -0.5)
    zero = jnp.zeros_like(qs)
    q_maps = (jnp.where(lane < DIFF_DIM, qs, zero), jnp.where(lane >= DIFF_DIM, qs, zero))
    chunks = tile // LANES
    streams = [(qt, mp) for qt in range(q_tiles) for mp in range(2)]

    def fold(op, acc, vals):
        for c in range(chunks):
            acc = op(acc, vals[:, c * LANES:(c + 1) * LANES])
        return acc

    def scores(n, kt, run_max):
        qt, mp = streams[n]
        ks = slice(kt * tile, (kt + 1) * tile)
        rel_class = jnp.clip(kt - (step * q_tiles + qt), -2, 2) + 2
        q = q_maps[mp][qt * tile:(qt + 1) * tile]
        s = lax.dot_general(q, k_ref[ks, :], _NT, preferred_element_type=F32) + bias_ref[0, rel_class]
        s_scr[n % 2, :, ks] = s
        return fold(jnp.maximum, run_max, s)

    def weights(n, kt, m):
        ks = slice(kt * tile, (kt + 1) * tile)
        p_scr[n % 2, :, ks] = jnp.exp(s_scr[n % 2, :, ks] - m).astype(BF16)

    def attend(n):
        acc = jnp.dot(p_scr[n % 2], v1_scr[...], preferred_element_type=F32)
        return acc[:, :HEAD_DIM] / acc[:, HEAD_DIM:]

    @pl.when(step == 0)
    def _():
        v1_scr[:, :HEAD_DIM] = v_ref[...]
        v1_scr[:, HEAD_DIM:] = jnp.ones((v1_scr.shape[0], HEAD_DIM), BF16)

    neg_inf = jnp.full((tile, LANES), -jnp.inf, F32)
    run_max = neg_inf
    for kt in range(n_tiles):
        run_max = scores(0, kt, run_max)
    outs = []
    for n in range(len(streams)):
        m = jnp.max(run_max, axis=-1, keepdims=True)
        run_max = neg_inf
        for kt in range(n_tiles):
            weights(n, kt, m)
            if n + 1 < len(streams):
                run_max = scores(n + 1, kt, run_max)
        outs.append(attend(n))
    for qt in range(q_tiles):
        o = outs[2 * qt] - lam_ref[0] * outs[2 * qt + 1]
        ms = jnp.mean(o * o, axis=-1, keepdims=True)
        o_ref[qt * tile:(qt + 1) * tile, :] = ((o * lax.rsqrt(ms + eps) * g_ref[0]) * post_scale).astype(o_ref.dtype)


def _diff_attention(proj, bias_table, gain, lam, lam_init, batch, seq, n_heads_a, n_heads_b, *, tile=256, q_tiles=2):
    t, pw = proj.shape
    n_tiles = seq // tile
    q0 = 3 * n_heads_a
    k0 = q0 + n_heads_b
    v0 = k0 + n_heads_b
    assert tile + 1 >= REL_MAX_DISTANCE
    rel = (jnp.arange(5)[:, None, None] - 2) * tile + jnp.arange(tile)[None, None, :] - jnp.arange(tile)[None, :, None]
    bias = _rel_bias(bias_table, rel)[n_heads_a:]
    n_steps = n_tiles // q_tiles
    rows = q_tiles * tile
    return pl.pallas_call(
        functools.partial(_diff_attn_kernel, tile=tile, q_tiles=q_tiles, n_tiles=n_tiles, eps=SUBLN_EPS,
                          post_scale=1.0 - lam_init),
        grid=(batch, n_heads_b, n_steps),
        in_specs=[pl.BlockSpec(memory_space=pltpu.SMEM),
                  pl.BlockSpec((rows, HEAD_DIM), lambda b, h, i: (b * n_steps + i, q0 + h)),
                  pl.BlockSpec((seq, HEAD_DIM), lambda b, h, i: (b, k0 + h)),
                  pl.BlockSpec((seq, HEAD_DIM), lambda b, h, i: (b, v0 + h)),
                  pl.BlockSpec((1, 5, tile, tile), lambda b, h, i: (h, 0, 0, 0)),
                  pl.BlockSpec((1, 1, HEAD_DIM), lambda b, h, i: (h, 0, 0))],
        out_specs=pl.BlockSpec((rows, HEAD_DIM), lambda b, h, i: (b * n_steps + i, h)),
        out_shape=jax.ShapeDtypeStruct((t, n_heads_b * HEAD_DIM), BF16),
        scratch_shapes=[pltpu.VMEM((2, tile, seq), F32), pltpu.VMEM((2, tile, seq), BF16),
                        pltpu.VMEM((seq, 2 * HEAD_DIM), BF16)],
        compiler_params=_params("arbitrary", "arbitrary", "arbitrary"),
    )(lam.reshape(1), proj, proj, proj, bias, gain.reshape(n_heads_b, 1, HEAD_DIM))


def _out_router_kernel(a_ref, b_ref, x_ref, wo_ref, g_ref, wrh_ref, wrl_ref,
                       x1_ref, hn_ref, route_ref, cnt_ref, *, tm, wa, eps):
    @pl.when(pl.program_id(0) == 0)
    def _():
        cnt_ref[...] = jnp.zeros(cnt_ref.shape, F32)

    y = jnp.dot(a_ref[...], wo_ref[:wa, :], preferred_element_type=F32)
    y = y + jnp.dot(b_ref[...], wo_ref[wa:, :], preferred_element_type=F32)
    x1 = x_ref[...] + y
    x1_ref[...] = x1
    ms = jnp.mean(x1 * x1, axis=-1, keepdims=True)
    hn = x1 * lax.rsqrt(ms + eps) * g_ref[...]
    hn_ref[...] = hn

    hi = hn.astype(BF16)
    lo = (hn - hi.astype(F32)).astype(BF16)
    logits = (jnp.dot(hi, wrh_ref[...], preferred_element_type=F32)
              + jnp.dot(lo, wrh_ref[...], preferred_element_type=F32)
              + jnp.dot(hi, wrl_ref[...], preferred_element_type=F32))

    lane = lax.broadcasted_iota(jnp.int32, (tm, LANES), 1)
    neg_inf = jnp.float32(-jnp.inf)

    def first_max(vals):
        top = jnp.max(vals, axis=-1, keepdims=True)
        idx = jnp.min(jnp.where(vals == top, lane, LANES), axis=-1, keepdims=True)
        return top, idx

    is_grp = lane < N_GROUPS
    g_top, g_idx = first_max(jnp.where(is_grp, logits, neg_inf))
    g_sum = jnp.sum(jnp.where(is_grp, jnp.exp(logits - g_top), 0.0), axis=-1, keepdims=True)
    p_grp = 1.0 / g_sum
    e_lo = N_GROUPS + EXPERTS_PER_GROUP * g_idx
    e_vals = jnp.where((lane >= e_lo) & (lane < e_lo + EXPERTS_PER_GROUP), logits, neg_inf)
    top1, i1 = first_max(e_vals)
    top2, i2 = first_max(jnp.where(lane == i1, neg_inf, e_vals))
    b2 = jnp.exp(top2 - top1)
    w0 = p_grp / (1.0 + b2)
    w1 = p_grp * b2 / (1.0 + b2)

    oh0 = (lane == i1).astype(F32)
    oh1 = (lane == i2).astype(F32)
    both = oh0 + oh1
    r_i = lax.broadcasted_iota(jnp.int32, (tm, tm), 0)
    c_i = lax.broadcasted_iota(jnp.int32, (tm, tm), 1)
    before = jnp.where(r_i > c_i, 1.0, 0.0).astype(BF16)
    base = cnt_ref[...] + jnp.dot(before, both.astype(BF16), preferred_element_type=F32)
    rank0 = jnp.sum(base * oh0, axis=-1, keepdims=True)
    rank1 = jnp.sum(base * oh1, axis=-1, keepdims=True)
    cnt_ref[...] = cnt_ref[...] + jnp.sum(both, axis=0, keepdims=True)

    fields = {_R_E0: (i1 - N_GROUPS).astype(F32), _R_E1: (i2 - N_GROUPS).astype(F32),
              _R_W0: w0, _R_W1: w1, _R_RANK0: rank0, _R_RANK1: rank1}
    route = jnp.zeros((tm, LANES), F32)
    for n, f in fields.items():
        route = jnp.where(lane == n, f, route)
    route_ref[...] = route


def _out_router(out_a, out_b, x, w_out, layer, g, wr_hi, wr_lo, *, tm=256):
    t, d = x.shape
    wa = out_a.shape[1]
    wb = out_b.shape[1]
    row = lambda i: (i, 0)
    const = lambda i: (0, 0)
    return pl.pallas_call(
        functools.partial(_out_router_kernel, tm=tm, wa=wa, eps=RMS_EPS),
        grid=(t // tm,),
        in_specs=[pl.BlockSpec((tm, wa), row), pl.BlockSpec((tm, wb), row), pl.BlockSpec((tm, d), row),
                  pl.BlockSpec((None, wa + wb, d), lambda i: (layer, 0, 0)), pl.BlockSpec((1, d), const),
                  pl.BlockSpec((d, LANES), const), pl.BlockSpec((d, LANES), const)],
        out_specs=[pl.BlockSpec((tm, d), row), pl.BlockSpec((tm, d), row),
                   pl.BlockSpec((tm, LANES), row), pl.BlockSpec((1, LANES), const)],
        out_shape=[jax.ShapeDtypeStruct((t, d), F32), jax.ShapeDtypeStruct((t, d), F32),
                   jax.ShapeDtypeStruct((t, LANES), F32), jax.ShapeDtypeStruct((1, LANES), F32)],
        compiler_params=_params("arbitrary"),
    )(out_a, out_b, x, w_out, g.reshape(1, d), wr_hi, wr_lo)


def _moe_kernel(nitems_ref, iexp_ref, icnt_ref, tok_ref, nxt_ref, dst_ref, hn_hbm, wg_ref, wu_ref, wd_ref, out_hbm,
                xbuf, ybuf, gsem, ssem, *, rows, dump_rows):
    del iexp_ref
    it = pl.program_id(0)
    first_half = pl.program_id(1) == 0
    n_items = nitems_ref[0]
    cnt = icnt_ref[it]
    buf = lax.rem(it, 2)
    tiles = rows // SUBLANES
    d = ybuf.shape[-1]

    def issue_rows(first_tile, n_tiles, copy_row):
        def body(g, carry):
            for u in range(SUBLANES):
                copy_row(first_tile + g, u).start(priority=ROW_COPY_PRIORITY)
            return carry
        lax.fori_loop(0, n_tiles, body, 0)

    def hbm_row(ref, row):
        return ref.at[lax.shift_right_logical(row, 3), pl.ds(row & (SUBLANES - 1), 1)]

    def start_gather(idx_ref, s, n_blocks):
        issue_rows(0, n_blocks * tiles, lambda g, u: pltpu.make_async_copy(
            hbm_row(hn_hbm, idx_ref[0, 0, g * SUBLANES + u]), xbuf.at[s, g, pl.ds(u, 1)], gsem.at[s]))

    def start_scatter(j):
        issue_rows(j * tiles, tiles, lambda g, u: pltpu.make_async_copy(
            ybuf.at[g, pl.ds(u, 1)], hbm_row(out_hbm, dst_ref[0, 0, g * SUBLANES + u]), ssem.at[0]))

    def wait_blocks(n_blocks, make_block_copy):
        def body(j, carry):
            make_block_copy().wait()
            return carry
        lax.fori_loop(0, n_blocks, body, 0)

    def wait_gather(s, n_blocks):
        wait_blocks(n_blocks, lambda: pltpu.make_async_copy(
            hn_hbm.at[pl.ds(0, tiles)], xbuf.at[s, pl.ds(0, tiles)], gsem.at[s]))

    def wait_scatter(n_blocks):
        wait_blocks(n_blocks, lambda: pltpu.make_async_copy(
            ybuf.at[pl.ds(0, tiles)], out_hbm.at[pl.ds(0, tiles)], ssem.at[0]))

    def block_tiles(j):
        return pl.ds(pl.multiple_of(j * tiles, tiles), tiles)

    def ffn_half(j):
        x = xbuf[buf, block_tiles(j)].reshape(rows, d)
        gate = jnp.dot(x, wg_ref[0], preferred_element_type=F32)
        up = jnp.dot(x, wu_ref[0], preferred_element_type=F32)
        hid = gate * (1.0 / (1.0 + jnp.exp(-gate))) * up
        return jnp.dot(hid, wd_ref[0], preferred_element_type=F32).reshape(tiles, SUBLANES, d)

    @pl.when(jnp.logical_and(it < n_items, first_half))
    def _():
        @pl.when(it == 0)
        def _():
            ybuf[pl.ds(0, tiles)] = jnp.zeros((tiles, SUBLANES, d), F32)
            for start in dump_rows:
                dump = pltpu.make_async_copy(ybuf.at[pl.ds(0, tiles)],
                                             out_hbm.at[pl.ds(start // SUBLANES, tiles)], ssem.at[0])
                dump.start()
                dump.wait()
            start_gather(tok_ref, buf, cnt)

        wait_gather(buf, cnt)

        def body(j, carry):
            y = ffn_half(j)

            @pl.when(jnp.logical_and(j == 0, it >= 1))
            def _():
                wait_scatter(icnt_ref[jnp.maximum(it - 1, 0)])

            ybuf[block_tiles(j)] = y
            return carry
        lax.fori_loop(0, cnt, body, 0)

    @pl.when(jnp.logical_and(it < n_items, jnp.logical_not(first_half)))
    def _():
        @pl.when(it + 1 < n_items)
        def _():
            start_gather(nxt_ref, 1 - buf, icnt_ref[jnp.minimum(it + 1, icnt_ref.shape[0] - 1)])

        def body(j, carry):
            ybuf[block_tiles(j)] = ybuf[block_tiles(j)] + ffn_half(j)
            start_scatter(j)
            return carry
        lax.fori_loop(0, cnt, body, 0)

        @pl.when(it + 1 >= n_items)
        def _():
            wait_scatter(cnt)


def _plane_rows(n_tok):
    return n_tok + 2 * MOE_BLOCK


def _moe(hn, item_tok, item_dst, item_exp, item_cnt, n_items, w_gate, w_up, w_down, layer):
    t, d = hn.shape
    max_items = item_tok.shape[0]
    rows = MOE_BLOCK
    item_rows = MOE_GROUP * rows
    ffn = w_gate.shape[3]
    plane = _plane_rows(t)
    dump_rows = tuple(p * plane + t + s * rows for p in range(2) for s in range(2))
    half_ffn = ffn // 2

    def half(it, h, ni):
        return jnp.where(it < ni[0], h, 1)

    grid_spec = pltpu.PrefetchScalarGridSpec(
        num_scalar_prefetch=3,
        grid=(max_items, 2),
        in_specs=[
            pl.BlockSpec((1, 1, item_rows), lambda it, h, ni, ie, ic: (it, 0, 0), memory_space=pltpu.SMEM),
            pl.BlockSpec((1, 1, item_rows), lambda it, h, ni, ie, ic: (jnp.minimum(it + 1, max_items - 1), 0, 0),
                         memory_space=pltpu.SMEM),
            pl.BlockSpec((1, 1, item_rows), lambda it, h, ni, ie, ic: (it, 0, 0), memory_space=pltpu.SMEM),
            pl.BlockSpec(memory_space=pl.ANY),
            pl.BlockSpec((None, 1, d, half_ffn), lambda it, h, ni, ie, ic: (layer, ie[it], 0, half(it, h, ni))),
            pl.BlockSpec((None, 1, d, half_ffn), lambda it, h, ni, ie, ic: (layer, ie[it], 0, half(it, h, ni))),
            pl.BlockSpec((None, 1, half_ffn, d), lambda it, h, ni, ie, ic: (layer, ie[it], half(it, h, ni), 0)),
        ],
        out_specs=pl.BlockSpec(memory_space=pl.ANY),
        scratch_shapes=[pltpu.VMEM((2, item_rows // SUBLANES, SUBLANES, d), F32),
                        pltpu.VMEM((item_rows // SUBLANES, SUBLANES, d), F32),
                        pltpu.SemaphoreType.DMA((2,)), pltpu.SemaphoreType.DMA((1,))],
    )
    out = pl.pallas_call(
        functools.partial(_moe_kernel, rows=rows, dump_rows=dump_rows),
        grid_spec=grid_spec,
        out_shape=jax.ShapeDtypeStruct((2 * plane // SUBLANES, SUBLANES, d), F32),
        compiler_params=_params("arbitrary", "arbitrary", disable_bounds_checks=True),
    )(n_items, item_exp, item_cnt, item_tok, item_tok, item_dst, hn.reshape(t // SUBLANES, SUBLANES, d),
      w_gate, w_up, w_down)
    return out.reshape(2 * plane, d)


def _dispatch(route, counts, n_tok):
    rows = MOE_BLOCK
    group = MOE_GROUP
    plane = _plane_rows(n_tok)
    n_blocks = -(-(2 * n_tok) // rows) + N_EXPERTS
    max_items = N_EXPERTS + n_blocks // group
    e = route[:, _R_E0:_R_E1 + 1].astype(jnp.int32)
    rank = route[:, _R_RANK0:_R_RANK1 + 1].astype(jnp.int32)
    cnt = counts[0, N_GROUPS:N_GROUPS + N_EXPERTS].astype(jnp.int32)
    padded = (cnt + rows - 1) // rows * rows
    pend = jnp.cumsum(padded)
    pstart = pend - padded
    dest = (pstart[e] + rank).reshape(-1)
    code = (2 * jnp.arange(n_tok, dtype=jnp.int32)[:, None] + jnp.arange(2, dtype=jnp.int32)[None, :]).reshape(-1)
    row_src = jnp.full((n_blocks * rows,), -1, jnp.int32).at[dest].set(code)
    row = jnp.arange(n_blocks * rows, dtype=jnp.int32)
    row_tok = jnp.maximum(row_src, 0) // 2
    row_dst = jnp.where(row_src >= 0, (row_src % 2) * plane + row_src // 2, n_tok + row % rows)
    n_blk = padded // rows
    first_blk = pstart // rows
    items_end = jnp.cumsum((n_blk + group - 1) // group)
    items_start = items_end - (n_blk + group - 1) // group
    n_items = items_end[-1].astype(jnp.int32)
    it = jnp.arange(max_items, dtype=jnp.int32)
    item_exp = jnp.minimum(jnp.sum(items_end[None, :] <= it[:, None], axis=1), N_EXPERTS - 1).astype(jnp.int32)
    run = it - items_start[item_exp]
    item_blk = first_blk[item_exp] + run * group
    item_cnt = jnp.clip(n_blk[item_exp] - run * group, 0, group)
    live = it < n_items
    item_cnt = jnp.where(live, item_cnt, 0).astype(jnp.int32)
    item_exp = jnp.where(live, item_exp, item_exp[jnp.maximum(n_items - 1, 0)])
    blk_ids = jnp.clip(item_blk[:, None] + jnp.arange(group, dtype=jnp.int32)[None, :], 0, n_blocks - 1)
    item_tok = row_tok.reshape(n_blocks, rows)[blk_ids].reshape(max_items, 1, group * rows)
    item_dst = row_dst.reshape(n_blocks, rows)[blk_ids].reshape(max_items, 1, group * rows)
    return item_tok, item_dst, item_exp, item_cnt, n_items.reshape(1)


def _final_kernel(x_ref, route_ref, ya_ref, yb_ref, g_ref, o_ref, *, eps):
    x = _add_expert_outputs(x_ref[...], route_ref, ya_ref, yb_ref)
    ms = jnp.mean(x * x, axis=-1, keepdims=True)
    o_ref[...] = x * lax.rsqrt(ms + eps) * g_ref[...]


def _final_norm(x, moe, g, *, tm=512):
    t, d = x.shape
    route, planes = moe
    row = lambda i: (i, 0)
    return pl.pallas_call(
        functools.partial(_final_kernel, eps=RMS_EPS),
        grid=(t // tm,),
        in_specs=[pl.BlockSpec((tm, d), row), pl.BlockSpec((tm, LANES), row)]
                 + _expert_output_specs(tm, d, t)
                 + [pl.BlockSpec((1, d), lambda i: (0, 0))],
        out_specs=pl.BlockSpec((tm, d), row),
        out_shape=jax.ShapeDtypeStruct((t, d), F32),
        compiler_params=_params("arbitrary"),
    )(x, route, planes, planes, g.reshape(1, d))


def kernel(x, rel_bias, norm_attn, w_in, gain_a, gain_b, lam_q1, lam_k1, lam_q2, lam_k2, w_out, norm_ffn,
           w_router_group, w_router_expert, w_gate, w_up, w_down, norm_final):
    batch, seq, d = x.shape
    depth = w_in.shape[0]
    n_heads_a = gain_a.shape[1]
    n_heads_b = gain_b.shape[1]
    t = batch * seq
    xs = x.reshape(t, d)
    w_in, w_out = w_in.astype(BF16), w_out.astype(BF16)
    dils = tuple(dil for _, dil in DILATED_PATTERNS)
    extra_dils = tuple(dil for dil in dils if dil > 1)
    moe = None
    for l in range(depth):
        proj, views, xs = _norm_proj(xs, moe, norm_attn[l], w_in, l, extra_dils, 3)
        views = dict(zip(extra_dils, views))
        parts = [_dilated_attention(views.get(dil, proj), rel_bias, batch, seq, n_heads_a, w, dil)
                 for (w, dil) in DILATED_PATTERNS]
        out_a = _dilated_mixture([p[0] for p in parts], [p[1] for p in parts], dils, gain_a[l])

        lam_init = 0.8 - 0.6 * math.exp(-0.3 * l)
        lam = (jnp.exp(jnp.sum(lam_q1[l].astype(F32) * lam_k1[l].astype(F32)))
               - jnp.exp(jnp.sum(lam_q2[l].astype(F32) * lam_k2[l].astype(F32))) + lam_init)
        out_b = _diff_attention(proj, rel_bias, gain_b[l], lam, lam_init, batch, seq, n_heads_a, n_heads_b)

        w_r = jnp.concatenate([w_router_group[l], w_router_expert[l].transpose(1, 0, 2).reshape(d, N_EXPERTS)], axis=1)
        w_r = jnp.pad(w_r.astype(F32), ((0, 0), (0, LANES - w_r.shape[1])))
        wr_hi = w_r.astype(BF16)
        wr_lo = (w_r - wr_hi.astype(F32)).astype(BF16)
        xs, hn, route, counts = _out_router(out_a, out_b, xs, w_out, l, norm_ffn[l], wr_hi, wr_lo)

        planes = _moe(hn, *_dispatch(route, counts, t), w_gate, w_up, w_down, l)
        moe = (route, planes)
    out = _final_norm(xs, moe, norm_final)
    return out.reshape(batch, seq, d)
```

```python
import functools
import math

import jax
import jax.numpy as jnp
from jax import lax
from jax.experimental import pallas as pl
from jax.experimental.pallas import tpu as pltpu

F32 = jnp.float32
BF16 = jnp.bfloat16

LANES = 128
HEAD_DIM = 128
DIFF_DIM = HEAD_DIM // 2
DILATED_PATTERNS = ((128, 1), (512, 4), (2048, 16))
DILATED_HALO = 64
N_REL_BUCKETS = 32
REL_MAX_DISTANCE = 128
N_GROUPS = 4
EXPERTS_PER_GROUP = 8
N_EXPERTS = N_GROUPS * EXPERTS_PER_GROUP
MOE_BLOCK = 256
MOE_GROUP = 3
SUBLANES = 8
ROW_COPY_PRIORITY = 1
RMS_EPS = 1e-6
SUBLN_EPS = 1e-5
NEG_BIG = -1e30
VMEM_LIMIT = 56 * 1024 * 1024

_NT = (((1,), (1,)), ((), ()))

_R_E0, _R_E1, _R_W0, _R_W1, _R_RANK0, _R_RANK1 = range(6)


def _t5_bucket(rel):
    nb = N_REL_BUCKETS // 2
    max_exact = nb // 2
    n = -rel
    ret = jnp.where(n < 0, nb, 0)
    n = jnp.abs(n)
    nf = jnp.maximum(n, 1).astype(jnp.float32)
    large = max_exact + (jnp.log(nf / max_exact) / math.log(REL_MAX_DISTANCE / max_exact)
                         * (nb - max_exact)).astype(jnp.int32)
    large = jnp.minimum(large, nb - 1)
    return (ret + jnp.where(n < max_exact, n, large)).astype(jnp.int32)


def _rel_bias(bias_table, rel):
    onehot = (_t5_bucket(rel).reshape(-1)[None, :] == jnp.arange(N_REL_BUCKETS)[:, None]).astype(F32)
    vals = jnp.dot(bias_table.astype(F32).T, onehot, precision=lax.Precision.HIGHEST)
    return vals.reshape((bias_table.shape[1],) + rel.shape)


def _params(*sem, **kwargs):
    return pltpu.CompilerParams(dimension_semantics=sem, vmem_limit_bytes=VMEM_LIMIT, **kwargs)


def _add_expert_outputs(x, route_ref, ya_ref, yb_ref):
    return x + route_ref[:, _R_W0:_R_W0 + 1] * ya_ref[...] + route_ref[:, _R_W1:_R_W1 + 1] * yb_ref[...]


def _expert_output_specs(tm, d, n_tok):
    blocks_per_plane, rem = divmod(n_tok + 2 * MOE_BLOCK, tm)
    assert rem == 0
    return [pl.BlockSpec((tm, d), lambda i, *_, p=p: (p * blocks_per_plane + i, 0)) for p in range(2)]


def _norm_proj_kernel(*refs, has_y, dils, n_split, eps):
    n_in = 6 if has_y else 3
    if has_y:
        x_ref, route_ref, ya_ref, yb_ref, g_ref, w_ref = refs[:n_in]
    else:
        x_ref, g_ref, w_ref = refs[:n_in]
    o_ref = refs[n_in]
    od_refs = refs[n_in + 1:n_in + 1 + len(dils)]
    rest = refs[n_in + 1 + len(dils):]
    xs_ref = rest[0] if has_y else None
    xn_ref, res_ref = rest[-2:]
    j = pl.program_id(1)

    @pl.when(j == 0)
    def _():
        x = x_ref[...]
        if has_y:
            x = _add_expert_outputs(x, route_ref, ya_ref, yb_ref)
            xs_ref[...] = x
        ms = jnp.mean(x * x, axis=-1, keepdims=True)
        xn_ref[...] = (x * lax.rsqrt(ms + eps) * g_ref[...]).astype(BF16)

    res = jnp.dot(xn_ref[...], w_ref[...], preferred_element_type=F32)
    o_ref[...] = res.astype(o_ref.dtype)

    @pl.when(j < n_split)
    def _():
        n_chunks, tm, _ = res_ref.shape
        tn = n_chunks * LANES
        for c in range(n_chunks):
            res_ref[c] = res[:, c * LANES:(c + 1) * LANES]
        for od_ref, dil in zip(od_refs, dils):
            for r in range(dil):
                for c in range(n_chunks):
                    col = r * tn + c * LANES
                    od_ref[:, col:col + LANES] = res_ref[c, pl.ds(r, tm // dil, stride=dil), :].astype(od_ref.dtype)


def _norm_proj(x, moe, g, w, layer, dils, n_split, *, tm=512, tn=1024):
    t, d = x.shape
    n = w.shape[2]
    has_y = moe is not None
    row = lambda i, j: (i, 0)
    in_specs = [pl.BlockSpec((tm, d), row)]
    args = [x]
    if has_y:
        route, planes = moe
        in_specs += [pl.BlockSpec((tm, LANES), row)]
        in_specs += _expert_output_specs(tm, d, t)
        args += [route, planes, planes]
    in_specs += [pl.BlockSpec((1, d), lambda i, j: (0, 0)), pl.BlockSpec((None, d, tn), lambda i, j: (layer, 0, j))]
    args += [g.reshape(1, d), w]
    out_shape = [jax.ShapeDtypeStruct((t, n), BF16)]
    out_specs = [pl.BlockSpec((tm, tn), lambda i, j: (i, j))]
    for dil in dils:
        out_shape.append(jax.ShapeDtypeStruct((t // dil, n_split * dil * tn), BF16))
        out_specs.append(pl.BlockSpec((tm // dil, dil * tn), lambda i, j: (i, jnp.minimum(j, n_split - 1))))
    if has_y:
        out_shape.append(jax.ShapeDtypeStruct((t, d), F32))
        out_specs.append(pl.BlockSpec((tm, d), row))
    res = pl.pallas_call(
        functools.partial(_norm_proj_kernel, has_y=has_y, dils=dils, n_split=n_split, eps=RMS_EPS),
        grid=(t // tm, n // tn),
        in_specs=in_specs, out_specs=out_specs, out_shape=out_shape,
        scratch_shapes=[pltpu.VMEM((tm, d), BF16), pltpu.VMEM((tn // LANES, tm, LANES), F32)],
        compiler_params=_params("arbitrary", "arbitrary"),
    )(*args)
    views = list(res[1:1 + len(dils)])
    return res[0], views, (res[-1] if has_y else x)


def _dilated_kernel(q_ref, kp_ref, kc_ref, kn_ref, vp_ref, vc_ref, vn_ref, bias_ref, o_ref, lse_ref,
                    *, tq, n_sub, n_heads, seq_sub, scale):
    i = pl.program_id(2)
    halo = DILATED_HALO
    win = tq + 2 * halo
    lane = lax.broadcasted_iota(jnp.int32, (tq, LANES), 1)
    col = lax.broadcasted_iota(jnp.int32, (tq, win), 1)
    in_range, lse_all = [], []
    for t in range(n_sub):
        key_m = (i * n_sub + t) * tq - halo + col
        in_range.append((key_m >= 0) & (key_m < seq_sub))
        lse_all.append(jnp.zeros((tq, LANES), F32))
    for h in range(n_heads):
        cs = slice(h * HEAD_DIM, (h + 1) * HEAD_DIM)
        k = jnp.concatenate([kp_ref[:, cs], kc_ref[:, cs], kn_ref[:, cs]], axis=0)
        v = jnp.concatenate([vp_ref[:, cs], vc_ref[:, cs], vn_ref[:, cs]], axis=0)
        for t in range(n_sub):
            rows = slice(t * tq, (t + 1) * tq)
            keys = slice(t * tq, t * tq + win)
            s = lax.dot_general(q_ref[rows, cs], k[keys], _NT, preferred_element_type=F32) * scale + bias_ref[h]
            s = jnp.where(in_range[t], s, NEG_BIG)
            m = jnp.max(s, axis=-1, keepdims=True)
            p = jnp.exp(s - m)
            den = jnp.sum(p, axis=-1, keepdims=True)
            o_ref[rows, cs] = jnp.dot(p.astype(BF16), v[keys], preferred_element_type=F32) / den
            lse_all[t] = jnp.where(lane == h, m + jnp.log(den), lse_all[t])
    for t in range(n_sub):
        lse_ref[t * tq:(t + 1) * tq, :] = lse_all[t]


def _dilated_bias(bias_table, n_heads, window, dilation, tq):
    radius = window // (2 * dilation)
    assert radius <= DILATED_HALO
    win = tq + 2 * DILATED_HALO
    dm = jnp.arange(win)[None, :] - DILATED_HALO - jnp.arange(tq)[:, None]
    b = _rel_bias(bias_table, dm * dilation)[:n_heads]
    return jnp.where((jnp.abs(dm) <= radius)[None], b, NEG_BIG)


def _dilated_attention(view, bias_table, batch, seq, n_heads, window, dilation, *, tq=128, n_sub=2):
    t = view.shape[0] * dilation
    wa = n_heads * HEAD_DIM
    sub = seq // dilation
    n_sub = min(n_sub, sub // tq)
    step = n_sub * tq
    nq = sub // step
    halo = DILATED_HALO
    per_step = step // halo
    n_halo = sub // halo
    bias = _dilated_bias(bias_table, n_heads, window, dilation, tq)

    def main(which):
        return pl.BlockSpec((step, wa), lambda b, r, i: (b * nq + i, which * dilation + r))

    def edge(which, side):
        def imap(b, r, i):
            blk = i * per_step - 1 if side < 0 else (i + 1) * per_step
            return (b * n_halo + jnp.clip(blk, 0, n_halo - 1), which * dilation + r)
        return pl.BlockSpec((halo, wa), imap)

    return pl.pallas_call(
        functools.partial(_dilated_kernel, tq=tq, n_sub=n_sub, n_heads=n_heads, seq_sub=sub,
                          scale=HEAD_DIM ** -0.5),
        grid=(batch, dilation, nq),
        in_specs=[main(0), edge(1, -1), main(1), edge(1, 1), edge(2, -1), main(2), edge(2, 1),
                  pl.BlockSpec(bias.shape, lambda b, r, i: (0, 0, 0))],
        out_specs=[pl.BlockSpec((step, wa), lambda b, r, i: (b * nq + i, r)),
                   pl.BlockSpec((step, LANES), lambda b, r, i: (b * nq + i, r))],
        out_shape=[jax.ShapeDtypeStruct((t // dilation, dilation * wa), F32),
                   jax.ShapeDtypeStruct((t // dilation, dilation * LANES), F32)],
        compiler_params=_params("arbitrary", "arbitrary", "arbitrary"),
    )(view, view, view, view, view, view, view, bias)


def _mix_kernel(*refs, dils, n_heads, eps):
    n = len(dils)
    o_refs, l_refs = list(refs[:n]), list(refs[n:2 * n])
    g_ref, out_ref = refs[2 * n], refs[2 * n + 1]
    scratch = list(refs[2 * n + 2:])
    tm, wa = out_ref.shape
    def head_reader(ref):
        return lambda h: ref[:, h * HEAD_DIM:(h + 1) * HEAD_DIM]

    heads = [head_reader(o_ref) for o_ref in o_refs]
    for p, dil in enumerate(dils):
        if dil == 1:
            continue
        o_tok, l_tok = scratch.pop(0), scratch.pop(0)
        for r in range(dil):
            rows = pl.ds(r, tm // dil, stride=dil)
            for h in range(n_heads):
                o_tok[h, rows, :] = heads[p](r * n_heads + h)
            l_tok[rows, :] = l_refs[p][:, r * LANES:(r + 1) * LANES]
        heads[p], l_refs[p] = (lambda h, ref=o_tok: ref[h]), l_tok
    ls = [l_ref[...] for l_ref in l_refs]
    top = functools.reduce(jnp.maximum, ls)
    es = [jnp.exp(l - top) for l in ls]
    tot = functools.reduce(lambda a, b: a + b, es)
    ws = [e / tot for e in es]
    for h in range(n_heads):
        cs = slice(h * HEAD_DIM, (h + 1) * HEAD_DIM)
        mix = ws[0][:, h:h + 1] * heads[0](h)
        for p in range(1, n):
            mix = mix + ws[p][:, h:h + 1] * heads[p](h)
        ms = jnp.mean(mix * mix, axis=-1, keepdims=True)
        out_ref[:, cs] = (mix * lax.rsqrt(ms + eps) * g_ref[:, cs]).astype(out_ref.dtype)


def _dilated_mixture(outs, lses, dils, gain, *, tm=512):
    wa = gain.size
    t = outs[0].shape[0] * dils[0]
    n_heads = wa // HEAD_DIM
    row = lambda i: (i, 0)
    scratch = []
    for dil in dils:
        if dil > 1:
            scratch += [pltpu.VMEM((n_heads, tm, HEAD_DIM), F32), pltpu.VMEM((tm, LANES), F32)]
    return pl.pallas_call(
        functools.partial(_mix_kernel, dils=dils, n_heads=n_heads, eps=RMS_EPS),
        grid=(t // tm,),
        in_specs=[pl.BlockSpec((tm // dil, dil * wa), row) for dil in dils]
                 + [pl.BlockSpec((tm // dil, dil * LANES), row) for dil in dils]
                 + [pl.BlockSpec((1, wa), lambda i: (0, 0))],
        out_specs=pl.BlockSpec((tm, wa), row),
        out_shape=jax.ShapeDtypeStruct((t, wa), BF16),
        scratch_shapes=scratch,
        compiler_params=_params("arbitrary"),
    )(*outs, *lses, gain.reshape(1, wa))


def _diff_attn_kernel(lam_ref, q_ref, k_ref, v_ref, bias_ref, g_ref, o_ref, s_scr, p_scr, v1_scr,
                      *, tile, q_tiles, n_tiles, eps, post_scale):
    step = pl.program_id(2)
    lane = lax.broadcasted_iota(jnp.int32, (q_tiles * tile, HEAD_DIM), 1)
    qs = q_ref[...] * (DIFF_DIM ** -0.5)
    zero = jnp.zeros_like(qs)
    q_maps = (jnp.where(lane < DIFF_DIM, qs, zero), jnp.where(lane >= DIFF_DIM, qs, zero))
    chunks = tile // LANES
    streams = [(qt, mp) for qt in range(q_tiles) for mp in range(2)]

    def fold(op, acc, vals):
        for c in range(chunks):
            acc = op(acc, vals[:, c * LANES:(c + 1) * LANES])
        return acc

    def scores(n, kt, run_max):
        qt, mp = streams[n]
        ks = slice(kt * tile, (kt + 1) * tile)
        rel_class = jnp.clip(kt - (step * q_tiles + qt), -2, 2) + 2
        q = q_maps[mp][qt * tile:(qt + 1) * tile]
        s = lax.dot_general(q, k_ref[ks, :], _NT, preferred_element_type=F32) + bias_ref[0, rel_class]
        s_scr[n % 2, :, ks] = s
        return fold(jnp.maximum, run_max, s)

    def weights(n, kt, m):
        ks = slice(kt * tile, (kt + 1) * tile)
        p_scr[n % 2, :, ks] = jnp.exp(s_scr[n % 2, :, ks] - m).astype(BF16)

    def attend(n):
        acc = jnp.dot(p_scr[n % 2], v1_scr[...], preferred_element_type=F32)
        return acc[:, :HEAD_DIM] / acc[:, HEAD_DIM:]

    @pl.when(step == 0)
    def _():
        v1_scr[:, :HEAD_DIM] = v_ref[...]
        v1_scr[:, HEAD_DIM:] = jnp.ones((v1_scr.shape[0], HEAD_DIM), BF16)

    neg_inf = jnp.full((tile, LANES), -jnp.inf, F32)
    run_max = neg_inf
    for kt in range(n_tiles):
        run_max = scores(0, kt, run_max)
    outs = []
    for n in range(len(streams)):
        m = jnp.max(run_max, axis=-1, keepdims=True)
        run_max = neg_inf
        for kt in range(n_tiles):
            weights(n, kt, m)
            if n + 1 < len(streams):
                run_max = scores(n + 1, kt, run_max)
        outs.append(attend(n))
    for qt in range(q_tiles):
        o = outs[2 * qt] - lam_ref[0] * outs[2 * qt + 1]
        ms = jnp.mean(o * o, axis=-1, keepdims=True)
        o_ref[qt * tile:(qt + 1) * tile, :] = ((o * lax.rsqrt(ms + eps) * g_ref[0]) * post_scale).astype(o_ref.dtype)


def _diff_attention(proj, bias_table, gain, lam, lam_init, batch, seq, n_heads_a, n_heads_b, *, tile=256, q_tiles=2):
    t, pw = proj.shape
    n_tiles = seq // tile
    q0 = 3 * n_heads_a
    k0 = q0 + n_heads_b
    v0 = k0 + n_heads_b
    assert tile + 1 >= REL_MAX_DISTANCE
    rel = (jnp.arange(5)[:, None, None] - 2) * tile + jnp.arange(tile)[None, None, :] - jnp.arange(tile)[None, :, None]
    bias = _rel_bias(bias_table, rel)[n_heads_a:]
    n_steps = n_tiles // q_tiles
    rows = q_tiles * tile
    return pl.pallas_call(
        functools.partial(_diff_attn_kernel, tile=tile, q_tiles=q_tiles, n_tiles=n_tiles, eps=SUBLN_EPS,
                          post_scale=1.0 - lam_init),
        grid=(batch, n_heads_b, n_steps),
        in_specs=[pl.BlockSpec(memory_space=pltpu.SMEM),
                  pl.BlockSpec((rows, HEAD_DIM), lambda b, h, i: (b * n_steps + i, q0 + h)),
                  pl.BlockSpec((seq, HEAD_DIM), lambda b, h, i: (b, k0 + h)),
                  pl.BlockSpec((seq, HEAD_DIM), lambda b, h, i: (b, v0 + h)),
                  pl.BlockSpec((1, 5, tile, tile), lambda b, h, i: (h, 0, 0, 0)),
                  pl.BlockSpec((1, 1, HEAD_DIM), lambda b, h, i: (h, 0, 0))],
        out_specs=pl.BlockSpec((rows, HEAD_DIM), lambda b, h, i: (b * n_steps + i, h)),
        out_shape=jax.ShapeDtypeStruct((t, n_heads_b * HEAD_DIM), BF16),
        scratch_shapes=[pltpu.VMEM((2, tile, seq), F32), pltpu.VMEM((2, tile, seq), BF16),
                        pltpu.VMEM((seq, 2 * HEAD_DIM), BF16)],
        compiler_params=_params("arbitrary", "arbitrary", "arbitrary"),
    )(lam.reshape(1), proj, proj, proj, bias, gain.reshape(n_heads_b, 1, HEAD_DIM))


def _out_router_kernel(a_ref, b_ref, x_ref, wo_ref, g_ref, wrh_ref, wrl_ref,
                       x1_ref, hn_ref, route_ref, cnt_ref, *, tm, sub, wa, eps):
    @pl.when(pl.program_id(0) == 0)
    def _():
        cnt_ref[...] = jnp.zeros(cnt_ref.shape, F32)

    lane = lax.broadcasted_iota(jnp.int32, (sub, LANES), 1)
    neg_inf = jnp.float32(-jnp.inf)

    def project(rows):
        y = jnp.dot(a_ref[rows, :], wo_ref[:wa, :], preferred_element_type=F32)
        y = y + jnp.dot(b_ref[rows, :], wo_ref[wa:, :], preferred_element_type=F32)
        x1 = x_ref[rows, :] + y
        x1_ref[rows, :] = x1
        ms = jnp.mean(x1 * x1, axis=-1, keepdims=True)
        hn = x1 * lax.rsqrt(ms + eps) * g_ref[...]
        hn_ref[rows, :] = hn
        hi = hn.astype(BF16)
        lo = (hn - hi.astype(F32)).astype(BF16)
        return (jnp.dot(hi, wrh_ref[...], preferred_element_type=F32)
                + jnp.dot(lo, wrh_ref[...], preferred_element_type=F32)
                + jnp.dot(hi, wrl_ref[...], preferred_element_type=F32))

    def first_max(vals):
        top = jnp.max(vals, axis=-1, keepdims=True)
        idx = jnp.min(jnp.where(vals == top, lane, LANES), axis=-1, keepdims=True)
        return top, idx

    def route(rows, logits):
        is_grp = lane < N_GROUPS
        g_top, g_idx = first_max(jnp.where(is_grp, logits, neg_inf))
        g_sum = jnp.sum(jnp.where(is_grp, jnp.exp(logits - g_top), 0.0), axis=-1, keepdims=True)
        p_grp = 1.0 / g_sum
        e_lo = N_GROUPS + EXPERTS_PER_GROUP * g_idx
        e_vals = jnp.where((lane >= e_lo) & (lane < e_lo + EXPERTS_PER_GROUP), logits, neg_inf)
        top1, i1 = first_max(e_vals)
        top2, i2 = first_max(jnp.where(lane == i1, neg_inf, e_vals))
        b2 = jnp.exp(top2 - top1)
        w0 = p_grp / (1.0 + b2)
        w1 = p_grp * b2 / (1.0 + b2)

        oh0 = (lane == i1).astype(F32)
        oh1 = (lane == i2).astype(F32)
        both = oh0 + oh1
        r_i = lax.broadcasted_iota(jnp.int32, (sub, sub), 0)
        c_i = lax.broadcasted_iota(jnp.int32, (sub, sub), 1)
        before = jnp.where(r_i > c_i, 1.0, 0.0).astype(BF16)
        base = cnt_ref[...] + jnp.dot(before, both.astype(BF16), preferred_element_type=F32)
        rank0 = jnp.sum(base * oh0, axis=-1, keepdims=True)
        rank1 = jnp.sum(base * oh1, axis=-1, keepdims=True)
        cnt_ref[...] = cnt_ref[...] + jnp.sum(both, axis=0, keepdims=True)

        fields = {_R_E0: (i1 - N_GROUPS).astype(F32), _R_E1: (i2 - N_GROUPS).astype(F32),
                  _R_W0: w0, _R_W1: w1, _R_RANK0: rank0, _R_RANK1: rank1}
        slab = jnp.zeros((sub, LANES), F32)
        for n, f in fields.items():
            slab = jnp.where(lane == n, f, slab)
        route_ref[rows, :] = slab

    row_sets = [slice(k * sub, (k + 1) * sub) for k in range(tm // sub)]
    logits = [project(rows) for rows in row_sets]
    for rows, lg in zip(row_sets, logits):
        route(rows, lg)


def _out_router(out_a, out_b, x, w_out, layer, g, wr_hi, wr_lo, *, tm=512, sub=256):
    t, d = x.shape
    wa = out_a.shape[1]
    wb = out_b.shape[1]
    row = lambda i: (i, 0)
    const = lambda i: (0, 0)
    return pl.pallas_call(
        functools.partial(_out_router_kernel, tm=tm, sub=sub, wa=wa, eps=RMS_EPS),
        grid=(t // tm,),
        in_specs=[pl.BlockSpec((tm, wa), row), pl.BlockSpec((tm, wb), row), pl.BlockSpec((tm, d), row),
                  pl.BlockSpec((None, wa + wb, d), lambda i: (layer, 0, 0)), pl.BlockSpec((1, d), const),
                  pl.BlockSpec((d, LANES), const), pl.BlockSpec((d, LANES), const)],
        out_specs=[pl.BlockSpec((tm, d), row), pl.BlockSpec((tm, d), row),
                   pl.BlockSpec((tm, LANES), row), pl.BlockSpec((1, LANES), const)],
        out_shape=[jax.ShapeDtypeStruct((t, d), F32), jax.ShapeDtypeStruct((t, d), F32),
                   jax.ShapeDtypeStruct((t, LANES), F32), jax.ShapeDtypeStruct((1, LANES), F32)],
        compiler_params=_params("arbitrary"),
    )(out_a, out_b, x, w_out, g.reshape(1, d), wr_hi, wr_lo)


def _moe_kernel(nitems_ref, iexp_ref, icnt_ref, ilive_ref, tok_ref, nxt_ref, dst_ref, hn_hbm, wg_ref, wu_ref, wd_ref, out_hbm,
                xbuf, ybuf, gsem, ssem, *, rows, dump_rows):
    del iexp_ref
    it = pl.program_id(0)
    first_half = pl.program_id(1) == 0
    n_items = nitems_ref[0]
    cnt = icnt_ref[it]
    live = ilive_ref[it]
    buf = lax.rem(it, 2)
    tiles = rows // SUBLANES
    d = ybuf.shape[-1]

    def issue_rows(first_tile, n_tiles, copy_row):
        def body(g, carry):
            for u in range(SUBLANES):
                copy_row(first_tile + g, u).start(priority=ROW_COPY_PRIORITY)
            return carry
        lax.fori_loop(0, n_tiles, body, 0)

    def hbm_row(ref, row):
        return ref.at[lax.shift_right_logical(row, 3), pl.ds(row & (SUBLANES - 1), 1)]

    def start_gather(idx_ref, s, live_tiles):
        issue_rows(0, live_tiles, lambda g, u: pltpu.make_async_copy(
            hbm_row(hn_hbm, idx_ref[0, 0, g * SUBLANES + u]), xbuf.at[s, g, pl.ds(u, 1)], gsem.at[s]))

    def start_scatter(j, live_tiles):
        issue_rows(j * tiles, jnp.clip(live_tiles - j * tiles, 0, tiles), lambda g, u: pltpu.make_async_copy(
            ybuf.at[g, pl.ds(u, 1)], hbm_row(out_hbm, dst_ref[0, 0, g * SUBLANES + u]), ssem.at[0]))

    def wait_tiles(n_tiles, make_tile_copy):
        def body(j, carry):
            make_tile_copy().wait()
            return carry
        lax.fori_loop(0, n_tiles, body, 0)

    def wait_gather(s, live_tiles):
        wait_tiles(live_tiles, lambda: pltpu.make_async_copy(
            hn_hbm.at[pl.ds(0, 1)], xbuf.at[s, pl.ds(0, 1)], gsem.at[s]))

    def wait_scatter(live_tiles):
        wait_tiles(live_tiles, lambda: pltpu.make_async_copy(
            ybuf.at[pl.ds(0, 1)], out_hbm.at[pl.ds(0, 1)], ssem.at[0]))

    def block_tiles(j):
        return pl.ds(pl.multiple_of(j * tiles, tiles), tiles)

    def ffn_half(j):
        x = xbuf[buf, block_tiles(j)].reshape(rows, d)
        gate = jnp.dot(x, wg_ref[0], preferred_element_type=F32)
        up = jnp.dot(x, wu_ref[0], preferred_element_type=F32)
        hid = gate * (1.0 / (1.0 + jnp.exp(-gate))) * up
        return jnp.dot(hid, wd_ref[0], preferred_element_type=F32).reshape(tiles, SUBLANES, d)

    @pl.when(jnp.logical_and(it < n_items, first_half))
    def _():
        @pl.when(it == 0)
        def _():
            ybuf[pl.ds(0, tiles)] = jnp.zeros((tiles, SUBLANES, d), F32)
            for start in dump_rows:
                dump = pltpu.make_async_copy(ybuf.at[pl.ds(0, tiles)],
                                             out_hbm.at[pl.ds(start // SUBLANES, tiles)], ssem.at[0])
                dump.start()
                dump.wait()

            def clear(g, carry):
                for s in range(2):
                    xbuf[s, g] = jnp.zeros((SUBLANES, d), F32)
                return carry
            lax.fori_loop(0, xbuf.shape[1], clear, 0)
            start_gather(tok_ref, buf, live)

        wait_gather(buf, live)

        def body(j, carry):
            y = ffn_half(j)

            @pl.when(jnp.logical_and(j == 0, it >= 1))
            def _():
                wait_scatter(ilive_ref[jnp.maximum(it - 1, 0)])

            ybuf[block_tiles(j)] = y
            return carry
        lax.fori_loop(0, cnt, body, 0)

    @pl.when(jnp.logical_and(it < n_items, jnp.logical_not(first_half)))
    def _():
        @pl.when(it + 1 < n_items)
        def _():
            start_gather(nxt_ref, 1 - buf, ilive_ref[jnp.minimum(it + 1, ilive_ref.shape[0] - 1)])

        def body(j, carry):
            ybuf[block_tiles(j)] = ybuf[block_tiles(j)] + ffn_half(j)
            start_scatter(j, live)
            return carry
        lax.fori_loop(0, cnt, body, 0)

        @pl.when(it + 1 >= n_items)
        def _():
            wait_scatter(live)


def _plane_rows(n_tok):
    return n_tok + 2 * MOE_BLOCK


def _moe(hn, item_tok, item_dst, item_exp, item_cnt, item_live, n_items, w_gate, w_up, w_down, layer):
    t, d = hn.shape
    max_items = item_tok.shape[0]
    rows = MOE_BLOCK
    item_rows = MOE_GROUP * rows
    ffn = w_gate.shape[3]
    plane = _plane_rows(t)
    dump_rows = tuple(p * plane + t + s * rows for p in range(2) for s in range(2))
    half_ffn = ffn // 2

    def half(it, h, ni):
        return jnp.where(it < ni[0], h, 1)

    grid_spec = pltpu.PrefetchScalarGridSpec(
        num_scalar_prefetch=4,
        grid=(max_items, 2),
        in_specs=[
            pl.BlockSpec((1, 1, item_rows), lambda it, h, ni, ie, ic, il: (it, 0, 0), memory_space=pltpu.SMEM),
            pl.BlockSpec((1, 1, item_rows), lambda it, h, ni, ie, ic, il: (jnp.minimum(it + 1, max_items - 1), 0, 0),
                         memory_space=pltpu.SMEM),
            pl.BlockSpec((1, 1, item_rows), lambda it, h, ni, ie, ic, il: (it, 0, 0), memory_space=pltpu.SMEM),
            pl.BlockSpec(memory_space=pl.ANY),
            pl.BlockSpec((None, 1, d, half_ffn), lambda it, h, ni, ie, ic, il: (layer, ie[it], 0, half(it, h, ni))),
            pl.BlockSpec((None, 1, d, half_ffn), lambda it, h, ni, ie, ic, il: (layer, ie[it], 0, half(it, h, ni))),
            pl.BlockSpec((None, 1, half_ffn, d), lambda it, h, ni, ie, ic, il: (layer, ie[it], half(it, h, ni), 0)),
        ],
        out_specs=pl.BlockSpec(memory_space=pl.ANY),
        scratch_shapes=[pltpu.VMEM((2, item_rows // SUBLANES, SUBLANES, d), F32),
                        pltpu.VMEM((item_rows // SUBLANES, SUBLANES, d), F32),
                        pltpu.SemaphoreType.DMA((2,)), pltpu.SemaphoreType.DMA((1,))],
    )
    out = pl.pallas_call(
        functools.partial(_moe_kernel, rows=rows, dump_rows=dump_rows),
        grid_spec=grid_spec,
        out_shape=jax.ShapeDtypeStruct((2 * plane // SUBLANES, SUBLANES, d), F32),
        compiler_params=_params("arbitrary", "arbitrary", disable_bounds_checks=True),
    )(n_items, item_exp, item_cnt, item_live, item_tok, item_tok, item_dst, hn.reshape(t // SUBLANES, SUBLANES, d),
      w_gate, w_up, w_down)
    return out.reshape(2 * plane, d)


def _dispatch(route, counts, n_tok):
    rows = MOE_BLOCK
    group = MOE_GROUP
    plane = _plane_rows(n_tok)
    n_blocks = -(-(2 * n_tok) // rows) + N_EXPERTS
    max_items = N_EXPERTS + n_blocks // group
    e = route[:, _R_E0:_R_E1 + 1].astype(jnp.int32)
    rank = route[:, _R_RANK0:_R_RANK1 + 1].astype(jnp.int32)
    cnt = counts[0, N_GROUPS:N_GROUPS + N_EXPERTS].astype(jnp.int32)
    padded = (cnt + rows - 1) // rows * rows
    pend = jnp.cumsum(padded)
    pstart = pend - padded
    dest = (pstart[e] + rank).reshape(-1)
    code = (2 * jnp.arange(n_tok, dtype=jnp.int32)[:, None] + jnp.arange(2, dtype=jnp.int32)[None, :]).reshape(-1)
    row_src = jnp.full((n_blocks * rows,), -1, jnp.int32).at[dest].set(code)
    row = jnp.arange(n_blocks * rows, dtype=jnp.int32)
    row_tok = jnp.maximum(row_src, 0) // 2
    row_dst = jnp.where(row_src >= 0, (row_src % 2) * plane + row_src // 2, n_tok + row % rows)
    n_blk = padded // rows
    first_blk = pstart // rows
    items_end = jnp.cumsum((n_blk + group - 1) // group)
    items_start = items_end - (n_blk + group - 1) // group
    n_items = items_end[-1].astype(jnp.int32)
    it = jnp.arange(max_items, dtype=jnp.int32)
    item_exp = jnp.minimum(jnp.sum(items_end[None, :] <= it[:, None], axis=1), N_EXPERTS - 1).astype(jnp.int32)
    run = it - items_start[item_exp]
    item_blk = first_blk[item_exp] + run * group
    item_cnt = jnp.clip(n_blk[item_exp] - run * group, 0, group)
    live = it < n_items
    item_cnt = jnp.where(live, item_cnt, 0).astype(jnp.int32)
    item_exp = jnp.where(live, item_exp, item_exp[jnp.maximum(n_items - 1, 0)])
    blk_ids = jnp.clip(item_blk[:, None] + jnp.arange(group, dtype=jnp.int32)[None, :], 0, n_blocks - 1)
    item_tok = row_tok.reshape(n_blocks, rows)[blk_ids].reshape(max_items, 1, group * rows)
    item_dst = row_dst.reshape(n_blocks, rows)[blk_ids].reshape(max_items, 1, group * rows)
    real_rows = jnp.clip(cnt[item_exp] - run * group * rows, 0, item_cnt * rows)
    item_live = jnp.where(live, (real_rows + SUBLANES - 1) // SUBLANES, 0).astype(jnp.int32)
    return item_tok, item_dst, item_exp, item_cnt, item_live, n_items.reshape(1)


def _final_kernel(x_ref, route_ref, ya_ref, yb_ref, g_ref, o_ref, *, eps):
    x = _add_expert_outputs(x_ref[...], route_ref, ya_ref, yb_ref)
    ms = jnp.mean(x * x, axis=-1, keepdims=True)
    o_ref[...] = x * lax.rsqrt(ms + eps) * g_ref[...]


def _final_norm(x, moe, g, *, tm=512):
    t, d = x.shape
    route, planes = moe
    row = lambda i: (i, 0)
    return pl.pallas_call(
        functools.partial(_final_kernel, eps=RMS_EPS),
        grid=(t // tm,),
        in_specs=[pl.BlockSpec((tm, d), row), pl.BlockSpec((tm, LANES), row)]
                 + _expert_output_specs(tm, d, t)
                 + [pl.BlockSpec((1, d), lambda i: (0, 0))],
        out_specs=pl.BlockSpec((tm, d), row),
        out_shape=jax.ShapeDtypeStruct((t, d), F32),
        compiler_params=_params("arbitrary"),
    )(x, route, planes, planes, g.reshape(1, d))


def kernel(x, rel_bias, norm_attn, w_in, gain_a, gain_b, lam_q1, lam_k1, lam_q2, lam_k2, w_out, norm_ffn,
           w_router_group, w_router_expert, w_gate, w_up, w_down, norm_final):
    batch, seq, d = x.shape
    depth = w_in.shape[0]
    n_heads_a = gain_a.shape[1]
    n_heads_b = gain_b.shape[1]
    t = batch * seq
    xs = x.reshape(t, d)
    w_in, w_out = w_in.astype(BF16), w_out.astype(BF16)
    dils = tuple(dil for _, dil in DILATED_PATTERNS)
    extra_dils = tuple(dil for dil in dils if dil > 1)
    moe = None
    for l in range(depth):
        proj, views, xs = _norm_proj(xs, moe, norm_attn[l], w_in, l, extra_dils, 3)
        views = dict(zip(extra_dils, views))
        parts = [_dilated_attention(views.get(dil, proj), rel_bias, batch, seq, n_heads_a, w, dil)
                 for (w, dil) in DILATED_PATTERNS]
        out_a = _dilated_mixture([p[0] for p in parts], [p[1] for p in parts], dils, gain_a[l])

        lam_init = 0.8 - 0.6 * math.exp(-0.3 * l)
        lam = (jnp.exp(jnp.sum(lam_q1[l].astype(F32) * lam_k1[l].astype(F32)))
               - jnp.exp(jnp.sum(lam_q2[l].astype(F32) * lam_k2[l].astype(F32))) + lam_init)
        out_b = _diff_attention(proj, rel_bias, gain_b[l], lam, lam_init, batch, seq, n_heads_a, n_heads_b)

        w_r = jnp.concatenate([w_router_group[l], w_router_expert[l].transpose(1, 0, 2).reshape(d, N_EXPERTS)], axis=1)
        w_r = jnp.pad(w_r.astype(F32), ((0, 0), (0, LANES - w_r.shape[1])))
        wr_hi = w_r.astype(BF16)
        wr_lo = (w_r - wr_hi.astype(F32)).astype(BF16)
        xs, hn, route, counts = _out_router(out_a, out_b, xs, w_out, l, norm_ffn[l], wr_hi, wr_lo)

        planes = _moe(hn, *_dispatch(route, counts, t), w_gate, w_up, w_down, l)
        moe = (route, planes)
    out = _final_norm(xs, moe, norm_final)
    return out.reshape(batch, seq, d)
```

```python
import functools
import math

import jax
import jax.numpy as jnp
from jax import lax
from jax.experimental import pallas as pl
from jax.experimental.pallas import tpu as pltpu

F32 = jnp.float32
BF16 = jnp.bfloat16

LANES = 128
HEAD_DIM = 128
DIFF_DIM = HEAD_DIM // 2
DILATED_PATTERNS = ((128, 1), (512, 4), (2048, 16))
DILATED_HALO = 64
N_REL_BUCKETS = 32
REL_MAX_DISTANCE = 128
N_GROUPS = 4
EXPERTS_PER_GROUP = 8
N_EXPERTS = N_GROUPS * EXPERTS_PER_GROUP
MOE_BLOCK = 256
MOE_GROUP = 3
SUBLANES = 8
ROW_COPY_PRIORITY = 1
RMS_EPS = 1e-6
SUBLN_EPS = 1e-5
NEG_BIG = -1e30
VMEM_LIMIT = 56 * 1024 * 1024

_NT = (((1,), (1,)), ((), ()))

_R_E0, _R_E1, _R_W0, _R_W1, _R_RANK0, _R_RANK1 = range(6)


def _t5_bucket(rel):
    nb = N_REL_BUCKETS // 2
    max_exact = nb // 2
    n = -rel
    ret = jnp.where(n < 0, nb, 0)
    n = jnp.abs(n)
    nf = jnp.maximum(n, 1).astype(jnp.float32)
    large = max_exact + (jnp.log(nf / max_exact) / math.log(REL_MAX_DISTANCE / max_exact)
                         * (nb - max_exact)).astype(jnp.int32)
    large = jnp.minimum(large, nb - 1)
    return (ret + jnp.where(n < max_exact, n, large)).astype(jnp.int32)


def _rel_bias(bias_table, rel):
    onehot = (_t5_bucket(rel).reshape(-1)[None, :] == jnp.arange(N_REL_BUCKETS)[:, None]).astype(F32)
    vals = jnp.dot(bias_table.astype(F32).T, onehot, precision=lax.Precision.HIGHEST)
    return vals.reshape((bias_table.shape[1],) + rel.shape)


def _params(*sem, **kwargs):
    return pltpu.CompilerParams(dimension_semantics=sem, vmem_limit_bytes=VMEM_LIMIT, **kwargs)


def _add_expert_outputs(x, route_ref, ya_ref, yb_ref):
    return x + route_ref[:, _R_W0:_R_W0 + 1] * ya_ref[...] + route_ref[:, _R_W1:_R_W1 + 1] * yb_ref[...]


def _expert_output_specs(tm, d, n_tok):
    blocks_per_plane, rem = divmod(n_tok + 2 * MOE_BLOCK, tm)
    assert rem == 0
    return [pl.BlockSpec((tm, d), lambda i, *_, p=p: (p * blocks_per_plane + i, 0)) for p in range(2)]


def _norm_proj_kernel(*refs, has_y, dils, n_split, eps):
    n_in = 6 if has_y else 3
    if has_y:
        x_ref, route_ref, ya_ref, yb_ref, g_ref, w_ref = refs[:n_in]
    else:
        x_ref, g_ref, w_ref = refs[:n_in]
    o_ref = refs[n_in]
    od_refs = refs[n_in + 1:n_in + 1 + len(dils)]
    rest = refs[n_in + 1 + len(dils):]
    xs_ref = rest[0] if has_y else None
    xn_ref, res_ref = rest[-2:]
    j = pl.program_id(1)

    @pl.when(j == 0)
    def _():
        x = x_ref[...]
        if has_y:
            x = _add_expert_outputs(x, route_ref, ya_ref, yb_ref)
            xs_ref[...] = x
        ms = jnp.mean(x * x, axis=-1, keepdims=True)
        xn_ref[...] = (x * lax.rsqrt(ms + eps) * g_ref[...]).astype(BF16)

    res = jnp.dot(xn_ref[...], w_ref[...], preferred_element_type=F32)
    o_ref[...] = res.astype(o_ref.dtype)

    @pl.when(j < n_split)
    def _():
        n_chunks, tm, _ = res_ref.shape
        tn = n_chunks * LANES
        for c in range(n_chunks):
            res_ref[c] = res[:, c * LANES:(c + 1) * LANES]
        for od_ref, dil in zip(od_refs, dils):
            for r in range(dil):
                for c in range(n_chunks):
                    col = r * tn + c * LANES
                    od_ref[:, col:col + LANES] = res_ref[c, pl.ds(r, tm // dil, stride=dil), :].astype(od_ref.dtype)


def _norm_proj(x, moe, g, w, layer, dils, n_split, *, tm=512, tn=1024):
    t, d = x.shape
    n = w.shape[2]
    has_y = moe is not None
    row = lambda i, j: (i, 0)
    in_specs = [pl.BlockSpec((tm, d), row)]
    args = [x]
    if has_y:
        route, planes = moe
        in_specs += [pl.BlockSpec((tm, LANES), row)]
        in_specs += _expert_output_specs(tm, d, t)
        args += [route, planes, planes]
    in_specs += [pl.BlockSpec((1, d), lambda i, j: (0, 0)), pl.BlockSpec((None, d, tn), lambda i, j: (layer, 0, j))]
    args += [g.reshape(1, d), w]
    out_shape = [jax.ShapeDtypeStruct((t, n), BF16)]
    out_specs = [pl.BlockSpec((tm, tn), lambda i, j: (i, j))]
    for dil in dils:
        out_shape.append(jax.ShapeDtypeStruct((t // dil, n_split * dil * tn), BF16))
        out_specs.append(pl.BlockSpec((tm // dil, dil * tn), lambda i, j: (i, jnp.minimum(j, n_split - 1))))
    if has_y:
        out_shape.append(jax.ShapeDtypeStruct((t, d), F32))
        out_specs.append(pl.BlockSpec((tm, d), row))
    res = pl.pallas_call(
        functools.partial(_norm_proj_kernel, has_y=has_y, dils=dils, n_split=n_split, eps=RMS_EPS),
        grid=(t // tm, n // tn),
        in_specs=in_specs, out_specs=out_specs, out_shape=out_shape,
        scratch_shapes=[pltpu.VMEM((tm, d), BF16), pltpu.VMEM((tn // LANES, tm, LANES), F32)],
        compiler_params=_params("arbitrary", "arbitrary"),
    )(*args)
    views = list(res[1:1 + len(dils)])
    return res[0], views, (res[-1] if has_y else x)


def _dilated_kernel(q_ref, kp_ref, kc_ref, kn_ref, vp_ref, vc_ref, vn_ref, bias_ref, o_ref, lse_ref,
                    *, tq, n_sub, n_heads, seq_sub, scale):
    i = pl.program_id(2)
    halo = DILATED_HALO
    win = tq + 2 * halo
    n_tiles = seq_sub // tq
    lane = lax.broadcasted_iota(jnp.int32, (tq, LANES), 1)
    variant = [((i * n_sub + t) == 0).astype(jnp.int32) + 2 * ((i * n_sub + t) == n_tiles - 1).astype(jnp.int32)
               for t in range(n_sub)]
    lse_all = [jnp.zeros((tq, LANES), F32) for _ in range(n_sub)]
    for h in range(n_heads):
        cs = slice(h * HEAD_DIM, (h + 1) * HEAD_DIM)
        k = jnp.concatenate([kp_ref[:, cs], kc_ref[:, cs], kn_ref[:, cs]], axis=0)
        v = jnp.concatenate([vp_ref[:, cs], vc_ref[:, cs], vn_ref[:, cs]], axis=0)
        for t in range(n_sub):
            rows = slice(t * tq, (t + 1) * tq)
            keys = slice(t * tq, t * tq + win)
            s = (lax.dot_general(q_ref[rows, cs], k[keys], _NT, preferred_element_type=F32) * scale
                 + bias_ref[variant[t], h])
            m = jnp.max(s, axis=-1, keepdims=True)
            p = jnp.exp(s - m)
            den = jnp.sum(p, axis=-1, keepdims=True)
            o_ref[rows, cs] = jnp.dot(p.astype(BF16), v[keys], preferred_element_type=F32) / den
            lse_all[t] = jnp.where(lane == h, m + jnp.log(den), lse_all[t])
    for t in range(n_sub):
        lse_ref[t * tq:(t + 1) * tq, :] = lse_all[t]


def _dilated_bias(bias_table, n_heads, window, dilation, tq):
    radius = window // (2 * dilation)
    assert radius <= DILATED_HALO
    win = tq + 2 * DILATED_HALO
    dm = jnp.arange(win)[None, :] - DILATED_HALO - jnp.arange(tq)[:, None]
    b = _rel_bias(bias_table, dm * dilation)[:n_heads]
    b = jnp.where((jnp.abs(dm) <= radius)[None], b, NEG_BIG)
    col = jnp.arange(win)
    variants = []
    for v in range(4):
        gone = ((col < DILATED_HALO) & bool(v & 1)) | ((col >= tq + DILATED_HALO) & bool(v & 2))
        variants.append(jnp.where(gone[None, None, :], NEG_BIG, b))
    return jnp.stack(variants)


def _dilated_attention(view, bias_table, batch, seq, n_heads, window, dilation, *, tq=128, n_sub=2):
    t = view.shape[0] * dilation
    wa = n_heads * HEAD_DIM
    sub = seq // dilation
    n_sub = min(n_sub, sub // tq)
    step = n_sub * tq
    nq = sub // step
    halo = DILATED_HALO
    per_step = step // halo
    n_halo = sub // halo
    bias = _dilated_bias(bias_table, n_heads, window, dilation, tq)

    def main(which):
        return pl.BlockSpec((step, wa), lambda b, r, i: (b * nq + i, which * dilation + r))

    def edge(which, side):
        def imap(b, r, i):
            blk = i * per_step - 1 if side < 0 else (i + 1) * per_step
            return (b * n_halo + jnp.clip(blk, 0, n_halo - 1), which * dilation + r)
        return pl.BlockSpec((halo, wa), imap)

    return pl.pallas_call(
        functools.partial(_dilated_kernel, tq=tq, n_sub=n_sub, n_heads=n_heads, seq_sub=sub,
                          scale=HEAD_DIM ** -0.5),
        grid=(batch, dilation, nq),
        in_specs=[main(0), edge(1, -1), main(1), edge(1, 1), edge(2, -1), main(2), edge(2, 1),
                  pl.BlockSpec(bias.shape, lambda b, r, i: (0, 0, 0, 0))],
        out_specs=[pl.BlockSpec((step, wa), lambda b, r, i: (b * nq + i, r)),
                   pl.BlockSpec((step, LANES), lambda b, r, i: (b * nq + i, r))],
        out_shape=[jax.ShapeDtypeStruct((t // dilation, dilation * wa), F32),
                   jax.ShapeDtypeStruct((t // dilation, dilation * LANES), F32)],
        compiler_params=_params("arbitrary", "arbitrary", "arbitrary"),
    )(view, view, view, view, view, view, view, bias)


def _mix_kernel(*refs, dils, n_heads, eps):
    n = len(dils)
    o_refs, l_refs = list(refs[:n]), list(refs[n:2 * n])
    g_ref, out_ref = refs[2 * n], refs[2 * n + 1]
    scratch = list(refs[2 * n + 2:])
    tm, wa = out_ref.shape
    def head_reader(ref):
        return lambda h: ref[:, h * HEAD_DIM:(h + 1) * HEAD_DIM]

    heads = [head_reader(o_ref) for o_ref in o_refs]
    for p, dil in enumerate(dils):
        if dil == 1:
            continue
        o_tok, l_tok = scratch.pop(0), scratch.pop(0)
        for r in range(dil):
            rows = pl.ds(r, tm // dil, stride=dil)
            for h in range(n_heads):
                o_tok[h, rows, :] = heads[p](r * n_heads + h)
            l_tok[rows, :] = l_refs[p][:, r * LANES:(r + 1) * LANES]
        heads[p], l_refs[p] = (lambda h, ref=o_tok: ref[h]), l_tok
    ls = [l_ref[...] for l_ref in l_refs]
    top = functools.reduce(jnp.maximum, ls)
    es = [jnp.exp(l - top) for l in ls]
    tot = functools.reduce(lambda a, b: a + b, es)
    ws = [e / tot for e in es]
    for h in range(n_heads):
        cs = slice(h * HEAD_DIM, (h + 1) * HEAD_DIM)
        mix = ws[0][:, h:h + 1] * heads[0](h)
        for p in range(1, n):
            mix = mix + ws[p][:, h:h + 1] * heads[p](h)
        ms = jnp.mean(mix * mix, axis=-1, keepdims=True)
        out_ref[:, cs] = (mix * lax.rsqrt(ms + eps) * g_ref[:, cs]).astype(out_ref.dtype)


def _dilated_mixture(outs, lses, dils, gain, *, tm=512):
    wa = gain.size
    t = outs[0].shape[0] * dils[0]
    n_heads = wa // HEAD_DIM
    row = lambda i: (i, 0)
    scratch = []
    for dil in dils:
        if dil > 1:
            scratch += [pltpu.VMEM((n_heads, tm, HEAD_DIM), F32), pltpu.VMEM((tm, LANES), F32)]
    return pl.pallas_call(
        functools.partial(_mix_kernel, dils=dils, n_heads=n_heads, eps=RMS_EPS),
        grid=(t // tm,),
        in_specs=[pl.BlockSpec((tm // dil, dil * wa), row) for dil in dils]
                 + [pl.BlockSpec((tm // dil, dil * LANES), row) for dil in dils]
                 + [pl.BlockSpec((1, wa), lambda i: (0, 0))],
        out_specs=pl.BlockSpec((tm, wa), row),
        out_shape=jax.ShapeDtypeStruct((t, wa), BF16),
        scratch_shapes=scratch,
        compiler_params=_params("arbitrary"),
    )(*outs, *lses, gain.reshape(1, wa))


def _diff_attn_kernel(lam_ref, q_ref, k_ref, v_ref, bias_ref, g_ref, o_ref, s_scr, p_scr, v1_scr,
                      *, tile, q_tiles, n_tiles, eps, post_scale):
    step = pl.program_id(2)
    lane = lax.broadcasted_iota(jnp.int32, (q_tiles * tile, HEAD_DIM), 1)
    qs = q_ref[...] * (DIFF_DIM ** -0.5)
    zero = jnp.zeros_like(qs)
    q_maps = (jnp.where(lane < DIFF_DIM, qs, zero), jnp.where(lane >= DIFF_DIM, qs, zero))
    chunks = tile // LANES
    streams = [(qt, mp) for qt in range(q_tiles) for mp in range(2)]

    def fold(op, acc, vals):
        for c in range(chunks):
            acc = op(acc, vals[:, c * LANES:(c + 1) * LANES])
        return acc

    def scores(n, kt, run_max):
        qt, mp = streams[n]
        ks = slice(kt * tile, (kt + 1) * tile)
        rel_class = jnp.clip(kt - (step * q_tiles + qt), -2, 2) + 2
        q = q_maps[mp][qt * tile:(qt + 1) * tile]
        s = lax.dot_general(q, k_ref[ks, :], _NT, preferred_element_type=F32) + bias_ref[0, rel_class]
        s_scr[n % 2, :, ks] = s
        return fold(jnp.maximum, run_max, s)

    def weights(n, kt, m):
        ks = slice(kt * tile, (kt + 1) * tile)
        p_scr[n % 2, :, ks] = jnp.exp(s_scr[n % 2, :, ks] - m).astype(BF16)

    def attend(n):
        acc = jnp.dot(p_scr[n % 2], v1_scr[...], preferred_element_type=F32)
        return acc[:, :HEAD_DIM] / acc[:, HEAD_DIM:]

    @pl.when(step == 0)
    def _():
        v1_scr[:, :HEAD_DIM] = v_ref[...]
        v1_scr[:, HEAD_DIM:] = jnp.ones((v1_scr.shape[0], HEAD_DIM), BF16)

    neg_inf = jnp.full((tile, LANES), -jnp.inf, F32)
    run_max = neg_inf
    for kt in range(n_tiles):
        run_max = scores(0, kt, run_max)
    outs = []
    for n in range(len(streams)):
        m = jnp.max(run_max, axis=-1, keepdims=True)
        run_max = neg_inf
        for kt in range(n_tiles):
            weights(n, kt, m)
            if n + 1 < len(streams):
                run_max = scores(n + 1, kt, run_max)
        outs.append(attend(n))
    for qt in range(q_tiles):
        o = outs[2 * qt] - lam_ref[0] * outs[2 * qt + 1]
        ms = jnp.mean(o * o, axis=-1, keepdims=True)
        o_ref[qt * tile:(qt + 1) * tile, :] = ((o * lax.rsqrt(ms + eps) * g_ref[0]) * post_scale).astype(o_ref.dtype)


def _diff_attention(proj, bias_table, gain, lam, lam_init, batch, seq, n_heads_a, n_heads_b, *, tile=256, q_tiles=4):
    t, pw = proj.shape
    n_tiles = seq // tile
    q0 = 3 * n_heads_a
    k0 = q0 + n_heads_b
    v0 = k0 + n_heads_b
    assert tile + 1 >= REL_MAX_DISTANCE
    rel = (jnp.arange(5)[:, None, None] - 2) * tile + jnp.arange(tile)[None, None, :] - jnp.arange(tile)[None, :, None]
    bias = _rel_bias(bias_table, rel)[n_heads_a:]
    n_steps = n_tiles // q_tiles
    rows = q_tiles * tile
    return pl.pallas_call(
        functools.partial(_diff_attn_kernel, tile=tile, q_tiles=q_tiles, n_tiles=n_tiles, eps=SUBLN_EPS,
                          post_scale=1.0 - lam_init),
        grid=(batch, n_heads_b, n_steps),
        in_specs=[pl.BlockSpec(memory_space=pltpu.SMEM),
                  pl.BlockSpec((rows, HEAD_DIM), lambda b, h, i: (b * n_steps + i, q0 + h)),
                  pl.BlockSpec((seq, HEAD_DIM), lambda b, h, i: (b, k0 + h)),
                  pl.BlockSpec((seq, HEAD_DIM), lambda b, h, i: (b, v0 + h)),
                  pl.BlockSpec((1, 5, tile, tile), lambda b, h, i: (h, 0, 0, 0)),
                  pl.BlockSpec((1, 1, HEAD_DIM), lambda b, h, i: (h, 0, 0))],
        out_specs=pl.BlockSpec((rows, HEAD_DIM), lambda b, h, i: (b * n_steps + i, h)),
        out_shape=jax.ShapeDtypeStruct((t, n_heads_b * HEAD_DIM), BF16),
        scratch_shapes=[pltpu.VMEM((2, tile, seq), F32), pltpu.VMEM((2, tile, seq), BF16),
                        pltpu.VMEM((seq, 2 * HEAD_DIM), BF16)],
        compiler_params=_params("arbitrary", "arbitrary", "arbitrary"),
    )(lam.reshape(1), proj, proj, proj, bias, gain.reshape(n_heads_b, 1, HEAD_DIM))


def _out_router_kernel(a_ref, b_ref, x_ref, wo_ref, g_ref, wrh_ref, wrl_ref,
                       x1_ref, hn_ref, route_ref, cnt_ref, *, tm, sub, wa, eps):
    @pl.when(pl.program_id(0) == 0)
    def _():
        cnt_ref[...] = jnp.zeros(cnt_ref.shape, F32)

    lane = lax.broadcasted_iota(jnp.int32, (sub, LANES), 1)
    neg_inf = jnp.float32(-jnp.inf)

    def project(rows):
        y = jnp.dot(a_ref[rows, :], wo_ref[:wa, :], preferred_element_type=F32)
        y = y + jnp.dot(b_ref[rows, :], wo_ref[wa:, :], preferred_element_type=F32)
        x1 = x_ref[rows, :] + y
        x1_ref[rows, :] = x1
        ms = jnp.mean(x1 * x1, axis=-1, keepdims=True)
        hn = x1 * lax.rsqrt(ms + eps) * g_ref[...]
        hn_ref[rows, :] = hn
        hi = hn.astype(BF16)
        lo = (hn - hi.astype(F32)).astype(BF16)
        return (jnp.dot(hi, wrh_ref[...], preferred_element_type=F32)
                + jnp.dot(lo, wrh_ref[...], preferred_element_type=F32)
                + jnp.dot(hi, wrl_ref[...], preferred_element_type=F32))

    def first_max(vals):
        top = jnp.max(vals, axis=-1, keepdims=True)
        idx = jnp.min(jnp.where(vals == top, lane, LANES), axis=-1, keepdims=True)
        return top, idx

    def route(rows, logits):
        is_grp = lane < N_GROUPS
        g_top, g_idx = first_max(jnp.where(is_grp, logits, neg_inf))
        g_sum = jnp.sum(jnp.where(is_grp, jnp.exp(logits - g_top), 0.0), axis=-1, keepdims=True)
        p_grp = 1.0 / g_sum
        e_lo = N_GROUPS + EXPERTS_PER_GROUP * g_idx
        e_vals = jnp.where((lane >= e_lo) & (lane < e_lo + EXPERTS_PER_GROUP), logits, neg_inf)
        top1, i1 = first_max(e_vals)
        top2, i2 = first_max(jnp.where(lane == i1, neg_inf, e_vals))
        b2 = jnp.exp(top2 - top1)
        w0 = p_grp / (1.0 + b2)
        w1 = p_grp * b2 / (1.0 + b2)

        oh0 = (lane == i1).astype(F32)
        oh1 = (lane == i2).astype(F32)
        both = oh0 + oh1
        r_i = lax.broadcasted_iota(jnp.int32, (sub, sub), 0)
        c_i = lax.broadcasted_iota(jnp.int32, (sub, sub), 1)
        before = jnp.where(r_i > c_i, 1.0, 0.0).astype(BF16)
        base = cnt_ref[...] + jnp.dot(before, both.astype(BF16), preferred_element_type=F32)
        rank0 = jnp.sum(base * oh0, axis=-1, keepdims=True)
        rank1 = jnp.sum(base * oh1, axis=-1, keepdims=True)
        cnt_ref[...] = cnt_ref[...] + jnp.sum(both, axis=0, keepdims=True)

        fields = {_R_E0: (i1 - N_GROUPS).astype(F32), _R_E1: (i2 - N_GROUPS).astype(F32),
                  _R_W0: w0, _R_W1: w1, _R_RANK0: rank0, _R_RANK1: rank1}
        slab = jnp.zeros((sub, LANES), F32)
        for n, f in fields.items():
            slab = jnp.where(lane == n, f, slab)
        route_ref[rows, :] = slab

    row_sets = [slice(k * sub, (k + 1) * sub) for k in range(tm // sub)]
    logits = [project(rows) for rows in row_sets]
    for rows, lg in zip(row_sets, logits):
        route(rows, lg)


def _out_router(out_a, out_b, x, w_out, layer, g, wr_hi, wr_lo, *, tm=512, sub=256):
    t, d = x.shape
    wa = out_a.shape[1]
    wb = out_b.shape[1]
    row = lambda i: (i, 0)
    const = lambda i: (0, 0)
    return pl.pallas_call(
        functools.partial(_out_router_kernel, tm=tm, sub=sub, wa=wa, eps=RMS_EPS),
        grid=(t // tm,),
        in_specs=[pl.BlockSpec((tm, wa), row), pl.BlockSpec((tm, wb), row), pl.BlockSpec((tm, d), row),
                  pl.BlockSpec((None, wa + wb, d), lambda i: (layer, 0, 0)), pl.BlockSpec((1, d), const),
                  pl.BlockSpec((d, LANES), const), pl.BlockSpec((d, LANES), const)],
        out_specs=[pl.BlockSpec((tm, d), row), pl.BlockSpec((tm, d), row),
                   pl.BlockSpec((tm, LANES), row), pl.BlockSpec((1, LANES), const)],
        out_shape=[jax.ShapeDtypeStruct((t, d), F32), jax.ShapeDtypeStruct((t, d), F32),
                   jax.ShapeDtypeStruct((t, LANES), F32), jax.ShapeDtypeStruct((1, LANES), F32)],
        compiler_params=_params("arbitrary"),
    )(out_a, out_b, x, w_out, g.reshape(1, d), wr_hi, wr_lo)


def _moe_kernel(nitems_ref, iexp_ref, icnt_ref, ilive_ref, tok_ref, nxt_ref, dst_ref, hn_hbm, wg_ref, wu_ref, wd_ref, out_hbm,
                xbuf, ybuf, gsem, ssem, *, rows, dump_rows):
    del iexp_ref
    it = pl.program_id(0)
    first_half = pl.program_id(1) == 0
    n_items = nitems_ref[0]
    cnt = icnt_ref[it]
    live = ilive_ref[it]
    buf = lax.rem(it, 2)
    tiles = rows // SUBLANES
    d = ybuf.shape[-1]

    def issue_rows(first_tile, n_tiles, copy_row):
        def body(g, carry):
            for u in range(SUBLANES):
                copy_row(first_tile + g, u).start(priority=ROW_COPY_PRIORITY)
            return carry
        lax.fori_loop(0, n_tiles, body, 0)

    def hbm_row(ref, row):
        return ref.at[lax.shift_right_logical(row, 3), pl.ds(row & (SUBLANES - 1), 1)]

    def start_gather(idx_ref, s, live_tiles):
        issue_rows(0, live_tiles, lambda g, u: pltpu.make_async_copy(
            hbm_row(hn_hbm, idx_ref[0, 0, g * SUBLANES + u]), xbuf.at[s, g, pl.ds(u, 1)], gsem.at[s]))

    def start_scatter(j, live_tiles):
        issue_rows(j * tiles, jnp.clip(live_tiles - j * tiles, 0, tiles), lambda g, u: pltpu.make_async_copy(
            ybuf.at[g, pl.ds(u, 1)], hbm_row(out_hbm, dst_ref[0, 0, g * SUBLANES + u]), ssem.at[0]))

    def wait_tiles(n_tiles, make_tile_copy):
        def body(j, carry):
            make_tile_copy().wait()
            return carry
        lax.fori_loop(0, n_tiles, body, 0)

    def wait_gather(s, live_tiles):
        wait_tiles(live_tiles, lambda: pltpu.make_async_copy(
            hn_hbm.at[pl.ds(0, 1)], xbuf.at[s, pl.ds(0, 1)], gsem.at[s]))

    def wait_scatter(live_tiles):
        wait_tiles(live_tiles, lambda: pltpu.make_async_copy(
            ybuf.at[pl.ds(0, 1)], out_hbm.at[pl.ds(0, 1)], ssem.at[0]))

    def block_tiles(j):
        return pl.ds(pl.multiple_of(j * tiles, tiles), tiles)

    def ffn_half(j):
        x = xbuf[buf, block_tiles(j)].reshape(rows, d)
        gate = jnp.dot(x, wg_ref[0], preferred_element_type=F32)
        up = jnp.dot(x, wu_ref[0], preferred_element_type=F32)
        hid = gate * (1.0 / (1.0 + jnp.exp(-gate))) * up
        return jnp.dot(hid, wd_ref[0], preferred_element_type=F32).reshape(tiles, SUBLANES, d)

    @pl.when(jnp.logical_and(it < n_items, first_half))
    def _():
        @pl.when(it == 0)
        def _():
            ybuf[pl.ds(0, tiles)] = jnp.zeros((tiles, SUBLANES, d), F32)
            for start in dump_rows:
                dump = pltpu.make_async_copy(ybuf.at[pl.ds(0, tiles)],
                                             out_hbm.at[pl.ds(start // SUBLANES, tiles)], ssem.at[0])
                dump.start()
                dump.wait()

            def clear(g, carry):
                for s in range(2):
                    xbuf[s, g] = jnp.zeros((SUBLANES, d), F32)
                return carry
            lax.fori_loop(0, xbuf.shape[1], clear, 0)
            start_gather(tok_ref, buf, live)

        wait_gather(buf, live)

        def body(j, carry):
            y = ffn_half(j)

            @pl.when(jnp.logical_and(j == 0, it >= 1))
            def _():
                wait_scatter(ilive_ref[jnp.maximum(it - 1, 0)])

            ybuf[block_tiles(j)] = y
            return carry
        lax.fori_loop(0, cnt, body, 0)

    @pl.when(jnp.logical_and(it < n_items, jnp.logical_not(first_half)))
    def _():
        @pl.when(it + 1 < n_items)
        def _():
            start_gather(nxt_ref, 1 - buf, ilive_ref[jnp.minimum(it + 1, ilive_ref.shape[0] - 1)])

        def body(j, carry):
            ybuf[block_tiles(j)] = ybuf[block_tiles(j)] + ffn_half(j)
            start_scatter(j, live)
            return carry
        lax.fori_loop(0, cnt, body, 0)

        @pl.when(it + 1 >= n_items)
        def _():
            wait_scatter(live)


def _plane_rows(n_tok):
    return n_tok + 2 * MOE_BLOCK


def _moe(hn, item_tok, item_dst, item_exp, item_cnt, item_live, n_items, w_gate, w_up, w_down, layer):
    t, d = hn.shape
    max_items = item_tok.shape[0]
    rows = MOE_BLOCK
    item_rows = MOE_GROUP * rows
    ffn = w_gate.shape[3]
    plane = _plane_rows(t)
    dump_rows = tuple(p * plane + t + s * rows for p in range(2) for s in range(2))
    half_ffn = ffn // 2

    def half(it, h, ni):
        return jnp.where(it < ni[0], h, 1)

    grid_spec = pltpu.PrefetchScalarGridSpec(
        num_scalar_prefetch=4,
        grid=(max_items, 2),
        in_specs=[
            pl.BlockSpec((1, 1, item_rows), lambda it, h, ni, ie, ic, il: (it, 0, 0), memory_space=pltpu.SMEM),
            pl.BlockSpec((1, 1, item_rows), lambda it, h, ni, ie, ic, il: (jnp.minimum(it + 1, max_items - 1), 0, 0),
                         memory_space=pltpu.SMEM),
            pl.BlockSpec((1, 1, item_rows), lambda it, h, ni, ie, ic, il: (it, 0, 0), memory_space=pltpu.SMEM),
            pl.BlockSpec(memory_space=pl.ANY),
            pl.BlockSpec((None, 1, d, half_ffn), lambda it, h, ni, ie, ic, il: (layer, ie[it], 0, half(it, h, ni))),
            pl.BlockSpec((None, 1, d, half_ffn), lambda it, h, ni, ie, ic, il: (layer, ie[it], 0, half(it, h, ni))),
            pl.BlockSpec((None, 1, half_ffn, d), lambda it, h, ni, ie, ic, il: (layer, ie[it], half(it, h, ni), 0)),
        ],
        out_specs=pl.BlockSpec(memory_space=pl.ANY),
        scratch_shapes=[pltpu.VMEM((2, item_rows // SUBLANES, SUBLANES, d), F32),
                        pltpu.VMEM((item_rows // SUBLANES, SUBLANES, d), F32),
                        pltpu.SemaphoreType.DMA((2,)), pltpu.SemaphoreType.DMA((1,))],
    )
    out = pl.pallas_call(
        functools.partial(_moe_kernel, rows=rows, dump_rows=dump_rows),
        grid_spec=grid_spec,
        out_shape=jax.ShapeDtypeStruct((2 * plane // SUBLANES, SUBLANES, d), F32),
        compiler_params=_params("arbitrary", "arbitrary", disable_bounds_checks=True),
    )(n_items, item_exp, item_cnt, item_live, item_tok, item_tok, item_dst, hn.reshape(t // SUBLANES, SUBLANES, d),
      w_gate, w_up, w_down)
    return out.reshape(2 * plane, d)


def _dispatch(route, counts, n_tok):
    rows = MOE_BLOCK
    group = MOE_GROUP
    plane = _plane_rows(n_tok)
    n_blocks = -(-(2 * n_tok) // rows) + N_EXPERTS
    max_items = N_EXPERTS + n_blocks // group
    e = route[:, _R_E0:_R_E1 + 1].astype(jnp.int32)
    rank = route[:, _R_RANK0:_R_RANK1 + 1].astype(jnp.int32)
    cnt = counts[0, N_GROUPS:N_GROUPS + N_EXPERTS].astype(jnp.int32)
    padded = (cnt + rows - 1) // rows * rows
    pend = jnp.cumsum(padded)
    pstart = pend - padded
    dest = (pstart[e] + rank).reshape(-1)
    code = (2 * jnp.arange(n_tok, dtype=jnp.int32)[:, None] + jnp.arange(2, dtype=jnp.int32)[None, :]).reshape(-1)
    row_src = jnp.full((n_blocks * rows,), -1, jnp.int32).at[dest].set(code)
    row = jnp.arange(n_blocks * rows, dtype=jnp.int32)
    row_tok = jnp.maximum(row_src, 0) // 2
    row_dst = jnp.where(row_src >= 0, (row_src % 2) * plane + row_src // 2, n_tok + row % rows)
    n_blk = padded // rows
    first_blk = pstart // rows
    items_end = jnp.cumsum((n_blk + group - 1) // group)
    items_start = items_end - (n_blk + group - 1) // group
    n_items = items_end[-1].astype(jnp.int32)
    it = jnp.arange(max_items, dtype=jnp.int32)
    item_exp = jnp.minimum(jnp.sum(items_end[None, :] <= it[:, None], axis=1), N_EXPERTS - 1).astype(jnp.int32)
    run = it - items_start[item_exp]
    item_blk = first_blk[item_exp] + run * group
    item_cnt = jnp.clip(n_blk[item_exp] - run * group, 0, group)
    live = it < n_items
    item_cnt = jnp.where(live, item_cnt, 0).astype(jnp.int32)
    item_exp = jnp.where(live, item_exp, item_exp[jnp.maximum(n_items - 1, 0)])
    blk_ids = jnp.clip(item_blk[:, None] + jnp.arange(group, dtype=jnp.int32)[None, :], 0, n_blocks - 1)
    item_tok = row_tok.reshape(n_blocks, rows)[blk_ids].reshape(max_items, 1, group * rows)
    item_dst = row_dst.reshape(n_blocks, rows)[blk_ids].reshape(max_items, 1, group * rows)
    real_rows = jnp.clip(cnt[item_exp] - run * group * rows, 0, item_cnt * rows)
    item_live = jnp.where(live, (real_rows + SUBLANES - 1) // SUBLANES, 0).astype(jnp.int32)
    return item_tok, item_dst, item_exp, item_cnt, item_live, n_items.reshape(1)


def _final_kernel(x_ref, route_ref, ya_ref, yb_ref, g_ref, o_ref, *, eps):
    x = _add_expert_outputs(x_ref[...], route_ref, ya_ref, yb_ref)
    ms = jnp.mean(x * x, axis=-1, keepdims=True)
    o_ref[...] = x * lax.rsqrt(ms + eps) * g_ref[...]


def _final_norm(x, moe, g, *, tm=512):
    t, d = x.shape
    route, planes = moe
    row = lambda i: (i, 0)
    return pl.pallas_call(
        functools.partial(_final_kernel, eps=RMS_EPS),
        grid=(t // tm,),
        in_specs=[pl.BlockSpec((tm, d), row), pl.BlockSpec((tm, LANES), row)]
                 + _expert_output_specs(tm, d, t)
                 + [pl.BlockSpec((1, d), lambda i: (0, 0))],
        out_specs=pl.BlockSpec((tm, d), row),
        out_shape=jax.ShapeDtypeStruct((t, d), F32),
        compiler_params=_params("arbitrary"),
    )(x, route, planes, planes, g.reshape(1, d))


def kernel(x, rel_bias, norm_attn, w_in, gain_a, gain_b, lam_q1, lam_k1, lam_q2, lam_k2, w_out, norm_ffn,
           w_router_group, w_router_expert, w_gate, w_up, w_down, norm_final):
    batch, seq, d = x.shape
    depth = w_in.shape[0]
    n_heads_a = gain_a.shape[1]
    n_heads_b = gain_b.shape[1]
    t = batch * seq
    xs = x.reshape(t, d)
    w_in, w_out = w_in.astype(BF16), w_out.astype(BF16)
    dils = tuple(dil for _, dil in DILATED_PATTERNS)
    extra_dils = tuple(dil for dil in dils if dil > 1)
    moe = None
    for l in range(depth):
        proj, views, xs = _norm_proj(xs, moe, norm_attn[l], w_in, l, extra_dils, 3)
        views = dict(zip(extra_dils, views))
        parts = [_dilated_attention(views.get(dil, proj), rel_bias, batch, seq, n_heads_a, w, dil)
                 for (w, dil) in DILATED_PATTERNS]
        out_a = _dilated_mixture([p[0] for p in parts], [p[1] for p in parts], dils, gain_a[l])

        lam_init = 0.8 - 0.6 * math.exp(-0.3 * l)
        lam = (jnp.exp(jnp.sum(lam_q1[l].astype(F32) * lam_k1[l].astype(F32)))
               - jnp.exp(jnp.sum(lam_q2[l].astype(F32) * lam_k2[l].astype(F32))) + lam_init)
        out_b = _diff_attention(proj, rel_bias, gain_b[l], lam, lam_init, batch, seq, n_heads_a, n_heads_b)

        w_r = jnp.concatenate([w_router_group[l], w_router_expert[l].transpose(1, 0, 2).reshape(d, N_EXPERTS)], axis=1)
        w_r = jnp.pad(w_r.astype(F32), ((0, 0), (0, LANES - w_r.shape[1])))
        wr_hi = w_r.astype(BF16)
        wr_lo = (w_r - wr_hi.astype(F32)).astype(BF16)
        xs, hn, route, counts = _out_router(out_a, out_b, xs, w_out, l, norm_ffn[l], wr_hi, wr_lo)

        planes = _moe(hn, *_dispatch(route, counts, t), w_gate, w_up, w_down, l)
        moe = (route, planes)
    out = _final_norm(xs, moe, norm_final)
    return out.reshape(batch, seq, d)
```

```python
import functools
import math

import jax
import jax.numpy as jnp
from jax import lax
from jax.experimental import pallas as pl
from jax.experimental.pallas import tpu as pltpu

F32 = jnp.float32
BF16 = jnp.bfloat16

LANES = 128
HEAD_DIM = 128
DIFF_DIM = HEAD_DIM // 2
DILATED_PATTERNS = ((128, 1), (512, 4), (2048, 16))
DILATED_HALO = 64
N_REL_BUCKETS = 32
REL_MAX_DISTANCE = 128
N_GROUPS = 4
EXPERTS_PER_GROUP = 8
N_EXPERTS = N_GROUPS * EXPERTS_PER_GROUP
MOE_BLOCK = 256
MOE_GROUP = 3
SUBLANES = 8
ROW_COPY_PRIORITY = 1
RMS_EPS = 1e-6
SUBLN_EPS = 1e-5
NEG_BIG = -1e30
VMEM_LIMIT = 56 * 1024 * 1024

_NT = (((1,), (1,)), ((), ()))

_R_E0, _R_E1, _R_W0, _R_W1, _R_RANK0, _R_RANK1 = range(6)


def _t5_bucket(rel):
    nb = N_REL_BUCKETS // 2
    max_exact = nb // 2
    n = -rel
    ret = jnp.where(n < 0, nb, 0)
    n = jnp.abs(n)
    nf = jnp.maximum(n, 1).astype(jnp.float32)
    large = max_exact + (jnp.log(nf / max_exact) / math.log(REL_MAX_DISTANCE / max_exact)
                         * (nb - max_exact)).astype(jnp.int32)
    large = jnp.minimum(large, nb - 1)
    return (ret + jnp.where(n < max_exact, n, large)).astype(jnp.int32)


def _rel_bias(bias_table, rel):
    onehot = (_t5_bucket(rel).reshape(-1)[None, :] == jnp.arange(N_REL_BUCKETS)[:, None]).astype(F32)
    vals = jnp.dot(bias_table.astype(F32).T, onehot, precision=lax.Precision.HIGHEST)
    return vals.reshape((bias_table.shape[1],) + rel.shape)


def _params(*sem, **kwargs):
    return pltpu.CompilerParams(dimension_semantics=sem, vmem_limit_bytes=VMEM_LIMIT, **kwargs)


def _add_expert_outputs(x, route_ref, ya_ref, yb_ref):
    return x + route_ref[:, _R_W0:_R_W0 + 1] * ya_ref[...] + route_ref[:, _R_W1:_R_W1 + 1] * yb_ref[...]


def _expert_output_specs(tm, d, n_tok):
    blocks_per_plane, rem = divmod(n_tok + 2 * MOE_BLOCK, tm)
    assert rem == 0
    return [pl.BlockSpec((tm, d), lambda i, *_, p=p: (p * blocks_per_plane + i, 0)) for p in range(2)]


def _norm_proj_kernel(*refs, has_y, dils, n_split, eps):
    n_in = 6 if has_y else 3
    if has_y:
        x_ref, route_ref, ya_ref, yb_ref, g_ref, w_ref = refs[:n_in]
    else:
        x_ref, g_ref, w_ref = refs[:n_in]
    o_ref = refs[n_in]
    od_refs = refs[n_in + 1:n_in + 1 + len(dils)]
    rest = refs[n_in + 1 + len(dils):]
    xs_ref = rest[0] if has_y else None
    xn_ref, res_ref = rest[-2:]
    j = pl.program_id(1)

    @pl.when(j == 0)
    def _():
        x = x_ref[...]
        if has_y:
            x = _add_expert_outputs(x, route_ref, ya_ref, yb_ref)
            xs_ref[...] = x
        ms = jnp.mean(x * x, axis=-1, keepdims=True)
        xn_ref[...] = (x * lax.rsqrt(ms + eps) * g_ref[...]).astype(BF16)

    res = jnp.dot(xn_ref[...], w_ref[...], preferred_element_type=F32)
    o_ref[...] = res.astype(o_ref.dtype)

    @pl.when(j < n_split)
    def _():
        n_chunks, tm, _ = res_ref.shape
        tn = n_chunks * LANES
        for c in range(n_chunks):
            res_ref[c] = res[:, c * LANES:(c + 1) * LANES]
        for od_ref, dil in zip(od_refs, dils):
            for r in range(dil):
                for c in range(n_chunks):
                    col = r * tn + c * LANES
                    od_ref[:, col:col + LANES] = res_ref[c, pl.ds(r, tm // dil, stride=dil), :].astype(od_ref.dtype)


def _norm_proj(x, moe, g, w, layer, dils, n_split, *, tm=512, tn=1024):
    t, d = x.shape
    n = w.shape[2]
    has_y = moe is not None
    row = lambda i, j: (i, 0)
    in_specs = [pl.BlockSpec((tm, d), row)]
    args = [x]
    if has_y:
        route, planes = moe
        in_specs += [pl.BlockSpec((tm, LANES), row)]
        in_specs += _expert_output_specs(tm, d, t)
        args += [route, planes, planes]
    in_specs += [pl.BlockSpec((1, d), lambda i, j: (0, 0)), pl.BlockSpec((None, d, tn), lambda i, j: (layer, 0, j))]
    args += [g.reshape(1, d), w]
    out_shape = [jax.ShapeDtypeStruct((t, n), BF16)]
    out_specs = [pl.BlockSpec((tm, tn), lambda i, j: (i, j))]
    for dil in dils:
        out_shape.append(jax.ShapeDtypeStruct((t // dil, n_split * dil * tn), BF16))
        out_specs.append(pl.BlockSpec((tm // dil, dil * tn), lambda i, j: (i, jnp.minimum(j, n_split - 1))))
    if has_y:
        out_shape.append(jax.ShapeDtypeStruct((t, d), F32))
        out_specs.append(pl.BlockSpec((tm, d), row))
    res = pl.pallas_call(
        functools.partial(_norm_proj_kernel, has_y=has_y, dils=dils, n_split=n_split, eps=RMS_EPS),
        grid=(t // tm, n // tn),
        in_specs=in_specs, out_specs=out_specs, out_shape=out_shape,
        scratch_shapes=[pltpu.VMEM((tm, d), BF16), pltpu.VMEM((tn // LANES, tm, LANES), F32)],
        compiler_params=_params("arbitrary", "arbitrary"),
    )(*args)
    views = list(res[1:1 + len(dils)])
    return res[0], views, (res[-1] if has_y else x)


def _dilated_kernel(q_ref, kp_ref, kc_ref, kn_ref, vp_ref, vc_ref, vn_ref, bias_ref, o_ref, lse_ref,
                    *, tq, n_sub, n_heads, seq_sub, scale):
    i = pl.program_id(2)
    halo = DILATED_HALO
    win = tq + 2 * halo
    n_tiles = seq_sub // tq
    lane = lax.broadcasted_iota(jnp.int32, (tq, LANES), 1)
    variant = [((i * n_sub + t) == 0).astype(jnp.int32) + 2 * ((i * n_sub + t) == n_tiles - 1).astype(jnp.int32)
               for t in range(n_sub)]
    lse_all = [jnp.zeros((tq, LANES), F32) for _ in range(n_sub)]
    for h in range(n_heads):
        cs = slice(h * HEAD_DIM, (h + 1) * HEAD_DIM)
        k = jnp.concatenate([kp_ref[:, cs], kc_ref[:, cs], kn_ref[:, cs]], axis=0)
        v = jnp.concatenate([vp_ref[:, cs], vc_ref[:, cs], vn_ref[:, cs]], axis=0)
        for t in range(n_sub):
            rows = slice(t * tq, (t + 1) * tq)
            keys = slice(t * tq, t * tq + win)
            s = (lax.dot_general(q_ref[rows, cs], k[keys], _NT, preferred_element_type=F32) * scale
                 + bias_ref[variant[t], h])
            m = jnp.max(s, axis=-1, keepdims=True)
            p = jnp.exp(s - m)
            den = jnp.sum(p, axis=-1, keepdims=True)
            o_ref[rows, cs] = jnp.dot(p.astype(BF16), v[keys], preferred_element_type=F32) / den
            lse_all[t] = jnp.where(lane == h, m + jnp.log(den), lse_all[t])
    for t in range(n_sub):
        lse_ref[t * tq:(t + 1) * tq, :] = lse_all[t]


def _dilated_bias(bias_table, n_heads, window, dilation, tq):
    radius = window // (2 * dilation)
    assert radius <= DILATED_HALO
    win = tq + 2 * DILATED_HALO
    dm = jnp.arange(win)[None, :] - DILATED_HALO - jnp.arange(tq)[:, None]
    b = _rel_bias(bias_table, dm * dilation)[:n_heads]
    b = jnp.where((jnp.abs(dm) <= radius)[None], b, NEG_BIG)
    col = jnp.arange(win)
    variants = []
    for v in range(4):
        gone = ((col < DILATED_HALO) & bool(v & 1)) | ((col >= tq + DILATED_HALO) & bool(v & 2))
        variants.append(jnp.where(gone[None, None, :], NEG_BIG, b))
    return jnp.stack(variants)


def _dilated_attention(view, bias_table, batch, seq, n_heads, window, dilation, *, tq=128, n_sub=4):
    t = view.shape[0] * dilation
    wa = n_heads * HEAD_DIM
    sub = seq // dilation
    n_sub = min(n_sub, sub // tq)
    step = n_sub * tq
    nq = sub // step
    halo = DILATED_HALO
    per_step = step // halo
    n_halo = sub // halo
    bias = _dilated_bias(bias_table, n_heads, window, dilation, tq)

    def main(which):
        return pl.BlockSpec((step, wa), lambda b, r, i: (b * nq + i, which * dilation + r))

    def edge(which, side):
        def imap(b, r, i):
            blk = i * per_step - 1 if side < 0 else (i + 1) * per_step
            return (b * n_halo + jnp.clip(blk, 0, n_halo - 1), which * dilation + r)
        return pl.BlockSpec((halo, wa), imap)

    return pl.pallas_call(
        functools.partial(_dilated_kernel, tq=tq, n_sub=n_sub, n_heads=n_heads, seq_sub=sub,
                          scale=HEAD_DIM ** -0.5),
        grid=(batch, dilation, nq),
        in_specs=[main(0), edge(1, -1), main(1), edge(1, 1), edge(2, -1), main(2), edge(2, 1),
                  pl.BlockSpec(bias.shape, lambda b, r, i: (0, 0, 0, 0))],
        out_specs=[pl.BlockSpec((step, wa), lambda b, r, i: (b * nq + i, r)),
                   pl.BlockSpec((step, LANES), lambda b, r, i: (b * nq + i, r))],
        out_shape=[jax.ShapeDtypeStruct((t // dilation, dilation * wa), F32),
                   jax.ShapeDtypeStruct((t // dilation, dilation * LANES), F32)],
        compiler_params=_params("arbitrary", "arbitrary", "arbitrary"),
    )(view, view, view, view, view, view, view, bias)


def _mix_kernel(*refs, dils, n_heads, eps):
    n = len(dils)
    o_refs, l_refs = list(refs[:n]), list(refs[n:2 * n])
    g_ref, out_ref = refs[2 * n], refs[2 * n + 1]
    scratch = list(refs[2 * n + 2:])
    tm, wa = out_ref.shape
    def head_reader(ref):
        return lambda h: ref[:, h * HEAD_DIM:(h + 1) * HEAD_DIM]

    heads = [head_reader(o_ref) for o_ref in o_refs]
    for p, dil in enumerate(dils):
        if dil == 1:
            continue
        o_tok, l_tok = scratch.pop(0), scratch.pop(0)
        for r in range(dil):
            rows = pl.ds(r, tm // dil, stride=dil)
            for h in range(n_heads):
                o_tok[h, rows, :] = heads[p](r * n_heads + h)
            l_tok[rows, :] = l_refs[p][:, r * LANES:(r + 1) * LANES]
        heads[p], l_refs[p] = (lambda h, ref=o_tok: ref[h]), l_tok
    ls = [l_ref[...] for l_ref in l_refs]
    top = functools.reduce(jnp.maximum, ls)
    es = [jnp.exp(l - top) for l in ls]
    tot = functools.reduce(lambda a, b: a + b, es)
    ws = [e / tot for e in es]
    for h in range(n_heads):
        cs = slice(h * HEAD_DIM, (h + 1) * HEAD_DIM)
        mix = ws[0][:, h:h + 1] * heads[0](h)
        for p in range(1, n):
            mix = mix + ws[p][:, h:h + 1] * heads[p](h)
        ms = jnp.mean(mix * mix, axis=-1, keepdims=True)
        out_ref[:, cs] = (mix * lax.rsqrt(ms + eps) * g_ref[:, cs]).astype(out_ref.dtype)


def _dilated_mixture(outs, lses, dils, gain, *, tm=512):
    wa = gain.size
    t = outs[0].shape[0] * dils[0]
    n_heads = wa // HEAD_DIM
    row = lambda i: (i, 0)
    scratch = []
    for dil in dils:
        if dil > 1:
            scratch += [pltpu.VMEM((n_heads, tm, HEAD_DIM), F32), pltpu.VMEM((tm, LANES), F32)]
    return pl.pallas_call(
        functools.partial(_mix_kernel, dils=dils, n_heads=n_heads, eps=RMS_EPS),
        grid=(t // tm,),
        in_specs=[pl.BlockSpec((tm // dil, dil * wa), row) for dil in dils]
                 + [pl.BlockSpec((tm // dil, dil * LANES), row) for dil in dils]
                 + [pl.BlockSpec((1, wa), lambda i: (0, 0))],
        out_specs=pl.BlockSpec((tm, wa), row),
        out_shape=jax.ShapeDtypeStruct((t, wa), BF16),
        scratch_shapes=scratch,
        compiler_params=_params("arbitrary"),
    )(*outs, *lses, gain.reshape(1, wa))


def _diff_attn_kernel(lam_ref, q_ref, k_ref, v_ref, bias_ref, g_ref, o_ref, s_scr, p_scr, v1_scr,
                      *, tile, q_tiles, n_tiles, eps, post_scale):
    step = pl.program_id(2)
    lane = lax.broadcasted_iota(jnp.int32, (q_tiles * tile, HEAD_DIM), 1)
    qs = q_ref[...] * (DIFF_DIM ** -0.5)
    zero = jnp.zeros_like(qs)
    q_maps = (jnp.where(lane < DIFF_DIM, qs, zero), jnp.where(lane >= DIFF_DIM, qs, zero))
    chunks = tile // LANES
    streams = [(qt, mp) for qt in range(q_tiles) for mp in range(2)]

    def fold(op, acc, vals):
        for c in range(chunks):
            acc = op(acc, vals[:, c * LANES:(c + 1) * LANES])
        return acc

    def scores(n, kt, run_max):
        qt, mp = streams[n]
        ks = slice(kt * tile, (kt + 1) * tile)
        rel_class = jnp.clip(kt - (step * q_tiles + qt), -2, 2) + 2
        q = q_maps[mp][qt * tile:(qt + 1) * tile]
        s = lax.dot_general(q, k_ref[ks, :], _NT, preferred_element_type=F32) + bias_ref[0, rel_class]
        s_scr[n % 2, :, ks] = s
        return fold(jnp.maximum, run_max, s)

    def weights(n, kt, m):
        ks = slice(kt * tile, (kt + 1) * tile)
        p_scr[n % 2, :, ks] = jnp.exp(s_scr[n % 2, :, ks] - m).astype(BF16)

    def attend(n):
        acc = jnp.dot(p_scr[n % 2], v1_scr[...], preferred_element_type=F32)
        return acc[:, :HEAD_DIM] / acc[:, HEAD_DIM:]

    @pl.when(step == 0)
    def _():
        v1_scr[:, :HEAD_DIM] = v_ref[...]
        v1_scr[:, HEAD_DIM:] = jnp.ones((v1_scr.shape[0], HEAD_DIM), BF16)

    neg_inf = jnp.full((tile, LANES), -jnp.inf, F32)
    run_max = neg_inf
    for kt in range(n_tiles):
        run_max = scores(0, kt, run_max)
    outs = []
    for n in range(len(streams)):
        m = jnp.max(run_max, axis=-1, keepdims=True)
        run_max = neg_inf
        for kt in range(n_tiles):
            weights(n, kt, m)
            if n + 1 < len(streams):
                run_max = scores(n + 1, kt, run_max)
        outs.append(attend(n))
    for qt in range(q_tiles):
        o = outs[2 * qt] - lam_ref[0] * outs[2 * qt + 1]
        ms = jnp.mean(o * o, axis=-1, keepdims=True)
        o_ref[qt * tile:(qt + 1) * tile, :] = ((o * lax.rsqrt(ms + eps) * g_ref[0]) * post_scale).astype(o_ref.dtype)


def _diff_attention(proj, bias_table, gain, lam, lam_init, batch, seq, n_heads_a, n_heads_b, *, tile=256, q_tiles=4):
    t, pw = proj.shape
    n_tiles = seq // tile
    q0 = 3 * n_heads_a
    k0 = q0 + n_heads_b
    v0 = k0 + n_heads_b
    assert tile + 1 >= REL_MAX_DISTANCE
    rel = (jnp.arange(5)[:, None, None] - 2) * tile + jnp.arange(tile)[None, None, :] - jnp.arange(tile)[None, :, None]
    bias = _rel_bias(bias_table, rel)[n_heads_a:]
    n_steps = n_tiles // q_tiles
    rows = q_tiles * tile
    return pl.pallas_call(
        functools.partial(_diff_attn_kernel, tile=tile, q_tiles=q_tiles, n_tiles=n_tiles, eps=SUBLN_EPS,
                          post_scale=1.0 - lam_init),
        grid=(batch, n_heads_b, n_steps),
        in_specs=[pl.BlockSpec(memory_space=pltpu.SMEM),
                  pl.BlockSpec((rows, HEAD_DIM), lambda b, h, i: (b * n_steps + i, q0 + h)),
                  pl.BlockSpec((seq, HEAD_DIM), lambda b, h, i: (b, k0 + h)),
                  pl.BlockSpec((seq, HEAD_DIM), lambda b, h, i: (b, v0 + h)),
                  pl.BlockSpec((1, 5, tile, tile), lambda b, h, i: (h, 0, 0, 0)),
                  pl.BlockSpec((1, 1, HEAD_DIM), lambda b, h, i: (h, 0, 0))],
        out_specs=pl.BlockSpec((rows, HEAD_DIM), lambda b, h, i: (b * n_steps + i, h)),
        out_shape=jax.ShapeDtypeStruct((t, n_heads_b * HEAD_DIM), BF16),
        scratch_shapes=[pltpu.VMEM((2, tile, seq), F32), pltpu.VMEM((2, tile, seq), BF16),
                        pltpu.VMEM((seq, 2 * HEAD_DIM), BF16)],
        compiler_params=_params("arbitrary", "arbitrary", "arbitrary"),
    )(lam.reshape(1), proj, proj, proj, bias, gain.reshape(n_heads_b, 1, HEAD_DIM))


def _out_router_kernel(a_ref, b_ref, x_ref, wo_ref, g_ref, wrh_ref, wrl_ref,
                       x1_ref, hn_ref, route_ref, cnt_ref, *, tm, sub, wa, eps):
    @pl.when(pl.program_id(0) == 0)
    def _():
        cnt_ref[...] = jnp.zeros(cnt_ref.shape, F32)

    lane = lax.broadcasted_iota(jnp.int32, (sub, LANES), 1)
    neg_inf = jnp.float32(-jnp.inf)

    def project(rows):
        y = jnp.dot(a_ref[rows, :], wo_ref[:wa, :], preferred_element_type=F32)
        y = y + jnp.dot(b_ref[rows, :], wo_ref[wa:, :], preferred_element_type=F32)
        x1 = x_ref[rows, :] + y
        x1_ref[rows, :] = x1
        ms = jnp.mean(x1 * x1, axis=-1, keepdims=True)
        hn = x1 * lax.rsqrt(ms + eps) * g_ref[...]
        hn_ref[rows, :] = hn
        hi = hn.astype(BF16)
        lo = (hn - hi.astype(F32)).astype(BF16)
        return (jnp.dot(hi, wrh_ref[...], preferred_element_type=F32)
                + jnp.dot(lo, wrh_ref[...], preferred_element_type=F32)
                + jnp.dot(hi, wrl_ref[...], preferred_element_type=F32))

    def first_max(vals):
        top = jnp.max(vals, axis=-1, keepdims=True)
        idx = jnp.min(jnp.where(vals == top, lane, LANES), axis=-1, keepdims=True)
        return top, idx

    def route(rows, logits):
        is_grp = lane < N_GROUPS
        g_top, g_idx = first_max(jnp.where(is_grp, logits, neg_inf))
        g_sum = jnp.sum(jnp.where(is_grp, jnp.exp(logits - g_top), 0.0), axis=-1, keepdims=True)
        p_grp = 1.0 / g_sum
        e_lo = N_GROUPS + EXPERTS_PER_GROUP * g_idx
        e_vals = jnp.where((lane >= e_lo) & (lane < e_lo + EXPERTS_PER_GROUP), logits, neg_inf)
        top1, i1 = first_max(e_vals)
        top2, i2 = first_max(jnp.where(lane == i1, neg_inf, e_vals))
        b2 = jnp.exp(top2 - top1)
        w0 = p_grp / (1.0 + b2)
        w1 = p_grp * b2 / (1.0 + b2)

        oh0 = (lane == i1).astype(F32)
        oh1 = (lane == i2).astype(F32)
        both = oh0 + oh1
        r_i = lax.broadcasted_iota(jnp.int32, (sub, sub), 0)
        c_i = lax.broadcasted_iota(jnp.int32, (sub, sub), 1)
        before = jnp.where(r_i > c_i, 1.0, 0.0).astype(BF16)
        base = cnt_ref[...] + jnp.dot(before, both.astype(BF16), preferred_element_type=F32)
        rank0 = jnp.sum(base * oh0, axis=-1, keepdims=True)
        rank1 = jnp.sum(base * oh1, axis=-1, keepdims=True)
        cnt_ref[...] = cnt_ref[...] + jnp.sum(both, axis=0, keepdims=True)

        fields = {_R_E0: (i1 - N_GROUPS).astype(F32), _R_E1: (i2 - N_GROUPS).astype(F32),
                  _R_W0: w0, _R_W1: w1, _R_RANK0: rank0, _R_RANK1: rank1}
        slab = jnp.zeros((sub, LANES), F32)
        for n, f in fields.items():
            slab = jnp.where(lane == n, f, slab)
        route_ref[rows, :] = slab

    row_sets = [slice(k * sub, (k + 1) * sub) for k in range(tm // sub)]
    logits = [project(rows) for rows in row_sets]
    for rows, lg in zip(row_sets, logits):
        route(rows, lg)


def _out_router(out_a, out_b, x, w_out, layer, g, wr_hi, wr_lo, *, tm=512, sub=256):
    t, d = x.shape
    wa = out_a.shape[1]
    wb = out_b.shape[1]
    row = lambda i: (i, 0)
    const = lambda i: (0, 0)
    return pl.pallas_call(
        functools.partial(_out_router_kernel, tm=tm, sub=sub, wa=wa, eps=RMS_EPS),
        grid=(t // tm,),
        in_specs=[pl.BlockSpec((tm, wa), row), pl.BlockSpec((tm, wb), row), pl.BlockSpec((tm, d), row),
                  pl.BlockSpec((None, wa + wb, d), lambda i: (layer, 0, 0)), pl.BlockSpec((1, d), const),
                  pl.BlockSpec((d, LANES), const), pl.BlockSpec((d, LANES), const)],
        out_specs=[pl.BlockSpec((tm, d), row), pl.BlockSpec((tm, d), row),
                   pl.BlockSpec((tm, LANES), row), pl.BlockSpec((1, LANES), const)],
        out_shape=[jax.ShapeDtypeStruct((t, d), F32), jax.ShapeDtypeStruct((t, d), F32),
                   jax.ShapeDtypeStruct((t, LANES), F32), jax.ShapeDtypeStruct((1, LANES), F32)],
        compiler_params=_params("arbitrary"),
    )(out_a, out_b, x, w_out, g.reshape(1, d), wr_hi, wr_lo)


def _moe_kernel(nitems_ref, iexp_ref, icnt_ref, ilive_ref, tok_ref, nxt_ref, dst_ref, hn_hbm, wg_ref, wu_ref, wd_ref, out_hbm,
                xbuf, ybuf, gsem, ssem, *, rows, dump_rows):
    del iexp_ref
    it = pl.program_id(0)
    first_half = pl.program_id(1) == 0
    n_items = nitems_ref[0]
    cnt = icnt_ref[it]
    live = ilive_ref[it]
    buf = lax.rem(it, 2)
    tiles = rows // SUBLANES
    d = ybuf.shape[-1]

    def issue_rows(first_tile, n_tiles, copy_row):
        def body(g, carry):
            for u in range(SUBLANES):
                copy_row(first_tile + g, u).start(priority=ROW_COPY_PRIORITY)
            return carry
        lax.fori_loop(0, n_tiles, body, 0)

    def hbm_row(ref, row):
        return ref.at[lax.shift_right_logical(row, SUBLANES.bit_length() - 1), pl.ds(row & (SUBLANES - 1), 1)]

    def start_gather(idx_ref, s, live_tiles):
        issue_rows(0, live_tiles, lambda g, u: pltpu.make_async_copy(
            hbm_row(hn_hbm, idx_ref[0, 0, g * SUBLANES + u]), xbuf.at[s, g, pl.ds(u, 1)], gsem.at[s]))

    def start_scatter(j, live_tiles):
        issue_rows(j * tiles, jnp.clip(live_tiles - j * tiles, 0, tiles), lambda g, u: pltpu.make_async_copy(
            ybuf.at[g, pl.ds(u, 1)], hbm_row(out_hbm, dst_ref[0, 0, g * SUBLANES + u]), ssem.at[0]))

    def wait_tiles(n_tiles, make_tile_copy):
        def body(j, carry):
            make_tile_copy().wait()
            return carry
        lax.fori_loop(0, n_tiles, body, 0)

    def wait_gather(s, live_tiles):
        wait_tiles(live_tiles, lambda: pltpu.make_async_copy(
            hn_hbm.at[pl.ds(0, 1)], xbuf.at[s, pl.ds(0, 1)], gsem.at[s]))

    def wait_scatter(live_tiles):
        wait_tiles(live_tiles, lambda: pltpu.make_async_copy(
            ybuf.at[pl.ds(0, 1)], out_hbm.at[pl.ds(0, 1)], ssem.at[0]))

    def block_tiles(j):
        return pl.ds(pl.multiple_of(j * tiles, tiles), tiles)

    def ffn_half(j):
        x = xbuf[buf, block_tiles(j)].reshape(rows, d)
        gate = jnp.dot(x, wg_ref[0], preferred_element_type=F32)
        up = jnp.dot(x, wu_ref[0], preferred_element_type=F32)
        hid = gate * (1.0 / (1.0 + jnp.exp(-gate))) * up
        return jnp.dot(hid, wd_ref[0], preferred_element_type=F32).reshape(tiles, SUBLANES, d)

    @pl.when(jnp.logical_and(it < n_items, first_half))
    def _():
        @pl.when(it == 0)
        def _():
            ybuf[pl.ds(0, tiles)] = jnp.zeros((tiles, SUBLANES, d), F32)
            for start in dump_rows:
                dump = pltpu.make_async_copy(ybuf.at[pl.ds(0, tiles)],
                                             out_hbm.at[pl.ds(start // SUBLANES, tiles)], ssem.at[0])
                dump.start()
                dump.wait()

            def clear(g, carry):
                for s in range(2):
                    xbuf[s, g] = jnp.zeros((SUBLANES, d), F32)
                return carry
            lax.fori_loop(0, xbuf.shape[1], clear, 0)
            start_gather(tok_ref, buf, live)

        wait_gather(buf, live)

        def body(j, carry):
            y = ffn_half(j)

            @pl.when(jnp.logical_and(j == 0, it >= 1))
            def _():
                wait_scatter(ilive_ref[jnp.maximum(it - 1, 0)])

            ybuf[block_tiles(j)] = y
            return carry
        lax.fori_loop(0, cnt, body, 0)

    @pl.when(jnp.logical_and(it < n_items, jnp.logical_not(first_half)))
    def _():
        @pl.when(it + 1 < n_items)
        def _():
            start_gather(nxt_ref, 1 - buf, ilive_ref[jnp.minimum(it + 1, ilive_ref.shape[0] - 1)])

        def body(j, carry):
            ybuf[block_tiles(j)] = ybuf[block_tiles(j)] + ffn_half(j)
            start_scatter(j, live)
            return carry
        lax.fori_loop(0, cnt, body, 0)

        @pl.when(it + 1 >= n_items)
        def _():
            wait_scatter(live)


def _plane_rows(n_tok):
    return n_tok + 2 * MOE_BLOCK


def _moe(hn, item_tok, item_dst, item_exp, item_cnt, item_live, n_items, w_gate, w_up, w_down, layer):
    t, d = hn.shape
    max_items = item_tok.shape[0]
    rows = MOE_BLOCK
    item_rows = MOE_GROUP * rows
    ffn = w_gate.shape[3]
    plane = _plane_rows(t)
    dump_rows = tuple(p * plane + t + s * rows for p in range(2) for s in range(2))
    half_ffn = ffn // 2

    def half(it, h, ni):
        return jnp.where(it < ni[0], h, 1)

    grid_spec = pltpu.PrefetchScalarGridSpec(
        num_scalar_prefetch=4,
        grid=(max_items, 2),
        in_specs=[
            pl.BlockSpec((1, 1, item_rows), lambda it, h, ni, ie, ic, il: (it, 0, 0), memory_space=pltpu.SMEM),
            pl.BlockSpec((1, 1, item_rows), lambda it, h, ni, ie, ic, il: (jnp.minimum(it + 1, max_items - 1), 0, 0),
                         memory_space=pltpu.SMEM),
            pl.BlockSpec((1, 1, item_rows), lambda it, h, ni, ie, ic, il: (it, 0, 0), memory_space=pltpu.SMEM),
            pl.BlockSpec(memory_space=pl.ANY),
            pl.BlockSpec((None, 1, d, half_ffn), lambda it, h, ni, ie, ic, il: (layer, ie[it], 0, half(it, h, ni))),
            pl.BlockSpec((None, 1, d, half_ffn), lambda it, h, ni, ie, ic, il: (layer, ie[it], 0, half(it, h, ni))),
            pl.BlockSpec((None, 1, half_ffn, d), lambda it, h, ni, ie, ic, il: (layer, ie[it], half(it, h, ni), 0)),
        ],
        out_specs=pl.BlockSpec(memory_space=pl.ANY),
        scratch_shapes=[pltpu.VMEM((2, item_rows // SUBLANES, SUBLANES, d), F32),
                        pltpu.VMEM((item_rows // SUBLANES, SUBLANES, d), F32),
                        pltpu.SemaphoreType.DMA((2,)), pltpu.SemaphoreType.DMA((1,))],
    )
    out = pl.pallas_call(
        functools.partial(_moe_kernel, rows=rows, dump_rows=dump_rows),
        grid_spec=grid_spec,
        out_shape=jax.ShapeDtypeStruct((2 * plane // SUBLANES, SUBLANES, d), F32),
        compiler_params=_params("arbitrary", "arbitrary", disable_bounds_checks=True),
    )(n_items, item_exp, item_cnt, item_live, item_tok, item_tok, item_dst, hn.reshape(t // SUBLANES, SUBLANES, d),
      w_gate, w_up, w_down)
    return out.reshape(2 * plane, d)


def _dispatch(route, counts, n_tok):
    rows = MOE_BLOCK
    group = MOE_GROUP
    plane = _plane_rows(n_tok)
    n_blocks = -(-(2 * n_tok) // rows) + N_EXPERTS
    max_items = N_EXPERTS + n_blocks // group
    e = route[:, _R_E0:_R_E1 + 1].astype(jnp.int32)
    rank = route[:, _R_RANK0:_R_RANK1 + 1].astype(jnp.int32)
    cnt = counts[0, N_GROUPS:N_GROUPS + N_EXPERTS].astype(jnp.int32)
    padded = (cnt + rows - 1) // rows * rows
    pend = jnp.cumsum(padded)
    pstart = pend - padded
    dest = (pstart[e] + rank).reshape(-1)
    code = (2 * jnp.arange(n_tok, dtype=jnp.int32)[:, None] + jnp.arange(2, dtype=jnp.int32)[None, :]).reshape(-1)
    row_src = jnp.full((n_blocks * rows,), -1, jnp.int32).at[dest].set(code)
    row = jnp.arange(n_blocks * rows, dtype=jnp.int32)
    row_tok = jnp.maximum(row_src, 0) // 2
    row_dst = jnp.where(row_src >= 0, (row_src % 2) * plane + row_src // 2, n_tok + row % rows)
    n_blk = padded // rows
    first_blk = pstart // rows
    items_end = jnp.cumsum((n_blk + group - 1) // group)
    items_start = items_end - (n_blk + group - 1) // group
    n_items = items_end[-1].astype(jnp.int32)
    it = jnp.arange(max_items, dtype=jnp.int32)
    item_exp = jnp.minimum(jnp.sum(items_end[None, :] <= it[:, None], axis=1), N_EXPERTS - 1).astype(jnp.int32)
    run = it - items_start[item_exp]
    item_blk = first_blk[item_exp] + run * group
    item_cnt = jnp.clip(n_blk[item_exp] - run * group, 0, group)
    live = it < n_items
    item_cnt = jnp.where(live, item_cnt, 0).astype(jnp.int32)
    item_exp = jnp.where(live, item_exp, item_exp[jnp.maximum(n_items - 1, 0)])
    blk_ids = jnp.clip(item_blk[:, None] + jnp.arange(group, dtype=jnp.int32)[None, :], 0, n_blocks - 1)
    item_tok = row_tok.reshape(n_blocks, rows)[blk_ids].reshape(max_items, 1, group * rows)
    item_dst = row_dst.reshape(n_blocks, rows)[blk_ids].reshape(max_items, 1, group * rows)
    real_rows = jnp.clip(cnt[item_exp] - run * group * rows, 0, item_cnt * rows)
    item_live = jnp.where(live, (real_rows + SUBLANES - 1) // SUBLANES, 0).astype(jnp.int32)
    return item_tok, item_dst, item_exp, item_cnt, item_live, n_items.reshape(1)


def _final_kernel(x_ref, route_ref, ya_ref, yb_ref, g_ref, o_ref, *, eps):
    x = _add_expert_outputs(x_ref[...], route_ref, ya_ref, yb_ref)
    ms = jnp.mean(x * x, axis=-1, keepdims=True)
    o_ref[...] = x * lax.rsqrt(ms + eps) * g_ref[...]


def _final_norm(x, moe, g, *, tm=512):
    t, d = x.shape
    route, planes = moe
    row = lambda i: (i, 0)
    return pl.pallas_call(
        functools.partial(_final_kernel, eps=RMS_EPS),
        grid=(t // tm,),
        in_specs=[pl.BlockSpec((tm, d), row), pl.BlockSpec((tm, LANES), row)]
                 + _expert_output_specs(tm, d, t)
                 + [pl.BlockSpec((1, d), lambda i: (0, 0))],
        out_specs=pl.BlockSpec((tm, d), row),
        out_shape=jax.ShapeDtypeStruct((t, d), F32),
        compiler_params=_params("arbitrary"),
    )(x, route, planes, planes, g.reshape(1, d))


def kernel(x, rel_bias, norm_attn, w_in, gain_a, gain_b, lam_q1, lam_k1, lam_q2, lam_k2, w_out, norm_ffn,
           w_router_group, w_router_expert, w_gate, w_up, w_down, norm_final):
    batch, seq, d = x.shape
    depth = w_in.shape[0]
    n_heads_a = gain_a.shape[1]
    n_heads_b = gain_b.shape[1]
    t = batch * seq
    xs = x.reshape(t, d)
    w_in, w_out = w_in.astype(BF16), w_out.astype(BF16)
    dils = tuple(dil for _, dil in DILATED_PATTERNS)
    extra_dils = tuple(dil for dil in dils if dil > 1)
    moe = None
    for l in range(depth):
        proj, views, xs = _norm_proj(xs, moe, norm_attn[l], w_in, l, extra_dils, 3)
        views = dict(zip(extra_dils, views))
        parts = [_dilated_attention(views.get(dil, proj), rel_bias, batch, seq, n_heads_a, w, dil)
                 for (w, dil) in DILATED_PATTERNS]
        out_a = _dilated_mixture([p[0] for p in parts], [p[1] for p in parts], dils, gain_a[l])

        lam_init = 0.8 - 0.6 * math.exp(-0.3 * l)
        lam = (jnp.exp(jnp.sum(lam_q1[l].astype(F32) * lam_k1[l].astype(F32)))
               - jnp.exp(jnp.sum(lam_q2[l].astype(F32) * lam_k2[l].astype(F32))) + lam_init)
        out_b = _diff_attention(proj, rel_bias, gain_b[l], lam, lam_init, batch, seq, n_heads_a, n_heads_b)

        w_r = jnp.concatenate([w_router_group[l], w_router_expert[l].transpose(1, 0, 2).reshape(d, N_EXPERTS)], axis=1)
        w_r = jnp.pad(w_r.astype(F32), ((0, 0), (0, LANES - w_r.shape[1])))
        wr_hi = w_r.astype(BF16)
        wr_lo = (w_r - wr_hi.astype(F32)).astype(BF16)
        xs, hn, route, counts = _out_router(out_a, out_b, xs, w_out, l, norm_ffn[l], wr_hi, wr_lo)

        planes = _moe(hn, *_dispatch(route, counts, t), w_gate, w_up, w_down, l)
        moe = (route, planes)
    out = _final_norm(xs, moe, norm_final)
    return out.reshape(batch, seq, d)
```

```python
import functools
import math

import jax
import jax.numpy as jnp
from jax import lax
from jax.experimental import pallas as pl
from jax.experimental.pallas import tpu as pltpu

F32 = jnp.float32
BF16 = jnp.bfloat16

LANES = 128
HEAD_DIM = 128
DIFF_DIM = HEAD_DIM // 2
DILATED_PATTERNS = ((128, 1), (512, 4), (2048, 16))
DILATED_HALO = 64
N_REL_BUCKETS = 32
REL_MAX_DISTANCE = 128
N_GROUPS = 4
EXPERTS_PER_GROUP = 8
N_EXPERTS = N_GROUPS * EXPERTS_PER_GROUP
MOE_BLOCK = 256
MOE_GROUP = 3
SUBLANES = 8
ROW_COPY_PRIORITY = 1
RMS_EPS = 1e-6
SUBLN_EPS = 1e-5
NEG_BIG = -1e30
VMEM_LIMIT = 56 * 1024 * 1024

_NT = (((1,), (1,)), ((), ()))

_R_E0, _R_E1, _R_W0, _R_W1, _R_RANK0, _R_RANK1 = range(6)


def _t5_bucket(rel):
    nb = N_REL_BUCKETS // 2
    max_exact = nb // 2
    n = -rel
    ret = jnp.where(n < 0, nb, 0)
    n = jnp.abs(n)
    nf = jnp.maximum(n, 1).astype(jnp.float32)
    large = max_exact + (jnp.log(nf / max_exact) / math.log(REL_MAX_DISTANCE / max_exact)
                         * (nb - max_exact)).astype(jnp.int32)
    large = jnp.minimum(large, nb - 1)
    return (ret + jnp.where(n < max_exact, n, large)).astype(jnp.int32)


def _rel_bias(bias_table, rel):
    onehot = (_t5_bucket(rel).reshape(-1)[None, :] == jnp.arange(N_REL_BUCKETS)[:, None]).astype(F32)
    vals = jnp.dot(bias_table.astype(F32).T, onehot, precision=lax.Precision.HIGHEST)
    return vals.reshape((bias_table.shape[1],) + rel.shape)


def _params(*sem, **kwargs):
    return pltpu.CompilerParams(dimension_semantics=sem, vmem_limit_bytes=VMEM_LIMIT, **kwargs)


def _add_expert_outputs(x, route_ref, ya_ref, yb_ref):
    return x + route_ref[:, _R_W0:_R_W0 + 1] * ya_ref[...] + route_ref[:, _R_W1:_R_W1 + 1] * yb_ref[...]


def _expert_output_specs(tm, d, n_tok):
    blocks_per_plane, rem = divmod(n_tok + 2 * MOE_BLOCK, tm)
    assert rem == 0
    return [pl.BlockSpec((tm, d), lambda i, *_, p=p: (p * blocks_per_plane + i, 0)) for p in range(2)]


def _norm_proj_kernel(*refs, has_y, dils, n_split, eps):
    n_in = 6 if has_y else 3
    if has_y:
        x_ref, route_ref, ya_ref, yb_ref, g_ref, w_ref = refs[:n_in]
    else:
        x_ref, g_ref, w_ref = refs[:n_in]
    o_ref = refs[n_in]
    od_refs = refs[n_in + 1:n_in + 1 + len(dils)]
    rest = refs[n_in + 1 + len(dils):]
    xs_ref = rest[0] if has_y else None
    xn_ref, res_ref = rest[-2:]
    j = pl.program_id(1)

    @pl.when(j == 0)
    def _():
        x = x_ref[...]
        if has_y:
            x = _add_expert_outputs(x, route_ref, ya_ref, yb_ref)
            xs_ref[...] = x
        ms = jnp.mean(x * x, axis=-1, keepdims=True)
        xn_ref[...] = (x * lax.rsqrt(ms + eps) * g_ref[...]).astype(BF16)

    res = jnp.dot(xn_ref[...], w_ref[...], preferred_element_type=F32)
    o_ref[...] = res.astype(o_ref.dtype)

    @pl.when(j < n_split)
    def _():
        n_chunks, tm, _ = res_ref.shape
        tn = n_chunks * LANES
        for c in range(n_chunks):
            res_ref[c] = res[:, c * LANES:(c + 1) * LANES]
        for od_ref, dil in zip(od_refs, dils):
            for r in range(dil):
                for c in range(n_chunks):
                    col = r * tn + c * LANES
                    od_ref[:, col:col + LANES] = res_ref[c, pl.ds(r, tm // dil, stride=dil), :].astype(od_ref.dtype)


def _norm_proj(x, moe, g, w, layer, dils, n_split, *, tm=512, tn=1024):
    t, d = x.shape
    n = w.shape[2]
    has_y = moe is not None
    row = lambda i, j: (i, 0)
    in_specs = [pl.BlockSpec((tm, d), row)]
    args = [x]
    if has_y:
        route, planes = moe
        in_specs += [pl.BlockSpec((tm, LANES), row)]
        in_specs += _expert_output_specs(tm, d, t)
        args += [route, planes, planes]
    in_specs += [pl.BlockSpec((1, d), lambda i, j: (0, 0)), pl.BlockSpec((None, d, tn), lambda i, j: (layer, 0, j))]
    args += [g.reshape(1, d), w]
    out_shape = [jax.ShapeDtypeStruct((t, n), BF16)]
    out_specs = [pl.BlockSpec((tm, tn), lambda i, j: (i, j))]
    for dil in dils:
        out_shape.append(jax.ShapeDtypeStruct((t // dil, n_split * dil * tn), BF16))
        out_specs.append(pl.BlockSpec((tm // dil, dil * tn), lambda i, j: (i, jnp.minimum(j, n_split - 1))))
    if has_y:
        out_shape.append(jax.ShapeDtypeStruct((t, d), F32))
        out_specs.append(pl.BlockSpec((tm, d), row))
    res = pl.pallas_call(
        functools.partial(_norm_proj_kernel, has_y=has_y, dils=dils, n_split=n_split, eps=RMS_EPS),
        grid=(t // tm, n // tn),
        in_specs=in_specs, out_specs=out_specs, out_shape=out_shape,
        scratch_shapes=[pltpu.VMEM((tm, d), BF16), pltpu.VMEM((tn // LANES, tm, LANES), F32)],
        compiler_params=_params("arbitrary", "arbitrary"),
    )(*args)
    views = list(res[1:1 + len(dils)])
    return res[0], views, (res[-1] if has_y else x)


def _dilated_kernel(q_ref, kp_ref, kc_ref, kn_ref, vp_ref, vc_ref, vn_ref, bias_ref, o_ref, lse_ref,
                    *, tq, n_sub, n_heads, seq_sub, scale):
    i = pl.program_id(2)
    halo = DILATED_HALO
    win = tq + 2 * halo
    n_tiles = seq_sub // tq
    lane = lax.broadcasted_iota(jnp.int32, (tq, LANES), 1)
    variant = [((i * n_sub + t) == 0).astype(jnp.int32) + 2 * ((i * n_sub + t) == n_tiles - 1).astype(jnp.int32)
               for t in range(n_sub)]
    lse_all = [jnp.zeros((tq, LANES), F32) for _ in range(n_sub)]
    for h in range(n_heads):
        cs = slice(h * HEAD_DIM, (h + 1) * HEAD_DIM)
        k = jnp.concatenate([kp_ref[:, cs], kc_ref[:, cs], kn_ref[:, cs]], axis=0)
        v = jnp.concatenate([vp_ref[:, cs], vc_ref[:, cs], vn_ref[:, cs]], axis=0)
        for t in range(n_sub):
            rows = slice(t * tq, (t + 1) * tq)
            keys = slice(t * tq, t * tq + win)
            s = (lax.dot_general(q_ref[rows, cs], k[keys], _NT, preferred_element_type=F32) * scale
                 + bias_ref[variant[t], h])
            m = jnp.max(s, axis=-1, keepdims=True)
            p = jnp.exp(s - m)
            den = jnp.sum(p, axis=-1, keepdims=True)
            o_ref[rows, cs] = jnp.dot(p.astype(BF16), v[keys], preferred_element_type=F32) / den
            lse_all[t] = jnp.where(lane == h, m + jnp.log(den), lse_all[t])
    for t in range(n_sub):
        lse_ref[t * tq:(t + 1) * tq, :] = lse_all[t]


def _dilated_bias(bias_table, n_heads, window, dilation, tq):
    radius = window // (2 * dilation)
    assert radius <= DILATED_HALO
    win = tq + 2 * DILATED_HALO
    dm = jnp.arange(win)[None, :] - DILATED_HALO - jnp.arange(tq)[:, None]
    b = _rel_bias(bias_table, dm * dilation)[:n_heads]
    b = jnp.where((jnp.abs(dm) <= radius)[None], b, NEG_BIG)
    col = jnp.arange(win)
    variants = []
    for v in range(4):
        gone = ((col < DILATED_HALO) & bool(v & 1)) | ((col >= tq + DILATED_HALO) & bool(v & 2))
        variants.append(jnp.where(gone[None, None, :], NEG_BIG, b))
    return jnp.stack(variants)


def _dilated_attention(view, bias_table, batch, seq, n_heads, window, dilation, *, tq=128, n_sub=4):
    t = view.shape[0] * dilation
    wa = n_heads * HEAD_DIM
    sub = seq // dilation
    n_sub = min(n_sub, sub // tq)
    step = n_sub * tq
    nq = sub // step
    halo = DILATED_HALO
    per_step = step // halo
    n_halo = sub // halo
    bias = _dilated_bias(bias_table, n_heads, window, dilation, tq)

    def main(which):
        return pl.BlockSpec((step, wa), lambda b, r, i: (b * nq + i, which * dilation + r))

    def edge(which, side):
        def imap(b, r, i):
            blk = i * per_step - 1 if side < 0 else (i + 1) * per_step
            return (b * n_halo + jnp.clip(blk, 0, n_halo - 1), which * dilation + r)
        return pl.BlockSpec((halo, wa), imap)

    return pl.pallas_call(
        functools.partial(_dilated_kernel, tq=tq, n_sub=n_sub, n_heads=n_heads, seq_sub=sub,
                          scale=HEAD_DIM ** -0.5),
        grid=(batch, dilation, nq),
        in_specs=[main(0), edge(1, -1), main(1), edge(1, 1), edge(2, -1), main(2), edge(2, 1),
                  pl.BlockSpec(bias.shape, lambda b, r, i: (0, 0, 0, 0))],
        out_specs=[pl.BlockSpec((step, wa), lambda b, r, i: (b * nq + i, r)),
                   pl.BlockSpec((step, LANES), lambda b, r, i: (b * nq + i, r))],
        out_shape=[jax.ShapeDtypeStruct((t // dilation, dilation * wa), F32),
                   jax.ShapeDtypeStruct((t // dilation, dilation * LANES), F32)],
        compiler_params=_params("arbitrary", "arbitrary", "arbitrary"),
    )(view, view, view, view, view, view, view, bias)


def _mix_kernel(*refs, dils, n_heads, eps):
    n = len(dils)
    o_refs, l_refs = list(refs[:n]), list(refs[n:2 * n])
    g_ref, out_ref = refs[2 * n], refs[2 * n + 1]
    scratch = list(refs[2 * n + 2:])
    tm, wa = out_ref.shape
    def head_reader(ref):
        return lambda h: ref[:, h * HEAD_DIM:(h + 1) * HEAD_DIM]

    heads = [head_reader(o_ref) for o_ref in o_refs]
    for p, dil in enumerate(dils):
        if dil == 1:
            continue
        o_tok, l_tok = scratch.pop(0), scratch.pop(0)
        for r in range(dil):
            rows = pl.ds(r, tm // dil, stride=dil)
            for h in range(n_heads):
                o_tok[h, rows, :] = heads[p](r * n_heads + h)
            l_tok[rows, :] = l_refs[p][:, r * LANES:(r + 1) * LANES]
        heads[p], l_refs[p] = (lambda h, ref=o_tok: ref[h]), l_tok
    ls = [l_ref[...] for l_ref in l_refs]
    top = functools.reduce(jnp.maximum, ls)
    es = [jnp.exp(l - top) for l in ls]
    tot = functools.reduce(lambda a, b: a + b, es)
    ws = [e / tot for e in es]
    for h in range(n_heads):
        cs = slice(h * HEAD_DIM, (h + 1) * HEAD_DIM)
        mix = ws[0][:, h:h + 1] * heads[0](h)
        for p in range(1, n):
            mix = mix + ws[p][:, h:h + 1] * heads[p](h)
        ms = jnp.mean(mix * mix, axis=-1, keepdims=True)
        out_ref[:, cs] = (mix * lax.rsqrt(ms + eps) * g_ref[:, cs]).astype(out_ref.dtype)


def _dilated_mixture(outs, lses, dils, gain, *, tm=512):
    wa = gain.size
    t = outs[0].shape[0] * dils[0]
    n_heads = wa // HEAD_DIM
    row = lambda i: (i, 0)
    scratch = []
    for dil in dils:
        if dil > 1:
            scratch += [pltpu.VMEM((n_heads, tm, HEAD_DIM), F32), pltpu.VMEM((tm, LANES), F32)]
    return pl.pallas_call(
        functools.partial(_mix_kernel, dils=dils, n_heads=n_heads, eps=RMS_EPS),
        grid=(t // tm,),
        in_specs=[pl.BlockSpec((tm // dil, dil * wa), row) for dil in dils]
                 + [pl.BlockSpec((tm // dil, dil * LANES), row) for dil in dils]
                 + [pl.BlockSpec((1, wa), lambda i: (0, 0))],
        out_specs=pl.BlockSpec((tm, wa), row),
        out_shape=jax.ShapeDtypeStruct((t, wa), BF16),
        scratch_shapes=scratch,
        compiler_params=_params("arbitrary"),
    )(*outs, *lses, gain.reshape(1, wa))


def _diff_attn_kernel(lam_ref, q_ref, k_ref, v_ref, bias_ref, g_ref, o_ref, s_scr, p_scr, v1_scr,
                      *, tile, q_tiles, n_tiles, eps, post_scale):
    step = pl.program_id(2)
    lane = lax.broadcasted_iota(jnp.int32, (q_tiles * tile, HEAD_DIM), 1)
    qs = q_ref[...] * (DIFF_DIM ** -0.5)
    zero = jnp.zeros_like(qs)
    q_maps = (jnp.where(lane < DIFF_DIM, qs, zero), jnp.where(lane >= DIFF_DIM, qs, zero))
    chunks = tile // LANES
    streams = [(qt, mp) for qt in range(q_tiles) for mp in range(2)]

    def fold(op, acc, vals):
        for c in range(chunks):
            acc = op(acc, vals[:, c * LANES:(c + 1) * LANES])
        return acc

    def scores(n, kt, run_max):
        qt, mp = streams[n]
        ks = slice(kt * tile, (kt + 1) * tile)
        rel_class = jnp.clip(kt - (step * q_tiles + qt), -2, 2) + 2
        q = q_maps[mp][qt * tile:(qt + 1) * tile]
        s = lax.dot_general(q, k_ref[ks, :], _NT, preferred_element_type=F32) + bias_ref[0, rel_class]
        s_scr[n % 2, :, ks] = s
        return fold(jnp.maximum, run_max, s)

    def weights(n, kt, m):
        ks = slice(kt * tile, (kt + 1) * tile)
        p_scr[n % 2, :, ks] = jnp.exp(s_scr[n % 2, :, ks] - m).astype(BF16)

    def attend(n):
        acc = jnp.dot(p_scr[n % 2], v1_scr[...], preferred_element_type=F32)
        return acc[:, :HEAD_DIM] / acc[:, HEAD_DIM:]

    @pl.when(step == 0)
    def _():
        v1_scr[:, :HEAD_DIM] = v_ref[...]
        v1_scr[:, HEAD_DIM:] = jnp.ones((v1_scr.shape[0], HEAD_DIM), BF16)

    neg_inf = jnp.full((tile, LANES), -jnp.inf, F32)
    run_max = neg_inf
    for kt in range(n_tiles):
        run_max = scores(0, kt, run_max)
    outs = []
    for n in range(len(streams)):
        m = jnp.max(run_max, axis=-1, keepdims=True)
        run_max = neg_inf
        for kt in range(n_tiles):
            weights(n, kt, m)
            if n + 1 < len(streams):
                run_max = scores(n + 1, kt, run_max)
        outs.append(attend(n))
    for qt in range(q_tiles):
        o = outs[2 * qt] - lam_ref[0] * outs[2 * qt + 1]
        ms = jnp.mean(o * o, axis=-1, keepdims=True)
        o_ref[qt * tile:(qt + 1) * tile, :] = ((o * lax.rsqrt(ms + eps) * g_ref[0]) * post_scale).astype(o_ref.dtype)


def _diff_attention(proj, bias_table, gain, lam, lam_init, batch, seq, n_heads_a, n_heads_b, *, tile=256, q_tiles=4):
    t, pw = proj.shape
    n_tiles = seq // tile
    q0 = 3 * n_heads_a
    k0 = q0 + n_heads_b
    v0 = k0 + n_heads_b
    assert tile + 1 >= REL_MAX_DISTANCE
    rel = (jnp.arange(5)[:, None, None] - 2) * tile + jnp.arange(tile)[None, None, :] - jnp.arange(tile)[None, :, None]
    bias = _rel_bias(bias_table, rel)[n_heads_a:]
    n_steps = n_tiles // q_tiles
    rows = q_tiles * tile
    return pl.pallas_call(
        functools.partial(_diff_attn_kernel, tile=tile, q_tiles=q_tiles, n_tiles=n_tiles, eps=SUBLN_EPS,
                          post_scale=1.0 - lam_init),
        grid=(batch, n_heads_b, n_steps),
        in_specs=[pl.BlockSpec(memory_space=pltpu.SMEM),
                  pl.BlockSpec((rows, HEAD_DIM), lambda b, h, i: (b * n_steps + i, q0 + h)),
                  pl.BlockSpec((seq, HEAD_DIM), lambda b, h, i: (b, k0 + h)),
                  pl.BlockSpec((seq, HEAD_DIM), lambda b, h, i: (b, v0 + h)),
                  pl.BlockSpec((1, 5, tile, tile), lambda b, h, i: (h, 0, 0, 0)),
                  pl.BlockSpec((1, 1, HEAD_DIM), lambda b, h, i: (h, 0, 0))],
        out_specs=pl.BlockSpec((rows, HEAD_DIM), lambda b, h, i: (b * n_steps + i, h)),
        out_shape=jax.ShapeDtypeStruct((t, n_heads_b * HEAD_DIM), BF16),
        scratch_shapes=[pltpu.VMEM((2, tile, seq), F32), pltpu.VMEM((2, tile, seq), BF16),
                        pltpu.VMEM((seq, 2 * HEAD_DIM), BF16)],
        compiler_params=_params("arbitrary", "arbitrary", "arbitrary"),
    )(lam.reshape(1), proj, proj, proj, bias, gain.reshape(n_heads_b, 1, HEAD_DIM))


def _out_router_kernel(a_ref, b_ref, x_ref, wo_ref, g_ref, wrh_ref, wrl_ref,
                       x1_ref, hn_ref, route_ref, cnt_ref, *, tm, sub, wa, eps):
    @pl.when(pl.program_id(0) == 0)
    def _():
        cnt_ref[...] = jnp.zeros(cnt_ref.shape, F32)

    lane = lax.broadcasted_iota(jnp.int32, (sub, LANES), 1)
    neg_inf = jnp.float32(-jnp.inf)

    def project(rows):
        y = jnp.dot(a_ref[rows, :], wo_ref[:wa, :], preferred_element_type=F32)
        y = y + jnp.dot(b_ref[rows, :], wo_ref[wa:, :], preferred_element_type=F32)
        x1 = x_ref[rows, :] + y
        x1_ref[rows, :] = x1
        ms = jnp.mean(x1 * x1, axis=-1, keepdims=True)
        hn = x1 * lax.rsqrt(ms + eps) * g_ref[...]
        hn_ref[rows, :] = hn
        hi = hn.astype(BF16)
        lo = (hn - hi.astype(F32)).astype(BF16)
        return (jnp.dot(hi, wrh_ref[...], preferred_element_type=F32)
                + jnp.dot(lo, wrh_ref[...], preferred_element_type=F32)
                + jnp.dot(hi, wrl_ref[...], preferred_element_type=F32))

    def first_max(vals):
        top = jnp.max(vals, axis=-1, keepdims=True)
        idx = jnp.min(jnp.where(vals == top, lane, LANES), axis=-1, keepdims=True)
        return top, idx

    def route(rows, logits):
        is_grp = lane < N_GROUPS
        g_top, g_idx = first_max(jnp.where(is_grp, logits, neg_inf))
        g_sum = jnp.sum(jnp.where(is_grp, jnp.exp(logits - g_top), 0.0), axis=-1, keepdims=True)
        p_grp = 1.0 / g_sum
        e_lo = N_GROUPS + EXPERTS_PER_GROUP * g_idx
        e_vals = jnp.where((lane >= e_lo) & (lane < e_lo + EXPERTS_PER_GROUP), logits, neg_inf)
        top1, i1 = first_max(e_vals)
        top2, i2 = first_max(jnp.where(lane == i1, neg_inf, e_vals))
        b2 = jnp.exp(top2 - top1)
        w0 = p_grp / (1.0 + b2)
        w1 = p_grp * b2 / (1.0 + b2)

        oh0 = (lane == i1).astype(F32)
        oh1 = (lane == i2).astype(F32)
        both = oh0 + oh1
        r_i = lax.broadcasted_iota(jnp.int32, (sub, sub), 0)
        c_i = lax.broadcasted_iota(jnp.int32, (sub, sub), 1)
        before = jnp.where(r_i > c_i, 1.0, 0.0).astype(BF16)
        base = cnt_ref[...] + jnp.dot(before, both.astype(BF16), preferred_element_type=F32)
        rank0 = jnp.sum(base * oh0, axis=-1, keepdims=True)
        rank1 = jnp.sum(base * oh1, axis=-1, keepdims=True)
        cnt_ref[...] = cnt_ref[...] + jnp.sum(both, axis=0, keepdims=True)

        fields = {_R_E0: (i1 - N_GROUPS).astype(F32), _R_E1: (i2 - N_GROUPS).astype(F32),
                  _R_W0: w0, _R_W1: w1, _R_RANK0: rank0, _R_RANK1: rank1}
        slab = jnp.zeros((sub, LANES), F32)
        for n, f in fields.items():
            slab = jnp.where(lane == n, f, slab)
        route_ref[rows, :] = slab

    row_sets = [slice(k * sub, (k + 1) * sub) for k in range(tm // sub)]
    logits = [project(rows) for rows in row_sets]
    for rows, lg in zip(row_sets, logits):
        route(rows, lg)


def _out_router(out_a, out_b, x, w_out, layer, g, wr_hi, wr_lo, *, tm=512, sub=256):
    t, d = x.shape
    wa = out_a.shape[1]
    wb = out_b.shape[1]
    row = lambda i: (i, 0)
    const = lambda i: (0, 0)
    return pl.pallas_call(
        functools.partial(_out_router_kernel, tm=tm, sub=sub, wa=wa, eps=RMS_EPS),
        grid=(t // tm,),
        in_specs=[pl.BlockSpec((tm, wa), row), pl.BlockSpec((tm, wb), row), pl.BlockSpec((tm, d), row),
                  pl.BlockSpec((None, wa + wb, d), lambda i: (layer, 0, 0)), pl.BlockSpec((1, d), const),
                  pl.BlockSpec((d, LANES), const), pl.BlockSpec((d, LANES), const)],
        out_specs=[pl.BlockSpec((tm, d), row), pl.BlockSpec((tm, d), row),
                   pl.BlockSpec((tm, LANES), row), pl.BlockSpec((1, LANES), const)],
        out_shape=[jax.ShapeDtypeStruct((t, d), F32), jax.ShapeDtypeStruct((t, d), F32),
                   jax.ShapeDtypeStruct((t, LANES), F32), jax.ShapeDtypeStruct((1, LANES), F32)],
        compiler_params=_params("arbitrary"),
    )(out_a, out_b, x, w_out, g.reshape(1, d), wr_hi, wr_lo)


def _moe_kernel(nitems_ref, iexp_ref, icnt_ref, ilive_ref, tok_ref, nxt_ref, dst_ref, hn_hbm, wg_ref, wu_ref, wd_ref, out_hbm,
                xbuf, ybuf, gsem, ssem, *, rows, dump_rows):
    del iexp_ref
    it = pl.program_id(0)
    first_half = pl.program_id(1) == 0
    n_items = nitems_ref[0]
    cnt = icnt_ref[it]
    live = ilive_ref[it]
    buf = lax.rem(it, 2)
    tiles = rows // SUBLANES
    d = ybuf.shape[-1]

    def issue_rows(first_tile, n_tiles, copy_row):
        def body(g, carry):
            for u in range(SUBLANES):
                copy_row(first_tile + g, u).start(priority=ROW_COPY_PRIORITY)
            return carry
        lax.fori_loop(0, n_tiles, body, 0)

    def hbm_row(ref, row):
        return ref.at[lax.shift_right_logical(row, SUBLANES.bit_length() - 1), pl.ds(row & (SUBLANES - 1), 1)]

    def start_gather(idx_ref, s, live_tiles):
        issue_rows(0, live_tiles, lambda g, u: pltpu.make_async_copy(
            hbm_row(hn_hbm, idx_ref[0, 0, g * SUBLANES + u]), xbuf.at[s, g, pl.ds(u, 1)], gsem.at[s]))

    def start_scatter(j, live_tiles):
        issue_rows(j * tiles, jnp.clip(live_tiles - j * tiles, 0, tiles), lambda g, u: pltpu.make_async_copy(
            ybuf.at[g, pl.ds(u, 1)], hbm_row(out_hbm, dst_ref[0, 0, g * SUBLANES + u]), ssem.at[0]))

    def wait_tiles(n_tiles, make_tile_copy):
        def body(j, carry):
            make_tile_copy().wait()
            return carry
        lax.fori_loop(0, n_tiles, body, 0)

    def wait_gather(s, live_tiles):
        wait_tiles(live_tiles, lambda: pltpu.make_async_copy(
            hn_hbm.at[pl.ds(0, 1)], xbuf.at[s, pl.ds(0, 1)], gsem.at[s]))

    def wait_scatter(live_tiles):
        wait_tiles(live_tiles, lambda: pltpu.make_async_copy(
            ybuf.at[pl.ds(0, 1)], out_hbm.at[pl.ds(0, 1)], ssem.at[0]))

    def block_tiles(j, n):
        return pl.ds(pl.multiple_of(j * tiles, tiles), n)

    def ffn_half(j, n):
        x = xbuf[buf, block_tiles(j, n)].reshape(n * SUBLANES, d)
        gate = jnp.dot(x, wg_ref[0], preferred_element_type=F32)
        up = jnp.dot(x, wu_ref[0], preferred_element_type=F32)
        hid = gate * (1.0 / (1.0 + jnp.exp(-gate))) * up
        return jnp.dot(hid, wd_ref[0], preferred_element_type=F32).reshape(n, SUBLANES, d)

    def for_block_size(j, fn):
        short = live - j * tiles <= tiles // 2
        for is_short, n in ((False, tiles), (True, tiles // 2)):
            pl.when(short == is_short)(functools.partial(fn, n))

    @pl.when(jnp.logical_and(it < n_items, first_half))
    def _():
        @pl.when(it == 0)
        def _():
            ybuf[pl.ds(0, tiles)] = jnp.zeros((tiles, SUBLANES, d), F32)
            for start in dump_rows:
                dump = pltpu.make_async_copy(ybuf.at[pl.ds(0, tiles)],
                                             out_hbm.at[pl.ds(start // SUBLANES, tiles)], ssem.at[0])
                dump.start()
                dump.wait()

            def clear(g, carry):
                for s in range(2):
                    xbuf[s, g] = jnp.zeros((SUBLANES, d), F32)
                return carry
            lax.fori_loop(0, xbuf.shape[1], clear, 0)
            start_gather(tok_ref, buf, live)

        wait_gather(buf, live)

        def body(j, carry):
            def first_pass(n):
                y = ffn_half(j, n)

                @pl.when(jnp.logical_and(j == 0, it >= 1))
                def _():
                    wait_scatter(ilive_ref[jnp.maximum(it - 1, 0)])

                ybuf[block_tiles(j, n)] = y
            for_block_size(j, first_pass)
            return carry
        lax.fori_loop(0, cnt, body, 0)

    @pl.when(jnp.logical_and(it < n_items, jnp.logical_not(first_half)))
    def _():
        @pl.when(it + 1 < n_items)
        def _():
            start_gather(nxt_ref, 1 - buf, ilive_ref[jnp.minimum(it + 1, ilive_ref.shape[0] - 1)])

        def body(j, carry):
            def second_pass(n):
                ybuf[block_tiles(j, n)] = ybuf[block_tiles(j, n)] + ffn_half(j, n)
            for_block_size(j, second_pass)
            start_scatter(j, live)
            return carry
        lax.fori_loop(0, cnt, body, 0)

        @pl.when(it + 1 >= n_items)
        def _():
            wait_scatter(live)


def _plane_rows(n_tok):
    return n_tok + 2 * MOE_BLOCK


def _moe(hn, item_tok, item_dst, item_exp, item_cnt, item_live, n_items, w_gate, w_up, w_down, layer):
    t, d = hn.shape
    max_items = item_tok.shape[0]
    rows = MOE_BLOCK
    item_rows = MOE_GROUP * rows
    ffn = w_gate.shape[3]
    plane = _plane_rows(t)
    dump_rows = tuple(p * plane + t + s * rows for p in range(2) for s in range(2))
    half_ffn = ffn // 2

    def half(it, h, ni):
        return jnp.where(it < ni[0], h, 1)

    grid_spec = pltpu.PrefetchScalarGridSpec(
        num_scalar_prefetch=4,
        grid=(max_items, 2),
        in_specs=[
            pl.BlockSpec((1, 1, item_rows), lambda it, h, ni, ie, ic, il: (it, 0, 0), memory_space=pltpu.SMEM),
            pl.BlockSpec((1, 1, item_rows), lambda it, h, ni, ie, ic, il: (jnp.minimum(it + 1, max_items - 1), 0, 0),
                         memory_space=pltpu.SMEM),
            pl.BlockSpec((1, 1, item_rows), lambda it, h, ni, ie, ic, il: (it, 0, 0), memory_space=pltpu.SMEM),
            pl.BlockSpec(memory_space=pl.ANY),
            pl.BlockSpec((None, 1, d, half_ffn), lambda it, h, ni, ie, ic, il: (layer, ie[it], 0, half(it, h, ni))),
            pl.BlockSpec((None, 1, d, half_ffn), lambda it, h, ni, ie, ic, il: (layer, ie[it], 0, half(it, h, ni))),
            pl.BlockSpec((None, 1, half_ffn, d), lambda it, h, ni, ie, ic, il: (layer, ie[it], half(it, h, ni), 0)),
        ],
        out_specs=pl.BlockSpec(memory_space=pl.ANY),
        scratch_shapes=[pltpu.VMEM((2, item_rows // SUBLANES, SUBLANES, d), F32),
                        pltpu.VMEM((item_rows // SUBLANES, SUBLANES, d), F32),
                        pltpu.SemaphoreType.DMA((2,)), pltpu.SemaphoreType.DMA((1,))],
    )
    out = pl.pallas_call(
        functools.partial(_moe_kernel, rows=rows, dump_rows=dump_rows),
        grid_spec=grid_spec,
        out_shape=jax.ShapeDtypeStruct((2 * plane // SUBLANES, SUBLANES, d), F32),
        compiler_params=_params("arbitrary", "arbitrary", disable_bounds_checks=True),
    )(n_items, item_exp, item_cnt, item_live, item_tok, item_tok, item_dst, hn.reshape(t // SUBLANES, SUBLANES, d),
      w_gate, w_up, w_down)
    return out.reshape(2 * plane, d)


def _dispatch(route, counts, n_tok):
    rows = MOE_BLOCK
    group = MOE_GROUP
    plane = _plane_rows(n_tok)
    n_blocks = -(-(2 * n_tok) // rows) + N_EXPERTS
    max_items = N_EXPERTS + n_blocks // group
    e = route[:, _R_E0:_R_E1 + 1].astype(jnp.int32)
    rank = route[:, _R_RANK0:_R_RANK1 + 1].astype(jnp.int32)
    cnt = counts[0, N_GROUPS:N_GROUPS + N_EXPERTS].astype(jnp.int32)
    padded = (cnt + rows - 1) // rows * rows
    pend = jnp.cumsum(padded)
    pstart = pend - padded
    dest = (pstart[e] + rank).reshape(-1)
    code = (2 * jnp.arange(n_tok, dtype=jnp.int32)[:, None] + jnp.arange(2, dtype=jnp.int32)[None, :]).reshape(-1)
    row_src = jnp.full((n_blocks * rows,), -1, jnp.int32).at[dest].set(code)
    row = jnp.arange(n_blocks * rows, dtype=jnp.int32)
    row_tok = jnp.maximum(row_src, 0) // 2
    row_dst = jnp.where(row_src >= 0, (row_src % 2) * plane + row_src // 2, n_tok + row % rows)
    n_blk = padded // rows
    first_blk = pstart // rows
    items_end = jnp.cumsum((n_blk + group - 1) // group)
    items_start = items_end - (n_blk + group - 1) // group
    n_items = items_end[-1].astype(jnp.int32)
    it = jnp.arange(max_items, dtype=jnp.int32)
    item_exp = jnp.minimum(jnp.sum(items_end[None, :] <= it[:, None], axis=1), N_EXPERTS - 1).astype(jnp.int32)
    run = it - items_start[item_exp]
    item_blk = first_blk[item_exp] + run * group
    item_cnt = jnp.clip(n_blk[item_exp] - run * group, 0, group)
    live = it < n_items
    item_cnt = jnp.where(live, item_cnt, 0).astype(jnp.int32)
    item_exp = jnp.where(live, item_exp, item_exp[jnp.maximum(n_items - 1, 0)])
    blk_ids = jnp.clip(item_blk[:, None] + jnp.arange(group, dtype=jnp.int32)[None, :], 0, n_blocks - 1)
    item_tok = row_tok.reshape(n_blocks, rows)[blk_ids].reshape(max_items, 1, group * rows)
    item_dst = row_dst.reshape(n_blocks, rows)[blk_ids].reshape(max_items, 1, group * rows)
    real_rows = jnp.clip(cnt[item_exp] - run * group * rows, 0, item_cnt * rows)
    item_live = jnp.where(live, (real_rows + SUBLANES - 1) // SUBLANES, 0).astype(jnp.int32)
    return item_tok, item_dst, item_exp, item_cnt, item_live, n_items.reshape(1)


def _final_kernel(x_ref, route_ref, ya_ref, yb_ref, g_ref, o_ref, *, eps):
    x = _add_expert_outputs(x_ref[...], route_ref, ya_ref, yb_ref)
    ms = jnp.mean(x * x, axis=-1, keepdims=True)
    o_ref[...] = x * lax.rsqrt(ms + eps) * g_ref[...]


def _final_norm(x, moe, g, *, tm=512):
    t, d = x.shape
    route, planes = moe
    row = lambda i: (i, 0)
    return pl.pallas_call(
        functools.partial(_final_kernel, eps=RMS_EPS),
        grid=(t // tm,),
        in_specs=[pl.BlockSpec((tm, d), row), pl.BlockSpec((tm, LANES), row)]
                 + _expert_output_specs(tm, d, t)
                 + [pl.BlockSpec((1, d), lambda i: (0, 0))],
        out_specs=pl.BlockSpec((tm, d), row),
        out_shape=jax.ShapeDtypeStruct((t, d), F32),
        compiler_params=_params("arbitrary"),
    )(x, route, planes, planes, g.reshape(1, d))


def kernel(x, rel_bias, norm_attn, w_in, gain_a, gain_b, lam_q1, lam_k1, lam_q2, lam_k2, w_out, norm_ffn,
           w_router_group, w_router_expert, w_gate, w_up, w_down, norm_final):
    batch, seq, d = x.shape
    depth = w_in.shape[0]
    n_heads_a = gain_a.shape[1]
    n_heads_b = gain_b.shape[1]
    t = batch * seq
    xs = x.reshape(t, d)
    w_in, w_out = w_in.astype(BF16), w_out.astype(BF16)
    dils = tuple(dil for _, dil in DILATED_PATTERNS)
    extra_dils = tuple(dil for dil in dils if dil > 1)
    moe = None
    for l in range(depth):
        proj, views, xs = _norm_proj(xs, moe, norm_attn[l], w_in, l, extra_dils, 3)
        views = dict(zip(extra_dils, views))
        parts = [_dilated_attention(views.get(dil, proj), rel_bias, batch, seq, n_heads_a, w, dil)
                 for (w, dil) in DILATED_PATTERNS]
        out_a = _dilated_mixture([p[0] for p in parts], [p[1] for p in parts], dils, gain_a[l])

        lam_init = 0.8 - 0.6 * math.exp(-0.3 * l)
        lam = (jnp.exp(jnp.sum(lam_q1[l].astype(F32) * lam_k1[l].astype(F32)))
               - jnp.exp(jnp.sum(lam_q2[l].astype(F32) * lam_k2[l].astype(F32))) + lam_init)
        out_b = _diff_attention(proj, rel_bias, gain_b[l], lam, lam_init, batch, seq, n_heads_a, n_heads_b)

        w_r = jnp.concatenate([w_router_group[l], w_router_expert[l].transpose(1, 0, 2).reshape(d, N_EXPERTS)], axis=1)
        w_r = jnp.pad(w_r.astype(F32), ((0, 0), (0, LANES - w_r.shape[1])))
        wr_hi = w_r.astype(BF16)
        wr_lo = (w_r - wr_hi.astype(F32)).astype(BF16)
        xs, hn, route, counts = _out_router(out_a, out_b, xs, w_out, l, norm_ffn[l], wr_hi, wr_lo)

        planes = _moe(hn, *_dispatch(route, counts, t), w_gate, w_up, w_down, l)
        moe = (route, planes)
    out = _final_norm(xs, moe, norm_final)
    return out.reshape(batch, seq, d)
```

```python
import functools
import math

import jax
import jax.numpy as jnp
from jax import lax
from jax.experimental import pallas as pl
from jax.experimental.pallas import tpu as pltpu

F32 = jnp.float32
BF16 = jnp.bfloat16

LANES = 128
HEAD_DIM = 128
DIFF_DIM = HEAD_DIM // 2
DILATED_PATTERNS = ((128, 1), (512, 4), (2048, 16))
DILATED_HALO = 64
N_REL_BUCKETS = 32
REL_MAX_DISTANCE = 128
N_GROUPS = 4
EXPERTS_PER_GROUP = 8
N_EXPERTS = N_GROUPS * EXPERTS_PER_GROUP
MOE_BLOCK = 256
MOE_GROUP = 4
SUBLANES = 8
ROW_COPY_PRIORITY = 1
RMS_EPS = 1e-6
SUBLN_EPS = 1e-5
NEG_BIG = -1e30
VMEM_LIMIT = 56 * 1024 * 1024

_NT = (((1,), (1,)), ((), ()))

_R_E0, _R_E1, _R_W0, _R_W1, _R_RANK0, _R_RANK1 = range(6)


def _t5_bucket(rel):
    nb = N_REL_BUCKETS // 2
    max_exact = nb // 2
    n = -rel
    ret = jnp.where(n < 0, nb, 0)
    n = jnp.abs(n)
    nf = jnp.maximum(n, 1).astype(jnp.float32)
    large = max_exact + (jnp.log(nf / max_exact) / math.log(REL_MAX_DISTANCE / max_exact)
                         * (nb - max_exact)).astype(jnp.int32)
    large = jnp.minimum(large, nb - 1)
    return (ret + jnp.where(n < max_exact, n, large)).astype(jnp.int32)


def _rel_bias(bias_table, rel):
    onehot = (_t5_bucket(rel).reshape(-1)[None, :] == jnp.arange(N_REL_BUCKETS)[:, None]).astype(F32)
    vals = jnp.dot(bias_table.astype(F32).T, onehot, precision=lax.Precision.HIGHEST)
    return vals.reshape((bias_table.shape[1],) + rel.shape)


def _params(*sem, **kwargs):
    return pltpu.CompilerParams(dimension_semantics=sem, vmem_limit_bytes=VMEM_LIMIT, **kwargs)


def _add_expert_outputs(x, route_ref, ya_ref, yb_ref):
    return x + route_ref[:, _R_W0:_R_W0 + 1] * ya_ref[...] + route_ref[:, _R_W1:_R_W1 + 1] * yb_ref[...]


def _expert_output_specs(tm, d, n_tok):
    blocks_per_plane, rem = divmod(n_tok + 2 * MOE_BLOCK, tm)
    assert rem == 0
    return [pl.BlockSpec((tm, d), lambda i, *_, p=p: (p * blocks_per_plane + i, 0)) for p in range(2)]


def _norm_proj_kernel(*refs, has_y, dils, n_split, eps):
    n_in = 6 if has_y else 3
    if has_y:
        x_ref, route_ref, ya_ref, yb_ref, g_ref, w_ref = refs[:n_in]
    else:
        x_ref, g_ref, w_ref = refs[:n_in]
    o_ref = refs[n_in]
    od_refs = refs[n_in + 1:n_in + 1 + len(dils)]
    rest = refs[n_in + 1 + len(dils):]
    xs_ref = rest[0] if has_y else None
    xn_ref, res_ref = rest[-2:]
    j = pl.program_id(1)

    @pl.when(j == 0)
    def _():
        x = x_ref[...]
        if has_y:
            x = _add_expert_outputs(x, route_ref, ya_ref, yb_ref)
            xs_ref[...] = x
        ms = jnp.mean(x * x, axis=-1, keepdims=True)
        xn_ref[...] = (x * lax.rsqrt(ms + eps) * g_ref[...]).astype(BF16)

    res = jnp.dot(xn_ref[...], w_ref[...], preferred_element_type=F32)
    o_ref[...] = res.astype(o_ref.dtype)

    @pl.when(j < n_split)
    def _():
        n_chunks, tm, _ = res_ref.shape
        tn = n_chunks * LANES
        for c in range(n_chunks):
            res_ref[c] = res[:, c * LANES:(c + 1) * LANES]
        for od_ref, dil in zip(od_refs, dils):
            for r in range(dil):
                for c in range(n_chunks):
                    col = r * tn + c * LANES
                    od_ref[:, col:col + LANES] = res_ref[c, pl.ds(r, tm // dil, stride=dil), :].astype(od_ref.dtype)


def _norm_proj(x, moe, g, w, layer, dils, n_split, *, tm=512, tn=1024):
    t, d = x.shape
    n = w.shape[2]
    has_y = moe is not None
    row = lambda i, j: (i, 0)
    in_specs = [pl.BlockSpec((tm, d), row)]
    args = [x]
    if has_y:
        route, planes = moe
        in_specs += [pl.BlockSpec((tm, LANES), row)]
        in_specs += _expert_output_specs(tm, d, t)
        args += [route, planes, planes]
    in_specs += [pl.BlockSpec((1, d), lambda i, j: (0, 0)), pl.BlockSpec((None, d, tn), lambda i, j: (layer, 0, j))]
    args += [g.reshape(1, d), w]
    out_shape = [jax.ShapeDtypeStruct((t, n), BF16)]
    out_specs = [pl.BlockSpec((tm, tn), lambda i, j: (i, j))]
    for dil in dils:
        out_shape.append(jax.ShapeDtypeStruct((t // dil, n_split * dil * tn), BF16))
        out_specs.append(pl.BlockSpec((tm // dil, dil * tn), lambda i, j: (i, jnp.minimum(j, n_split - 1))))
    if has_y:
        out_shape.append(jax.ShapeDtypeStruct((t, d), F32))
        out_specs.append(pl.BlockSpec((tm, d), row))
    res = pl.pallas_call(
        functools.partial(_norm_proj_kernel, has_y=has_y, dils=dils, n_split=n_split, eps=RMS_EPS),
        grid=(t // tm, n // tn),
        in_specs=in_specs, out_specs=out_specs, out_shape=out_shape,
        scratch_shapes=[pltpu.VMEM((tm, d), BF16), pltpu.VMEM((tn // LANES, tm, LANES), F32)],
        compiler_params=_params("arbitrary", "arbitrary"),
    )(*args)
    views = list(res[1:1 + len(dils)])
    return res[0], views, (res[-1] if has_y else x)


def _dilated_kernel(q_ref, kp_ref, kc_ref, kn_ref, vp_ref, vc_ref, vn_ref, bias_ref, o_ref, lse_ref,
                    *, tq, n_sub, n_heads, seq_sub, scale):
    i = pl.program_id(2)
    halo = DILATED_HALO
    win = tq + 2 * halo
    n_tiles = seq_sub // tq
    lane = lax.broadcasted_iota(jnp.int32, (tq, LANES), 1)
    variant = [((i * n_sub + t) == 0).astype(jnp.int32) + 2 * ((i * n_sub + t) == n_tiles - 1).astype(jnp.int32)
               for t in range(n_sub)]
    lse_all = [jnp.zeros((tq, LANES), F32) for _ in range(n_sub)]
    for h in range(n_heads):
        cs = slice(h * HEAD_DIM, (h + 1) * HEAD_DIM)
        k = jnp.concatenate([kp_ref[:, cs], kc_ref[:, cs], kn_ref[:, cs]], axis=0)
        v = jnp.concatenate([vp_ref[:, cs], vc_ref[:, cs], vn_ref[:, cs]], axis=0)
        for t in range(n_sub):
            rows = slice(t * tq, (t + 1) * tq)
            keys = slice(t * tq, t * tq + win)
            s = (lax.dot_general(q_ref[rows, cs], k[keys], _NT, preferred_element_type=F32) * scale
                 + bias_ref[variant[t], h])
            m = jnp.max(s, axis=-1, keepdims=True)
            p = jnp.exp(s - m)
            den = jnp.sum(p, axis=-1, keepdims=True)
            o_ref[rows, cs] = jnp.dot(p.astype(BF16), v[keys], preferred_element_type=F32) / den
            lse_all[t] = jnp.where(lane == h, m + jnp.log(den), lse_all[t])
    for t in range(n_sub):
        lse_ref[t * tq:(t + 1) * tq, :] = lse_all[t]


def _dilated_bias(bias_table, n_heads, window, dilation, tq):
    radius = window // (2 * dilation)
    assert radius <= DILATED_HALO
    win = tq + 2 * DILATED_HALO
    dm = jnp.arange(win)[None, :] - DILATED_HALO - jnp.arange(tq)[:, None]
    b = _rel_bias(bias_table, dm * dilation)[:n_heads]
    b = jnp.where((jnp.abs(dm) <= radius)[None], b, NEG_BIG)
    col = jnp.arange(win)
    variants = []
    for v in range(4):
        gone = ((col < DILATED_HALO) & bool(v & 1)) | ((col >= tq + DILATED_HALO) & bool(v & 2))
        variants.append(jnp.where(gone[None, None, :], NEG_BIG, b))
    return jnp.stack(variants)


def _dilated_attention(view, bias_table, batch, seq, n_heads, window, dilation, *, tq=128, n_sub=4):
    t = view.shape[0] * dilation
    wa = n_heads * HEAD_DIM
    sub = seq // dilation
    n_sub = min(n_sub, sub // tq)
    step = n_sub * tq
    nq = sub // step
    halo = DILATED_HALO
    per_step = step // halo
    n_halo = sub // halo
    bias = _dilated_bias(bias_table, n_heads, window, dilation, tq)

    def main(which):
        return pl.BlockSpec((step, wa), lambda b, r, i: (b * nq + i, which * dilation + r))

    def edge(which, side):
        def imap(b, r, i):
            blk = i * per_step - 1 if side < 0 else (i + 1) * per_step
            return (b * n_halo + jnp.clip(blk, 0, n_halo - 1), which * dilation + r)
        return pl.BlockSpec((halo, wa), imap)

    return pl.pallas_call(
        functools.partial(_dilated_kernel, tq=tq, n_sub=n_sub, n_heads=n_heads, seq_sub=sub,
                          scale=HEAD_DIM ** -0.5),
        grid=(batch, dilation, nq),
        in_specs=[main(0), edge(1, -1), main(1), edge(1, 1), edge(2, -1), main(2), edge(2, 1),
                  pl.BlockSpec(bias.shape, lambda b, r, i: (0, 0, 0, 0))],
        out_specs=[pl.BlockSpec((step, wa), lambda b, r, i: (b * nq + i, r)),
                   pl.BlockSpec((step, LANES), lambda b, r, i: (b * nq + i, r))],
        out_shape=[jax.ShapeDtypeStruct((t // dilation, dilation * wa), F32),
                   jax.ShapeDtypeStruct((t // dilation, dilation * LANES), F32)],
        compiler_params=_params("arbitrary", "arbitrary", "arbitrary"),
    )(view, view, view, view, view, view, view, bias)


def _mix_kernel(*refs, dils, n_heads, eps):
    n = len(dils)
    o_refs, l_refs = list(refs[:n]), list(refs[n:2 * n])
    g_ref, out_ref = refs[2 * n], refs[2 * n + 1]
    scratch = list(refs[2 * n + 2:])
    tm, wa = out_ref.shape
    def head_reader(ref):
        return lambda h: ref[:, h * HEAD_DIM:(h + 1) * HEAD_DIM]

    heads = [head_reader(o_ref) for o_ref in o_refs]
    for p, dil in enumerate(dils):
        if dil == 1:
            continue
        o_tok, l_tok = scratch.pop(0), scratch.pop(0)
        for r in range(dil):
            rows = pl.ds(r, tm // dil, stride=dil)
            for h in range(n_heads):
                o_tok[h, rows, :] = heads[p](r * n_heads + h)
            l_tok[rows, :] = l_refs[p][:, r * LANES:(r + 1) * LANES]
        heads[p], l_refs[p] = (lambda h, ref=o_tok: ref[h]), l_tok
    ls = [l_ref[...] for l_ref in l_refs]
    top = functools.reduce(jnp.maximum, ls)
    es = [jnp.exp(l - top) for l in ls]
    tot = functools.reduce(lambda a, b: a + b, es)
    ws = [e / tot for e in es]
    for h in range(n_heads):
        cs = slice(h * HEAD_DIM, (h + 1) * HEAD_DIM)
        mix = ws[0][:, h:h + 1] * heads[0](h)
        for p in range(1, n):
            mix = mix + ws[p][:, h:h + 1] * heads[p](h)
        ms = jnp.mean(mix * mix, axis=-1, keepdims=True)
        out_ref[:, cs] = (mix * lax.rsqrt(ms + eps) * g_ref[:, cs]).astype(out_ref.dtype)


def _dilated_mixture(outs, lses, dils, gain, *, tm=512):
    wa = gain.size
    t = outs[0].shape[0] * dils[0]
    n_heads = wa // HEAD_DIM
    row = lambda i: (i, 0)
    scratch = []
    for dil in dils:
        if dil > 1:
            scratch += [pltpu.VMEM((n_heads, tm, HEAD_DIM), F32), pltpu.VMEM((tm, LANES), F32)]
    return pl.pallas_call(
        functools.partial(_mix_kernel, dils=dils, n_heads=n_heads, eps=RMS_EPS),
        grid=(t // tm,),
        in_specs=[pl.BlockSpec((tm // dil, dil * wa), row) for dil in dils]
                 + [pl.BlockSpec((tm // dil, dil * LANES), row) for dil in dils]
                 + [pl.BlockSpec((1, wa), lambda i: (0, 0))],
        out_specs=pl.BlockSpec((tm, wa), row),
        out_shape=jax.ShapeDtypeStruct((t, wa), BF16),
        scratch_shapes=scratch,
        compiler_params=_params("arbitrary"),
    )(*outs, *lses, gain.reshape(1, wa))


def _diff_attn_kernel(lam_ref, q_ref, k_ref, v_ref, bias_ref, g_ref, o_ref, s_scr, p_scr, v1_scr,
                      *, tile, q_tiles, n_tiles, eps, post_scale):
    step = pl.program_id(2)
    lane = lax.broadcasted_iota(jnp.int32, (q_tiles * tile, HEAD_DIM), 1)
    qs = q_ref[...] * (DIFF_DIM ** -0.5)
    zero = jnp.zeros_like(qs)
    q_maps = (jnp.where(lane < DIFF_DIM, qs, zero), jnp.where(lane >= DIFF_DIM, qs, zero))
    chunks = tile // LANES
    streams = [(qt, mp) for qt in range(q_tiles) for mp in range(2)]

    def fold(op, acc, vals):
        for c in range(chunks):
            acc = op(acc, vals[:, c * LANES:(c + 1) * LANES])
        return acc

    def scores(n, kt, run_max):
        qt, mp = streams[n]
        ks = slice(kt * tile, (kt + 1) * tile)
        rel_class = jnp.clip(kt - (step * q_tiles + qt), -2, 2) + 2
        q = q_maps[mp][qt * tile:(qt + 1) * tile]
        s = lax.dot_general(q, k_ref[ks, :], _NT, preferred_element_type=F32) + bias_ref[0, rel_class]
        s_scr[n % 2, :, ks] = s
        return fold(jnp.maximum, run_max, s)

    def weights(n, kt, m):
        ks = slice(kt * tile, (kt + 1) * tile)
        p_scr[n % 2, :, ks] = jnp.exp(s_scr[n % 2, :, ks] - m).astype(BF16)

    def attend(n):
        acc = jnp.dot(p_scr[n % 2], v1_scr[...], preferred_element_type=F32)
        return acc[:, :HEAD_DIM] / acc[:, HEAD_DIM:]

    @pl.when(step == 0)
    def _():
        v1_scr[:, :HEAD_DIM] = v_ref[...]
        v1_scr[:, HEAD_DIM:] = jnp.ones((v1_scr.shape[0], HEAD_DIM), BF16)

    neg_inf = jnp.full((tile, LANES), -jnp.inf, F32)
    run_max = neg_inf
    for kt in range(n_tiles):
        run_max = scores(0, kt, run_max)
    outs = []
    for n in range(len(streams)):
        m = jnp.max(run_max, axis=-1, keepdims=True)
        run_max = neg_inf
        for kt in range(n_tiles):
            weights(n, kt, m)
            if n + 1 < len(streams):
                run_max = scores(n + 1, kt, run_max)
        outs.append(attend(n))
    for qt in range(q_tiles):
        o = outs[2 * qt] - lam_ref[0] * outs[2 * qt + 1]
        ms = jnp.mean(o * o, axis=-1, keepdims=True)
        o_ref[qt * tile:(qt + 1) * tile, :] = ((o * lax.rsqrt(ms + eps) * g_ref[0]) * post_scale).astype(o_ref.dtype)


def _diff_attention(proj, bias_table, gain, lam, lam_init, batch, seq, n_heads_a, n_heads_b, *, tile=256, q_tiles=4):
    t, pw = proj.shape
    n_tiles = seq // tile
    q0 = 3 * n_heads_a
    k0 = q0 + n_heads_b
    v0 = k0 + n_heads_b
    assert tile + 1 >= REL_MAX_DISTANCE
    rel = (jnp.arange(5)[:, None, None] - 2) * tile + jnp.arange(tile)[None, None, :] - jnp.arange(tile)[None, :, None]
    bias = _rel_bias(bias_table, rel)[n_heads_a:]
    n_steps = n_tiles // q_tiles
    rows = q_tiles * tile
    return pl.pallas_call(
        functools.partial(_diff_attn_kernel, tile=tile, q_tiles=q_tiles, n_tiles=n_tiles, eps=SUBLN_EPS,
                          post_scale=1.0 - lam_init),
        grid=(batch, n_heads_b, n_steps),
        in_specs=[pl.BlockSpec(memory_space=pltpu.SMEM),
                  pl.BlockSpec((rows, HEAD_DIM), lambda b, h, i: (b * n_steps + i, q0 + h)),
                  pl.BlockSpec((seq, HEAD_DIM), lambda b, h, i: (b, k0 + h)),
                  pl.BlockSpec((seq, HEAD_DIM), lambda b, h, i: (b, v0 + h)),
                  pl.BlockSpec((1, 5, tile, tile), lambda b, h, i: (h, 0, 0, 0)),
                  pl.BlockSpec((1, 1, HEAD_DIM), lambda b, h, i: (h, 0, 0))],
        out_specs=pl.BlockSpec((rows, HEAD_DIM), lambda b, h, i: (b * n_steps + i, h)),
        out_shape=jax.ShapeDtypeStruct((t, n_heads_b * HEAD_DIM), BF16),
        scratch_shapes=[pltpu.VMEM((2, tile, seq), F32), pltpu.VMEM((2, tile, seq), BF16),
                        pltpu.VMEM((seq, 2 * HEAD_DIM), BF16)],
        compiler_params=_params("arbitrary", "arbitrary", "arbitrary"),
    )(lam.reshape(1), proj, proj, proj, bias, gain.reshape(n_heads_b, 1, HEAD_DIM))


def _out_router_kernel(a_ref, b_ref, x_ref, wo_ref, g_ref, wrh_ref, wrl_ref,
                       x1_ref, hn_ref, route_ref, cnt_ref, *, tm, sub, wa, eps):
    @pl.when(pl.program_id(0) == 0)
    def _():
        cnt_ref[...] = jnp.zeros(cnt_ref.shape, F32)

    lane = lax.broadcasted_iota(jnp.int32, (sub, LANES), 1)
    neg_inf = jnp.float32(-jnp.inf)

    def project(rows):
        y = jnp.dot(a_ref[rows, :], wo_ref[:wa, :], preferred_element_type=F32)
        y = y + jnp.dot(b_ref[rows, :], wo_ref[wa:, :], preferred_element_type=F32)
        x1 = x_ref[rows, :] + y
        x1_ref[rows, :] = x1
        ms = jnp.mean(x1 * x1, axis=-1, keepdims=True)
        hn = x1 * lax.rsqrt(ms + eps) * g_ref[...]
        hn_ref[rows, :] = hn
        hi = hn.astype(BF16)
        lo = (hn - hi.astype(F32)).astype(BF16)
        return (jnp.dot(hi, wrh_ref[...], preferred_element_type=F32)
                + jnp.dot(lo, wrh_ref[...], preferred_element_type=F32)
                + jnp.dot(hi, wrl_ref[...], preferred_element_type=F32))

    def first_max(vals):
        top = jnp.max(vals, axis=-1, keepdims=True)
        idx = jnp.min(jnp.where(vals == top, lane, LANES), axis=-1, keepdims=True)
        return top, idx

    def route(rows, logits):
        is_grp = lane < N_GROUPS
        g_top, g_idx = first_max(jnp.where(is_grp, logits, neg_inf))
        g_sum = jnp.sum(jnp.where(is_grp, jnp.exp(logits - g_top), 0.0), axis=-1, keepdims=True)
        p_grp = 1.0 / g_sum
        e_lo = N_GROUPS + EXPERTS_PER_GROUP * g_idx
        e_vals = jnp.where((lane >= e_lo) & (lane < e_lo + EXPERTS_PER_GROUP), logits, neg_inf)
        top1, i1 = first_max(e_vals)
        top2, i2 = first_max(jnp.where(lane == i1, neg_inf, e_vals))
        b2 = jnp.exp(top2 - top1)
        w0 = p_grp / (1.0 + b2)
        w1 = p_grp * b2 / (1.0 + b2)

        oh0 = (lane == i1).astype(F32)
        oh1 = (lane == i2).astype(F32)
        both = oh0 + oh1
        r_i = lax.broadcasted_iota(jnp.int32, (sub, sub), 0)
        c_i = lax.broadcasted_iota(jnp.int32, (sub, sub), 1)
        before = jnp.where(r_i > c_i, 1.0, 0.0).astype(BF16)
        base = cnt_ref[...] + jnp.dot(before, both.astype(BF16), preferred_element_type=F32)
        rank0 = jnp.sum(base * oh0, axis=-1, keepdims=True)
        rank1 = jnp.sum(base * oh1, axis=-1, keepdims=True)
        cnt_ref[...] = cnt_ref[...] + jnp.sum(both, axis=0, keepdims=True)

        fields = {_R_E0: (i1 - N_GROUPS).astype(F32), _R_E1: (i2 - N_GROUPS).astype(F32),
                  _R_W0: w0, _R_W1: w1, _R_RANK0: rank0, _R_RANK1: rank1}
        slab = jnp.zeros((sub, LANES), F32)
        for n, f in fields.items():
            slab = jnp.where(lane == n, f, slab)
        route_ref[rows, :] = slab

    row_sets = [slice(k * sub, (k + 1) * sub) for k in range(tm // sub)]
    logits = [project(rows) for rows in row_sets]
    for rows, lg in zip(row_sets, logits):
        route(rows, lg)


def _out_router(out_a, out_b, x, w_out, layer, g, wr_hi, wr_lo, *, tm=512, sub=256):
    t, d = x.shape
    wa = out_a.shape[1]
    wb = out_b.shape[1]
    row = lambda i: (i, 0)
    const = lambda i: (0, 0)
    return pl.pallas_call(
        functools.partial(_out_router_kernel, tm=tm, sub=sub, wa=wa, eps=RMS_EPS),
        grid=(t // tm,),
        in_specs=[pl.BlockSpec((tm, wa), row), pl.BlockSpec((tm, wb), row), pl.BlockSpec((tm, d), row),
                  pl.BlockSpec((None, wa + wb, d), lambda i: (layer, 0, 0)), pl.BlockSpec((1, d), const),
                  pl.BlockSpec((d, LANES), const), pl.BlockSpec((d, LANES), const)],
        out_specs=[pl.BlockSpec((tm, d), row), pl.BlockSpec((tm, d), row),
                   pl.BlockSpec((tm, LANES), row), pl.BlockSpec((1, LANES), const)],
        out_shape=[jax.ShapeDtypeStruct((t, d), F32), jax.ShapeDtypeStruct((t, d), F32),
                   jax.ShapeDtypeStruct((t, LANES), F32), jax.ShapeDtypeStruct((1, LANES), F32)],
        compiler_params=_params("arbitrary"),
    )(out_a, out_b, x, w_out, g.reshape(1, d), wr_hi, wr_lo)


def _moe_kernel(nitems_ref, iexp_ref, icnt_ref, ilive_ref, tok_ref, nxt_ref, dst_ref, hn_hbm, wg_ref, wu_ref, wd_ref, out_hbm,
                xbuf, ybuf, gsem, ssem, *, rows, dump_rows):
    del iexp_ref
    it = pl.program_id(0)
    first_half = pl.program_id(1) == 0
    n_items = nitems_ref[0]
    cnt = icnt_ref[it]
    live = ilive_ref[it]
    buf = lax.rem(it, 2)
    tiles = rows // SUBLANES
    d = ybuf.shape[-1]

    def issue_rows(first_tile, n_tiles, copy_row):
        def body(g, carry):
            for u in range(SUBLANES):
                copy_row(first_tile + g, u).start(priority=ROW_COPY_PRIORITY)
            return carry
        lax.fori_loop(0, n_tiles, body, 0)

    def hbm_row(ref, row):
        return ref.at[lax.shift_right_logical(row, SUBLANES.bit_length() - 1), pl.ds(row & (SUBLANES - 1), 1)]

    def start_gather(idx_ref, s, live_tiles):
        issue_rows(0, live_tiles, lambda g, u: pltpu.make_async_copy(
            hbm_row(hn_hbm, idx_ref[0, 0, g * SUBLANES + u]), xbuf.at[s, g, pl.ds(u, 1)], gsem.at[s]))

    def start_scatter(j, live_tiles):
        issue_rows(j * tiles, jnp.clip(live_tiles - j * tiles, 0, tiles), lambda g, u: pltpu.make_async_copy(
            ybuf.at[g, pl.ds(u, 1)], hbm_row(out_hbm, dst_ref[0, 0, g * SUBLANES + u]), ssem.at[0]))

    def wait_tiles(n_tiles, make_tile_copy):
        def body(j, carry):
            make_tile_copy().wait()
            return carry
        lax.fori_loop(0, n_tiles, body, 0)

    def wait_gather(s, live_tiles):
        wait_tiles(live_tiles, lambda: pltpu.make_async_copy(
            hn_hbm.at[pl.ds(0, 1)], xbuf.at[s, pl.ds(0, 1)], gsem.at[s]))

    def wait_scatter(live_tiles):
        wait_tiles(live_tiles, lambda: pltpu.make_async_copy(
            ybuf.at[pl.ds(0, 1)], out_hbm.at[pl.ds(0, 1)], ssem.at[0]))

    def block_tiles(j, n):
        return pl.ds(pl.multiple_of(j * tiles, tiles), n)

    def ffn_half(j, n):
        x = xbuf[buf, block_tiles(j, n)].reshape(n * SUBLANES, d)
        gate = jnp.dot(x, wg_ref[0], preferred_element_type=F32)
        up = jnp.dot(x, wu_ref[0], preferred_element_type=F32)
        hid = gate * (1.0 / (1.0 + jnp.exp(-gate))) * up
        return jnp.dot(hid, wd_ref[0], preferred_element_type=F32).reshape(n, SUBLANES, d)

    def for_block_size(j, fn):
        short = live - j * tiles <= tiles // 2
        for is_short, n in ((False, tiles), (True, tiles // 2)):
            pl.when(short == is_short)(functools.partial(fn, n))

    @pl.when(jnp.logical_and(it < n_items, first_half))
    def _():
        @pl.when(it == 0)
        def _():
            ybuf[pl.ds(0, tiles)] = jnp.zeros((tiles, SUBLANES, d), F32)
            for start in dump_rows:
                dump = pltpu.make_async_copy(ybuf.at[pl.ds(0, tiles)],
                                             out_hbm.at[pl.ds(start // SUBLANES, tiles)], ssem.at[0])
                dump.start()
                dump.wait()

            def clear(g, carry):
                for s in range(2):
                    xbuf[s, g] = jnp.zeros((SUBLANES, d), F32)
                return carry
            lax.fori_loop(0, xbuf.shape[1], clear, 0)
            start_gather(tok_ref, buf, live)

        wait_gather(buf, live)

        def body(j, carry):
            def first_pass(n):
                y = ffn_half(j, n)

                @pl.when(jnp.logical_and(j == 0, it >= 1))
                def _():
                    wait_scatter(ilive_ref[jnp.maximum(it - 1, 0)])

                ybuf[block_tiles(j, n)] = y
            for_block_size(j, first_pass)
            return carry
        lax.fori_loop(0, cnt, body, 0)

    @pl.when(jnp.logical_and(it < n_items, jnp.logical_not(first_half)))
    def _():
        @pl.when(it + 1 < n_items)
        def _():
            start_gather(nxt_ref, 1 - buf, ilive_ref[jnp.minimum(it + 1, ilive_ref.shape[0] - 1)])

        def body(j, carry):
            def second_pass(n):
                ybuf[block_tiles(j, n)] = ybuf[block_tiles(j, n)] + ffn_half(j, n)
            for_block_size(j, second_pass)
            start_scatter(j, live)
            return carry
        lax.fori_loop(0, cnt, body, 0)

        @pl.when(it + 1 >= n_items)
        def _():
            wait_scatter(live)


def _plane_rows(n_tok):
    return n_tok + 2 * MOE_BLOCK


def _moe(hn, item_tok, item_dst, item_exp, item_cnt, item_live, n_items, w_gate, w_up, w_down, layer):
    t, d = hn.shape
    max_items = item_tok.shape[0]
    rows = MOE_BLOCK
    item_rows = MOE_GROUP * rows
    ffn = w_gate.shape[3]
    plane = _plane_rows(t)
    dump_rows = tuple(p * plane + t + s * rows for p in range(2) for s in range(2))
    half_ffn = ffn // 2

    def half(it, h, ni):
        return jnp.where(it < ni[0], h, 1)

    grid_spec = pltpu.PrefetchScalarGridSpec(
        num_scalar_prefetch=4,
        grid=(max_items, 2),
        in_specs=[
            pl.BlockSpec((1, 1, item_rows), lambda it, h, ni, ie, ic, il: (it, 0, 0), memory_space=pltpu.SMEM),
            pl.BlockSpec((1, 1, item_rows), lambda it, h, ni, ie, ic, il: (jnp.minimum(it + 1, max_items - 1), 0, 0),
                         memory_space=pltpu.SMEM),
            pl.BlockSpec((1, 1, item_rows), lambda it, h, ni, ie, ic, il: (it, 0, 0), memory_space=pltpu.SMEM),
            pl.BlockSpec(memory_space=pl.ANY),
            pl.BlockSpec((None, 1, d, half_ffn), lambda it, h, ni, ie, ic, il: (layer, ie[it], 0, half(it, h, ni))),
            pl.BlockSpec((None, 1, d, half_ffn), lambda it, h, ni, ie, ic, il: (layer, ie[it], 0, half(it, h, ni))),
            pl.BlockSpec((None, 1, half_ffn, d), lambda it, h, ni, ie, ic, il: (layer, ie[it], half(it, h, ni), 0)),
        ],
        out_specs=pl.BlockSpec(memory_space=pl.ANY),
        scratch_shapes=[pltpu.VMEM((2, item_rows // SUBLANES, SUBLANES, d), F32),
                        pltpu.VMEM((item_rows // SUBLANES, SUBLANES, d), F32),
                        pltpu.SemaphoreType.DMA((2,)), pltpu.SemaphoreType.DMA((1,))],
    )
    out = pl.pallas_call(
        functools.partial(_moe_kernel, rows=rows, dump_rows=dump_rows),
        grid_spec=grid_spec,
        out_shape=jax.ShapeDtypeStruct((2 * plane // SUBLANES, SUBLANES, d), F32),
        compiler_params=_params("arbitrary", "arbitrary", disable_bounds_checks=True),
    )(n_items, item_exp, item_cnt, item_live, item_tok, item_tok, item_dst, hn.reshape(t // SUBLANES, SUBLANES, d),
      w_gate, w_up, w_down)
    return out.reshape(2 * plane, d)


def _dispatch(route, counts, n_tok):
    rows = MOE_BLOCK
    group = MOE_GROUP
    plane = _plane_rows(n_tok)
    n_blocks = -(-(2 * n_tok) // rows) + N_EXPERTS
    max_items = N_EXPERTS + n_blocks // group
    e = route[:, _R_E0:_R_E1 + 1].astype(jnp.int32)
    rank = route[:, _R_RANK0:_R_RANK1 + 1].astype(jnp.int32)
    cnt = counts[0, N_GROUPS:N_GROUPS + N_EXPERTS].astype(jnp.int32)
    padded = (cnt + rows - 1) // rows * rows
    pend = jnp.cumsum(padded)
    pstart = pend - padded
    dest = (pstart[e] + rank).reshape(-1)
    code = (2 * jnp.arange(n_tok, dtype=jnp.int32)[:, None] + jnp.arange(2, dtype=jnp.int32)[None, :]).reshape(-1)
    row_src = jnp.full((n_blocks * rows,), -1, jnp.int32).at[dest].set(code)
    row = jnp.arange(n_blocks * rows, dtype=jnp.int32)
    row_tok = jnp.maximum(row_src, 0) // 2
    row_dst = jnp.where(row_src >= 0, (row_src % 2) * plane + row_src // 2, n_tok + row % rows)
    n_blk = padded // rows
    first_blk = pstart // rows
    items_end = jnp.cumsum((n_blk + group - 1) // group)
    items_start = items_end - (n_blk + group - 1) // group
    n_items = items_end[-1].astype(jnp.int32)
    it = jnp.arange(max_items, dtype=jnp.int32)
    item_exp = jnp.minimum(jnp.sum(items_end[None, :] <= it[:, None], axis=1), N_EXPERTS - 1).astype(jnp.int32)
    run = it - items_start[item_exp]
    item_blk = first_blk[item_exp] + run * group
    item_cnt = jnp.clip(n_blk[item_exp] - run * group, 0, group)
    live = it < n_items
    item_cnt = jnp.where(live, item_cnt, 0).astype(jnp.int32)
    item_exp = jnp.where(live, item_exp, item_exp[jnp.maximum(n_items - 1, 0)])
    blk_ids = jnp.clip(item_blk[:, None] + jnp.arange(group, dtype=jnp.int32)[None, :], 0, n_blocks - 1)
    item_tok = row_tok.reshape(n_blocks, rows)[blk_ids].reshape(max_items, 1, group * rows)
    item_dst = row_dst.reshape(n_blocks, rows)[blk_ids].reshape(max_items, 1, group * rows)
    real_rows = jnp.clip(cnt[item_exp] - run * group * rows, 0, item_cnt * rows)
    item_live = jnp.where(live, (real_rows + SUBLANES - 1) // SUBLANES, 0).astype(jnp.int32)
    return item_tok, item_dst, item_exp, item_cnt, item_live, n_items.reshape(1)


def _final_kernel(x_ref, route_ref, ya_ref, yb_ref, g_ref, o_ref, *, eps):
    x = _add_expert_outputs(x_ref[...], route_ref, ya_ref, yb_ref)
    ms = jnp.mean(x * x, axis=-1, keepdims=True)
    o_ref[...] = x * lax.rsqrt(ms + eps) * g_ref[...]


def _final_norm(x, moe, g, *, tm=512):
    t, d = x.shape
    route, planes = moe
    row = lambda i: (i, 0)
    return pl.pallas_call(
        functools.partial(_final_kernel, eps=RMS_EPS),
        grid=(t // tm,),
        in_specs=[pl.BlockSpec((tm, d), row), pl.BlockSpec((tm, LANES), row)]
                 + _expert_output_specs(tm, d, t)
                 + [pl.BlockSpec((1, d), lambda i: (0, 0))],
        out_specs=pl.BlockSpec((tm, d), row),
        out_shape=jax.ShapeDtypeStruct((t, d), F32),
        compiler_params=_params("arbitrary"),
    )(x, route, planes, planes, g.reshape(1, d))


def kernel(x, rel_bias, norm_attn, w_in, gain_a, gain_b, lam_q1, lam_k1, lam_q2, lam_k2, w_out, norm_ffn,
           w_router_group, w_router_expert, w_gate, w_up, w_down, norm_final):
    batch, seq, d = x.shape
    depth = w_in.shape[0]
    n_heads_a = gain_a.shape[1]
    n_heads_b = gain_b.shape[1]
    t = batch * seq
    xs = x.reshape(t, d)
    w_in, w_out = w_in.astype(BF16), w_out.astype(BF16)
    dils = tuple(dil for _, dil in DILATED_PATTERNS)
    extra_dils = tuple(dil for dil in dils if dil > 1)
    moe = None
    for l in range(depth):
        proj, views, xs = _norm_proj(xs, moe, norm_attn[l], w_in, l, extra_dils, 3)
        views = dict(zip(extra_dils, views))
        parts = [_dilated_attention(views.get(dil, proj), rel_bias, batch, seq, n_heads_a, w, dil)
                 for (w, dil) in DILATED_PATTERNS]
        out_a = _dilated_mixture([p[0] for p in parts], [p[1] for p in parts], dils, gain_a[l])

        lam_init = 0.8 - 0.6 * math.exp(-0.3 * l)
        lam = (jnp.exp(jnp.sum(lam_q1[l].astype(F32) * lam_k1[l].astype(F32)))
               - jnp.exp(jnp.sum(lam_q2[l].astype(F32) * lam_k2[l].astype(F32))) + lam_init)
        out_b = _diff_attention(proj, rel_bias, gain_b[l], lam, lam_init, batch, seq, n_heads_a, n_heads_b)

        w_r = jnp.concatenate([w_router_group[l], w_router_expert[l].transpose(1, 0, 2).reshape(d, N_EXPERTS)], axis=1)
        w_r = jnp.pad(w_r.astype(F32), ((0, 0), (0, LANES - w_r.shape[1])))
        wr_hi = w_r.astype(BF16)
        wr_lo = (w_r - wr_hi.astype(F32)).astype(BF16)
        xs, hn, route, counts = _out_router(out_a, out_b, xs, w_out, l, norm_ffn[l], wr_hi, wr_lo)

        planes = _moe(hn, *_dispatch(route, counts, t), w_gate, w_up, w_down, l)
        moe = (route, planes)
    out = _final_norm(xs, moe, norm_final)
    return out.reshape(batch, seq, d)
```

```python
import functools
import math

import jax
import jax.numpy as jnp
from jax import lax
from jax.experimental import pallas as pl
from jax.experimental.pallas import tpu as pltpu

F32 = jnp.float32
BF16 = jnp.bfloat16

LANES = 128
HEAD_DIM = 128
DIFF_DIM = HEAD_DIM // 2
DILATED_PATTERNS = ((128, 1), (512, 4), (2048, 16))
DILATED_HALO = 64
N_REL_BUCKETS = 32
REL_MAX_DISTANCE = 128
N_GROUPS = 4
EXPERTS_PER_GROUP = 8
N_EXPERTS = N_GROUPS * EXPERTS_PER_GROUP
MOE_BLOCK = 256
MOE_GROUP = 4
SUBLANES = 8
ROW_COPY_PRIORITY = 1
RMS_EPS = 1e-6
SUBLN_EPS = 1e-5
NEG_BIG = -1e30
VMEM_LIMIT = 56 * 1024 * 1024

_NT = (((1,), (1,)), ((), ()))

_R_E0, _R_E1, _R_W0, _R_W1, _R_RANK0, _R_RANK1 = range(6)


def _t5_bucket(rel):
    nb = N_REL_BUCKETS // 2
    max_exact = nb // 2
    n = -rel
    ret = jnp.where(n < 0, nb, 0)
    n = jnp.abs(n)
    nf = jnp.maximum(n, 1).astype(jnp.float32)
    large = max_exact + (jnp.log(nf / max_exact) / math.log(REL_MAX_DISTANCE / max_exact)
                         * (nb - max_exact)).astype(jnp.int32)
    large = jnp.minimum(large, nb - 1)
    return (ret + jnp.where(n < max_exact, n, large)).astype(jnp.int32)


def _rel_bias(bias_table, rel):
    onehot = (_t5_bucket(rel).reshape(-1)[None, :] == jnp.arange(N_REL_BUCKETS)[:, None]).astype(F32)
    vals = jnp.dot(bias_table.astype(F32).T, onehot, precision=lax.Precision.HIGHEST)
    return vals.reshape((bias_table.shape[1],) + rel.shape)


def _params(*sem, **kwargs):
    return pltpu.CompilerParams(dimension_semantics=sem, vmem_limit_bytes=VMEM_LIMIT, **kwargs)


def _add_expert_outputs(x, route_ref, ya_ref, yb_ref):
    return x + route_ref[:, _R_W0:_R_W0 + 1] * ya_ref[...] + route_ref[:, _R_W1:_R_W1 + 1] * yb_ref[...]


def _expert_output_specs(tm, d, n_tok):
    blocks_per_plane, rem = divmod(n_tok + 2 * MOE_BLOCK, tm)
    assert rem == 0
    return [pl.BlockSpec((tm, d), lambda i, *_, p=p: (p * blocks_per_plane + i, 0)) for p in range(2)]


def _norm_proj_kernel(*refs, has_y, dils, n_split, eps):
    n_in = 6 if has_y else 3
    if has_y:
        x_ref, route_ref, ya_ref, yb_ref, g_ref, w_ref = refs[:n_in]
    else:
        x_ref, g_ref, w_ref = refs[:n_in]
    o_ref = refs[n_in]
    od_refs = refs[n_in + 1:n_in + 1 + len(dils)]
    rest = refs[n_in + 1 + len(dils):]
    xs_ref = rest[0] if has_y else None
    xn_ref, res_ref = rest[-2:]
    j = pl.program_id(1)

    @pl.when(j == 0)
    def _():
        x = x_ref[...]
        if has_y:
            x = _add_expert_outputs(x, route_ref, ya_ref, yb_ref)
            xs_ref[...] = x
        ms = jnp.mean(x * x, axis=-1, keepdims=True)
        xn_ref[...] = (x * lax.rsqrt(ms + eps) * g_ref[...]).astype(BF16)

    res = jnp.dot(xn_ref[...], w_ref[...], preferred_element_type=F32)
    o_ref[...] = res.astype(o_ref.dtype)

    @pl.when(j < n_split)
    def _():
        n_chunks, tm, _ = res_ref.shape
        tn = n_chunks * LANES
        for c in range(n_chunks):
            res_ref[c] = res[:, c * LANES:(c + 1) * LANES]
        for od_ref, dil in zip(od_refs, dils):
            for r in range(dil):
                for c in range(n_chunks):
                    col = r * tn + c * LANES
                    od_ref[:, col:col + LANES] = res_ref[c, pl.ds(r, tm // dil, stride=dil), :].astype(od_ref.dtype)


def _norm_proj(x, moe, g, w, layer, dils, n_split, *, tm=512, tn=1024):
    t, d = x.shape
    n = w.shape[2]
    has_y = moe is not None
    row = lambda i, j: (i, 0)
    in_specs = [pl.BlockSpec((tm, d), row)]
    args = [x]
    if has_y:
        route, planes = moe
        in_specs += [pl.BlockSpec((tm, LANES), row)]
        in_specs += _expert_output_specs(tm, d, t)
        args += [route, planes, planes]
    in_specs += [pl.BlockSpec((1, d), lambda i, j: (0, 0)), pl.BlockSpec((None, d, tn), lambda i, j: (layer, 0, j))]
    args += [g.reshape(1, d), w]
    out_shape = [jax.ShapeDtypeStruct((t, n), BF16)]
    out_specs = [pl.BlockSpec((tm, tn), lambda i, j: (i, j))]
    for dil in dils:
        out_shape.append(jax.ShapeDtypeStruct((t // dil, n_split * dil * tn), BF16))
        out_specs.append(pl.BlockSpec((tm // dil, dil * tn), lambda i, j: (i, jnp.minimum(j, n_split - 1))))
    if has_y:
        out_shape.append(jax.ShapeDtypeStruct((t, d), F32))
        out_specs.append(pl.BlockSpec((tm, d), row))
    res = pl.pallas_call(
        functools.partial(_norm_proj_kernel, has_y=has_y, dils=dils, n_split=n_split, eps=RMS_EPS),
        grid=(t // tm, n // tn),
        in_specs=in_specs, out_specs=out_specs, out_shape=out_shape,
        scratch_shapes=[pltpu.VMEM((tm, d), BF16), pltpu.VMEM((tn // LANES, tm, LANES), F32)],
        compiler_params=_params("arbitrary", "arbitrary"),
    )(*args)
    views = list(res[1:1 + len(dils)])
    return res[0], views, (res[-1] if has_y else x)


def _dilated_kernel(q_ref, kp_ref, kc_ref, kn_ref, vp_ref, vc_ref, vn_ref, bias_ref, o_ref, lse_ref,
                    *, tq, n_sub, n_heads, seq_sub, scale):
    i = pl.program_id(2)
    halo = DILATED_HALO
    win = tq + 2 * halo
    n_tiles = seq_sub // tq
    lane = lax.broadcasted_iota(jnp.int32, (tq, LANES), 1)
    variant = [((i * n_sub + t) == 0).astype(jnp.int32) + 2 * ((i * n_sub + t) == n_tiles - 1).astype(jnp.int32)
               for t in range(n_sub)]
    lse_all = [jnp.zeros((tq, LANES), F32) for _ in range(n_sub)]
    for h in range(n_heads):
        cs = slice(h * HEAD_DIM, (h + 1) * HEAD_DIM)
        k = jnp.concatenate([kp_ref[:, cs], kc_ref[:, cs], kn_ref[:, cs]], axis=0)
        v = jnp.concatenate([vp_ref[:, cs], vc_ref[:, cs], vn_ref[:, cs]], axis=0)
        for t in range(n_sub):
            rows = slice(t * tq, (t + 1) * tq)
            keys = slice(t * tq, t * tq + win)
            s = (lax.dot_general(q_ref[rows, cs], k[keys], _NT, preferred_element_type=F32) * scale
                 + bias_ref[variant[t], h])
            m = jnp.max(s, axis=-1, keepdims=True)
            p = jnp.exp(s - m)
            den = jnp.sum(p, axis=-1, keepdims=True)
            o_ref[rows, cs] = jnp.dot(p.astype(BF16), v[keys], preferred_element_type=F32) / den
            lse_all[t] = jnp.where(lane == h, m + jnp.log(den), lse_all[t])
    for t in range(n_sub):
        lse_ref[t * tq:(t + 1) * tq, :] = lse_all[t]


def _dilated_bias(bias_table, n_heads, window, dilation, tq):
    radius = window // (2 * dilation)
    assert radius <= DILATED_HALO
    win = tq + 2 * DILATED_HALO
    dm = jnp.arange(win)[None, :] - DILATED_HALO - jnp.arange(tq)[:, None]
    b = _rel_bias(bias_table, dm * dilation)[:n_heads]
    b = jnp.where((jnp.abs(dm) <= radius)[None], b, NEG_BIG)
    col = jnp.arange(win)
    variants = []
    for v in range(4):
        gone = ((col < DILATED_HALO) & bool(v & 1)) | ((col >= tq + DILATED_HALO) & bool(v & 2))
        variants.append(jnp.where(gone[None, None, :], NEG_BIG, b))
    return jnp.stack(variants)


def _dilated_attention(view, bias_table, batch, seq, n_heads, window, dilation, *, tq=128, n_sub=4):
    t = view.shape[0] * dilation
    wa = n_heads * HEAD_DIM
    sub = seq // dilation
    n_sub = min(n_sub, sub // tq)
    step = n_sub * tq
    nq = sub // step
    halo = DILATED_HALO
    per_step = step // halo
    n_halo = sub // halo
    bias = _dilated_bias(bias_table, n_heads, window, dilation, tq)

    def main(which):
        return pl.BlockSpec((step, wa), lambda b, r, i: (b * nq + i, which * dilation + r))

    def edge(which, side):
        def imap(b, r, i):
            blk = i * per_step - 1 if side < 0 else (i + 1) * per_step
            return (b * n_halo + jnp.clip(blk, 0, n_halo - 1), which * dilation + r)
        return pl.BlockSpec((halo, wa), imap)

    return pl.pallas_call(
        functools.partial(_dilated_kernel, tq=tq, n_sub=n_sub, n_heads=n_heads, seq_sub=sub,
                          scale=HEAD_DIM ** -0.5),
        grid=(batch, dilation, nq),
        in_specs=[main(0), edge(1, -1), main(1), edge(1, 1), edge(2, -1), main(2), edge(2, 1),
                  pl.BlockSpec(bias.shape, lambda b, r, i: (0, 0, 0, 0))],
        out_specs=[pl.BlockSpec((step, wa), lambda b, r, i: (b * nq + i, r)),
                   pl.BlockSpec((step, LANES), lambda b, r, i: (b * nq + i, r))],
        out_shape=[jax.ShapeDtypeStruct((t // dilation, dilation * wa), F32),
                   jax.ShapeDtypeStruct((t // dilation, dilation * LANES), F32)],
        compiler_params=_params("arbitrary", "arbitrary", "arbitrary"),
    )(view, view, view, view, view, view, view, bias)


def _mix_kernel(*refs, dils, n_heads, eps):
    n = len(dils)
    o_refs, l_refs = list(refs[:n]), list(refs[n:2 * n])
    g_ref, out_ref = refs[2 * n], refs[2 * n + 1]
    scratch = list(refs[2 * n + 2:])
    tm, wa = out_ref.shape
    def head_reader(ref):
        return lambda h: ref[:, h * HEAD_DIM:(h + 1) * HEAD_DIM]

    heads = [head_reader(o_ref) for o_ref in o_refs]
    for p, dil in enumerate(dils):
        if dil == 1:
            continue
        o_tok, l_tok = scratch.pop(0), scratch.pop(0)
        for r in range(dil):
            rows = pl.ds(r, tm // dil, stride=dil)
            for h in range(n_heads):
                o_tok[h, rows, :] = heads[p](r * n_heads + h)
            l_tok[rows, :] = l_refs[p][:, r * LANES:(r + 1) * LANES]
        heads[p], l_refs[p] = (lambda h, ref=o_tok: ref[h]), l_tok
    ls = [l_ref[...] for l_ref in l_refs]
    top = functools.reduce(jnp.maximum, ls)
    es = [jnp.exp(l - top) for l in ls]
    tot = functools.reduce(lambda a, b: a + b, es)
    ws = [e / tot for e in es]
    for h in range(n_heads):
        cs = slice(h * HEAD_DIM, (h + 1) * HEAD_DIM)
        mix = ws[0][:, h:h + 1] * heads[0](h)
        for p in range(1, n):
            mix = mix + ws[p][:, h:h + 1] * heads[p](h)
        ms = jnp.mean(mix * mix, axis=-1, keepdims=True)
        out_ref[:, cs] = (mix * lax.rsqrt(ms + eps) * g_ref[:, cs]).astype(out_ref.dtype)


def _dilated_mixture(outs, lses, dils, gain, *, tm=512):
    wa = gain.size
    t = outs[0].shape[0] * dils[0]
    n_heads = wa // HEAD_DIM
    row = lambda i: (i, 0)
    scratch = []
    for dil in dils:
        if dil > 1:
            scratch += [pltpu.VMEM((n_heads, tm, HEAD_DIM), F32), pltpu.VMEM((tm, LANES), F32)]
    return pl.pallas_call(
        functools.partial(_mix_kernel, dils=dils, n_heads=n_heads, eps=RMS_EPS),
        grid=(t // tm,),
        in_specs=[pl.BlockSpec((tm // dil, dil * wa), row) for dil in dils]
                 + [pl.BlockSpec((tm // dil, dil * LANES), row) for dil in dils]
                 + [pl.BlockSpec((1, wa), lambda i: (0, 0))],
        out_specs=pl.BlockSpec((tm, wa), row),
        out_shape=jax.ShapeDtypeStruct((t, wa), BF16),
        scratch_shapes=scratch,
        compiler_params=_params("arbitrary"),
    )(*outs, *lses, gain.reshape(1, wa))


def _diff_attn_kernel(lam_ref, q_ref, k_ref, v_ref, bias_ref, g_ref, o_ref, s_scr, p_scr, v1_scr,
                      *, tile, q_tiles, n_tiles, eps, post_scale):
    step = pl.program_id(2)
    lane = lax.broadcasted_iota(jnp.int32, (q_tiles * tile, HEAD_DIM), 1)
    qs = q_ref[...] * (DIFF_DIM ** -0.5)
    zero = jnp.zeros_like(qs)
    q_maps = (jnp.where(lane < DIFF_DIM, qs, zero), jnp.where(lane >= DIFF_DIM, qs, zero))
    chunks = tile // LANES
    streams = [(qt, mp) for qt in range(q_tiles) for mp in range(2)]

    def fold(op, acc, vals):
        for c in range(chunks):
            acc = op(acc, vals[:, c * LANES:(c + 1) * LANES])
        return acc

    def scores(n, kt, run_max):
        qt, mp = streams[n]
        ks = slice(kt * tile, (kt + 1) * tile)
        rel_class = jnp.clip(kt - (step * q_tiles + qt), -2, 2) + 2
        q = q_maps[mp][qt * tile:(qt + 1) * tile]
        s = lax.dot_general(q, k_ref[ks, :], _NT, preferred_element_type=F32) + bias_ref[0, rel_class]
        s_scr[n % 2, :, ks] = s
        return fold(jnp.maximum, run_max, s)

    def weights(n, kt, m):
        ks = slice(kt * tile, (kt + 1) * tile)
        for rows in (slice(0, tile // 2), slice(tile // 2, tile)):
            p_scr[n % 2, rows, ks] = jnp.exp(s_scr[n % 2, rows, ks] - m[rows]).astype(BF16)

    def attend(n):
        acc = jnp.dot(p_scr[n % 2], v1_scr[...], preferred_element_type=F32)
        return acc[:, :HEAD_DIM] / acc[:, HEAD_DIM:]

    @pl.when(step == 0)
    def _():
        v1_scr[:, :HEAD_DIM] = v_ref[...]
        v1_scr[:, HEAD_DIM:] = jnp.ones((v1_scr.shape[0], HEAD_DIM), BF16)

    neg_inf = jnp.full((tile, LANES), -jnp.inf, F32)
    run_max = neg_inf
    for kt in range(n_tiles):
        run_max = scores(0, kt, run_max)
    outs = []
    for n in range(len(streams)):
        m = jnp.max(run_max, axis=-1, keepdims=True)
        run_max = neg_inf
        for kt in range(n_tiles):
            weights(n, kt, m)
            if n + 1 < len(streams):
                run_max = scores(n + 1, kt, run_max)
        outs.append(attend(n))
    for qt in range(q_tiles):
        o = outs[2 * qt] - lam_ref[0] * outs[2 * qt + 1]
        ms = jnp.mean(o * o, axis=-1, keepdims=True)
        o_ref[qt * tile:(qt + 1) * tile, :] = ((o * lax.rsqrt(ms + eps) * g_ref[0]) * post_scale).astype(o_ref.dtype)


def _diff_attention(proj, bias_table, gain, lam, lam_init, batch, seq, n_heads_a, n_heads_b, *, tile=256, q_tiles=4):
    t, pw = proj.shape
    n_tiles = seq // tile
    q0 = 3 * n_heads_a
    k0 = q0 + n_heads_b
    v0 = k0 + n_heads_b
    assert tile + 1 >= REL_MAX_DISTANCE
    rel = (jnp.arange(5)[:, None, None] - 2) * tile + jnp.arange(tile)[None, None, :] - jnp.arange(tile)[None, :, None]
    bias = _rel_bias(bias_table, rel)[n_heads_a:]
    n_steps = n_tiles // q_tiles
    rows = q_tiles * tile
    return pl.pallas_call(
        functools.partial(_diff_attn_kernel, tile=tile, q_tiles=q_tiles, n_tiles=n_tiles, eps=SUBLN_EPS,
                          post_scale=1.0 - lam_init),
        grid=(batch, n_heads_b, n_steps),
        in_specs=[pl.BlockSpec(memory_space=pltpu.SMEM),
                  pl.BlockSpec((rows, HEAD_DIM), lambda b, h, i: (b * n_steps + i, q0 + h)),
                  pl.BlockSpec((seq, HEAD_DIM), lambda b, h, i: (b, k0 + h)),
                  pl.BlockSpec((seq, HEAD_DIM), lambda b, h, i: (b, v0 + h)),
                  pl.BlockSpec((1, 5, tile, tile), lambda b, h, i: (h, 0, 0, 0)),
                  pl.BlockSpec((1, 1, HEAD_DIM), lambda b, h, i: (h, 0, 0))],
        out_specs=pl.BlockSpec((rows, HEAD_DIM), lambda b, h, i: (b * n_steps + i, h)),
        out_shape=jax.ShapeDtypeStruct((t, n_heads_b * HEAD_DIM), BF16),
        scratch_shapes=[pltpu.VMEM((2, tile, seq), F32), pltpu.VMEM((2, tile, seq), BF16),
                        pltpu.VMEM((seq, 2 * HEAD_DIM), BF16)],
        compiler_params=_params("arbitrary", "arbitrary", "arbitrary"),
    )(lam.reshape(1), proj, proj, proj, bias, gain.reshape(n_heads_b, 1, HEAD_DIM))


def _out_router_kernel(a_ref, b_ref, x_ref, wo_ref, g_ref, wrh_ref, wrl_ref,
                       x1_ref, hn_ref, route_ref, cnt_ref, *, tm, sub, wa, eps):
    @pl.when(pl.program_id(0) == 0)
    def _():
        cnt_ref[...] = jnp.zeros(cnt_ref.shape, F32)

    lane = lax.broadcasted_iota(jnp.int32, (sub, LANES), 1)
    neg_inf = jnp.float32(-jnp.inf)

    def project(rows):
        y = jnp.dot(a_ref[rows, :], wo_ref[:wa, :], preferred_element_type=F32)
        y = y + jnp.dot(b_ref[rows, :], wo_ref[wa:, :], preferred_element_type=F32)
        x1 = x_ref[rows, :] + y
        x1_ref[rows, :] = x1
        ms = jnp.mean(x1 * x1, axis=-1, keepdims=True)
        hn = x1 * lax.rsqrt(ms + eps) * g_ref[...]
        hn_ref[rows, :] = hn
        hi = hn.astype(BF16)
        lo = (hn - hi.astype(F32)).astype(BF16)
        return (jnp.dot(hi, wrh_ref[...], preferred_element_type=F32)
                + jnp.dot(lo, wrh_ref[...], preferred_element_type=F32)
                + jnp.dot(hi, wrl_ref[...], preferred_element_type=F32))

    def first_max(vals):
        top = jnp.max(vals, axis=-1, keepdims=True)
        idx = jnp.min(jnp.where(vals == top, lane, LANES), axis=-1, keepdims=True)
        return top, idx

    def route(rows, logits):
        is_grp = lane < N_GROUPS
        g_top, g_idx = first_max(jnp.where(is_grp, logits, neg_inf))
        g_sum = jnp.sum(jnp.where(is_grp, jnp.exp(logits - g_top), 0.0), axis=-1, keepdims=True)
        p_grp = 1.0 / g_sum
        e_lo = N_GROUPS + EXPERTS_PER_GROUP * g_idx
        e_vals = jnp.where((lane >= e_lo) & (lane < e_lo + EXPERTS_PER_GROUP), logits, neg_inf)
        top1, i1 = first_max(e_vals)
        top2, i2 = first_max(jnp.where(lane == i1, neg_inf, e_vals))
        b2 = jnp.exp(top2 - top1)
        w0 = p_grp / (1.0 + b2)
        w1 = p_grp * b2 / (1.0 + b2)

        oh0 = (lane == i1).astype(F32)
        oh1 = (lane == i2).astype(F32)
        both = oh0 + oh1
        r_i = lax.broadcasted_iota(jnp.int32, (sub, sub), 0)
        c_i = lax.broadcasted_iota(jnp.int32, (sub, sub), 1)
        before = jnp.where(r_i > c_i, 1.0, 0.0).astype(BF16)
        base = cnt_ref[...] + jnp.dot(before, both.astype(BF16), preferred_element_type=F32)
        rank0 = jnp.sum(base * oh0, axis=-1, keepdims=True)
        rank1 = jnp.sum(base * oh1, axis=-1, keepdims=True)
        cnt_ref[...] = cnt_ref[...] + jnp.sum(both, axis=0, keepdims=True)

        fields = {_R_E0: (i1 - N_GROUPS).astype(F32), _R_E1: (i2 - N_GROUPS).astype(F32),
                  _R_W0: w0, _R_W1: w1, _R_RANK0: rank0, _R_RANK1: rank1}
        slab = jnp.zeros((sub, LANES), F32)
        for n, f in fields.items():
            slab = jnp.where(lane == n, f, slab)
        route_ref[rows, :] = slab

    row_sets = [slice(k * sub, (k + 1) * sub) for k in range(tm // sub)]
    logits = [project(rows) for rows in row_sets]
    for rows, lg in zip(row_sets, logits):
        route(rows, lg)


def _out_router(out_a, out_b, x, w_out, layer, g, wr_hi, wr_lo, *, tm=512, sub=256):
    t, d = x.shape
    wa = out_a.shape[1]
    wb = out_b.shape[1]
    row = lambda i: (i, 0)
    const = lambda i: (0, 0)
    return pl.pallas_call(
        functools.partial(_out_router_kernel, tm=tm, sub=sub, wa=wa, eps=RMS_EPS),
        grid=(t // tm,),
        in_specs=[pl.BlockSpec((tm, wa), row), pl.BlockSpec((tm, wb), row), pl.BlockSpec((tm, d), row),
                  pl.BlockSpec((None, wa + wb, d), lambda i: (layer, 0, 0)), pl.BlockSpec((1, d), const),
                  pl.BlockSpec((d, LANES), const), pl.BlockSpec((d, LANES), const)],
        out_specs=[pl.BlockSpec((tm, d), row), pl.BlockSpec((tm, d), row),
                   pl.BlockSpec((tm, LANES), row), pl.BlockSpec((1, LANES), const)],
        out_shape=[jax.ShapeDtypeStruct((t, d), F32), jax.ShapeDtypeStruct((t, d), F32),
                   jax.ShapeDtypeStruct((t, LANES), F32), jax.ShapeDtypeStruct((1, LANES), F32)],
        compiler_params=_params("arbitrary"),
    )(out_a, out_b, x, w_out, g.reshape(1, d), wr_hi, wr_lo)


def _moe_kernel(nitems_ref, iexp_ref, icnt_ref, ilive_ref, tok_ref, nxt_ref, dst_ref, hn_hbm, wg_ref, wu_ref, wd_ref, out_hbm,
                xbuf, ybuf, gsem, ssem, *, rows, dump_rows):
    del iexp_ref
    it = pl.program_id(0)
    first_half = pl.program_id(1) == 0
    n_items = nitems_ref[0]
    cnt = icnt_ref[it]
    live = ilive_ref[it]
    buf = lax.rem(it, 2)
    tiles = rows // SUBLANES
    d = ybuf.shape[-1]

    def issue_rows(first_tile, n_tiles, copy_row):
        def body(g, carry):
            for u in range(SUBLANES):
                copy_row(first_tile + g, u).start(priority=ROW_COPY_PRIORITY)
            return carry
        lax.fori_loop(0, n_tiles, body, 0)

    def hbm_row(ref, row):
        return ref.at[lax.shift_right_logical(row, SUBLANES.bit_length() - 1), pl.ds(row & (SUBLANES - 1), 1)]

    def start_gather(idx_ref, s, live_tiles):
        issue_rows(0, live_tiles, lambda g, u: pltpu.make_async_copy(
            hbm_row(hn_hbm, idx_ref[0, 0, g * SUBLANES + u]), xbuf.at[s, g, pl.ds(u, 1)], gsem.at[s]))

    def start_scatter(j, live_tiles):
        issue_rows(j * tiles, jnp.clip(live_tiles - j * tiles, 0, tiles), lambda g, u: pltpu.make_async_copy(
            ybuf.at[g, pl.ds(u, 1)], hbm_row(out_hbm, dst_ref[0, 0, g * SUBLANES + u]), ssem.at[0]))

    def wait_tiles(n_tiles, make_tile_copy):
        def body(j, carry):
            make_tile_copy().wait()
            return carry
        lax.fori_loop(0, n_tiles, body, 0)

    def wait_gather(s, live_tiles):
        wait_tiles(live_tiles, lambda: pltpu.make_async_copy(
            hn_hbm.at[pl.ds(0, 1)], xbuf.at[s, pl.ds(0, 1)], gsem.at[s]))

    def wait_scatter(live_tiles):
        wait_tiles(live_tiles, lambda: pltpu.make_async_copy(
            ybuf.at[pl.ds(0, 1)], out_hbm.at[pl.ds(0, 1)], ssem.at[0]))

    def block_tiles(j, n):
        return pl.ds(pl.multiple_of(j * tiles, tiles), n)

    def ffn_half(j, n):
        x = xbuf[buf, block_tiles(j, n)].reshape(n * SUBLANES, d)
        gate = jnp.dot(x, wg_ref[0], preferred_element_type=F32)
        up = jnp.dot(x, wu_ref[0], preferred_element_type=F32)
        hid = gate * (1.0 / (1.0 + jnp.exp(-gate))) * up
        return jnp.dot(hid, wd_ref[0], preferred_element_type=F32).reshape(n, SUBLANES, d)

    def for_block_size(j, fn):
        short = live - j * tiles <= tiles // 2
        for is_short, n in ((False, tiles), (True, tiles // 2)):
            pl.when(short == is_short)(functools.partial(fn, n))

    @pl.when(jnp.logical_and(it < n_items, first_half))
    def _():
        @pl.when(it == 0)
        def _():
            ybuf[pl.ds(0, tiles)] = jnp.zeros((tiles, SUBLANES, d), F32)
            for start in dump_rows:
                dump = pltpu.make_async_copy(ybuf.at[pl.ds(0, tiles)],
                                             out_hbm.at[pl.ds(start // SUBLANES, tiles)], ssem.at[0])
                dump.start()
                dump.wait()

            def clear(g, carry):
                for s in range(2):
                    xbuf[s, g] = jnp.zeros((SUBLANES, d), F32)
                return carry
            lax.fori_loop(0, xbuf.shape[1], clear, 0)
            start_gather(tok_ref, buf, live)

        wait_gather(buf, live)

        def body(j, carry):
            def first_pass(n):
                y = ffn_half(j, n)

                @pl.when(jnp.logical_and(j == 0, it >= 1))
                def _():
                    wait_scatter(ilive_ref[jnp.maximum(it - 1, 0)])

                ybuf[block_tiles(j, n)] = y
            for_block_size(j, first_pass)
            return carry
        lax.fori_loop(0, cnt, body, 0)

    @pl.when(jnp.logical_and(it < n_items, jnp.logical_not(first_half)))
    def _():
        @pl.when(it + 1 < n_items)
        def _():
            start_gather(nxt_ref, 1 - buf, ilive_ref[jnp.minimum(it + 1, ilive_ref.shape[0] - 1)])

        def body(j, carry):
            def second_pass(n):
                ybuf[block_tiles(j, n)] = ybuf[block_tiles(j, n)] + ffn_half(j, n)
            for_block_size(j, second_pass)
            start_scatter(j, live)
            return carry
        lax.fori_loop(0, cnt, body, 0)

        @pl.when(it + 1 >= n_items)
        def _():
            wait_scatter(live)


def _plane_rows(n_tok):
    return n_tok + 2 * MOE_BLOCK


def _moe(hn, item_tok, item_dst, item_exp, item_cnt, item_live, n_items, w_gate, w_up, w_down, layer):
    t, d = hn.shape
    max_items = item_tok.shape[0]
    rows = MOE_BLOCK
    item_rows = MOE_GROUP * rows
    ffn = w_gate.shape[3]
    plane = _plane_rows(t)
    dump_rows = tuple(p * plane + t + s * rows for p in range(2) for s in range(2))
    half_ffn = ffn // 2

    def half(it, h, ni):
        return jnp.where(it < ni[0], h, 1)

    grid_spec = pltpu.PrefetchScalarGridSpec(
        num_scalar_prefetch=4,
        grid=(max_items, 2),
        in_specs=[
            pl.BlockSpec((1, 1, item_rows), lambda it, h, ni, ie, ic, il: (it, 0, 0), memory_space=pltpu.SMEM),
            pl.BlockSpec((1, 1, item_rows), lambda it, h, ni, ie, ic, il: (jnp.minimum(it + 1, max_items - 1), 0, 0),
                         memory_space=pltpu.SMEM),
            pl.BlockSpec((1, 1, item_rows), lambda it, h, ni, ie, ic, il: (it, 0, 0), memory_space=pltpu.SMEM),
            pl.BlockSpec(memory_space=pl.ANY),
            pl.BlockSpec((None, 1, d, half_ffn), lambda it, h, ni, ie, ic, il: (layer, ie[it], 0, half(it, h, ni))),
            pl.BlockSpec((None, 1, d, half_ffn), lambda it, h, ni, ie, ic, il: (layer, ie[it], 0, half(it, h, ni))),
            pl.BlockSpec((None, 1, half_ffn, d), lambda it, h, ni, ie, ic, il: (layer, ie[it], half(it, h, ni), 0)),
        ],
        out_specs=pl.BlockSpec(memory_space=pl.ANY),
        scratch_shapes=[pltpu.VMEM((2, item_rows // SUBLANES, SUBLANES, d), F32),
                        pltpu.VMEM((item_rows // SUBLANES, SUBLANES, d), F32),
                        pltpu.SemaphoreType.DMA((2,)), pltpu.SemaphoreType.DMA((1,))],
    )
    out = pl.pallas_call(
        functools.partial(_moe_kernel, rows=rows, dump_rows=dump_rows),
        grid_spec=grid_spec,
        out_shape=jax.ShapeDtypeStruct((2 * plane // SUBLANES, SUBLANES, d), F32),
        compiler_params=_params("arbitrary", "arbitrary", disable_bounds_checks=True),
    )(n_items, item_exp, item_cnt, item_live, item_tok, item_tok, item_dst, hn.reshape(t // SUBLANES, SUBLANES, d),
      w_gate, w_up, w_down)
    return out.reshape(2 * plane, d)


def _dispatch(route, counts, n_tok):
    rows = MOE_BLOCK
    group = MOE_GROUP
    plane = _plane_rows(n_tok)
    n_blocks = -(-(2 * n_tok) // rows) + N_EXPERTS
    max_items = N_EXPERTS + n_blocks // group
    e = route[:, _R_E0:_R_E1 + 1].astype(jnp.int32)
    rank = route[:, _R_RANK0:_R_RANK1 + 1].astype(jnp.int32)
    cnt = counts[0, N_GROUPS:N_GROUPS + N_EXPERTS].astype(jnp.int32)
    padded = (cnt + rows - 1) // rows * rows
    pend = jnp.cumsum(padded)
    pstart = pend - padded
    dest = (pstart[e] + rank).reshape(-1)
    code = (2 * jnp.arange(n_tok, dtype=jnp.int32)[:, None] + jnp.arange(2, dtype=jnp.int32)[None, :]).reshape(-1)
    row_src = jnp.full((n_blocks * rows,), -1, jnp.int32).at[dest].set(code)
    row = jnp.arange(n_blocks * rows, dtype=jnp.int32)
    row_tok = jnp.maximum(row_src, 0) // 2
    row_dst = jnp.where(row_src >= 0, (row_src % 2) * plane + row_src // 2, n_tok + row % rows)
    n_blk = padded // rows
    first_blk = pstart // rows
    items_end = jnp.cumsum((n_blk + group - 1) // group)
    items_start = items_end - (n_blk + group - 1) // group
    n_items = items_end[-1].astype(jnp.int32)
    it = jnp.arange(max_items, dtype=jnp.int32)
    item_exp = jnp.minimum(jnp.sum(items_end[None, :] <= it[:, None], axis=1), N_EXPERTS - 1).astype(jnp.int32)
    run = it - items_start[item_exp]
    item_blk = first_blk[item_exp] + run * group
    item_cnt = jnp.clip(n_blk[item_exp] - run * group, 0, group)
    live = it < n_items
    item_cnt = jnp.where(live, item_cnt, 0).astype(jnp.int32)
    item_exp = jnp.where(live, item_exp, item_exp[jnp.maximum(n_items - 1, 0)])
    blk_ids = jnp.clip(item_blk[:, None] + jnp.arange(group, dtype=jnp.int32)[None, :], 0, n_blocks - 1)
    item_tok = row_tok.reshape(n_blocks, rows)[blk_ids].reshape(max_items, 1, group * rows)
    item_dst = row_dst.reshape(n_blocks, rows)[blk_ids].reshape(max_items, 1, group * rows)
    real_rows = jnp.clip(cnt[item_exp] - run * group * rows, 0, item_cnt * rows)
    item_live = jnp.where(live, (real_rows + SUBLANES - 1) // SUBLANES, 0).astype(jnp.int32)
    return item_tok, item_dst, item_exp, item_cnt, item_live, n_items.reshape(1)


def _final_kernel(x_ref, route_ref, ya_ref, yb_ref, g_ref, o_ref, *, eps):
    x = _add_expert_outputs(x_ref[...], route_ref, ya_ref, yb_ref)
    ms = jnp.mean(x * x, axis=-1, keepdims=True)
    o_ref[...] = x * lax.rsqrt(ms + eps) * g_ref[...]


def _final_norm(x, moe, g, *, tm=512):
    t, d = x.shape
    route, planes = moe
    row = lambda i: (i, 0)
    return pl.pallas_call(
        functools.partial(_final_kernel, eps=RMS_EPS),
        grid=(t // tm,),
        in_specs=[pl.BlockSpec((tm, d), row), pl.BlockSpec((tm, LANES), row)]
                 + _expert_output_specs(tm, d, t)
                 + [pl.BlockSpec((1, d), lambda i: (0, 0))],
        out_specs=pl.BlockSpec((tm, d), row),
        out_shape=jax.ShapeDtypeStruct((t, d), F32),
        compiler_params=_params("arbitrary"),
    )(x, route, planes, planes, g.reshape(1, d))


def kernel(x, rel_bias, norm_attn, w_in, gain_a, gain_b, lam_q1, lam_k1, lam_q2, lam_k2, w_out, norm_ffn,
           w_router_group, w_router_expert, w_gate, w_up, w_down, norm_final):
    batch, seq, d = x.shape
    depth = w_in.shape[0]
    n_heads_a = gain_a.shape[1]
    n_heads_b = gain_b.shape[1]
    t = batch * seq
    xs = x.reshape(t, d)
    w_in, w_out = w_in.astype(BF16), w_out.astype(BF16)
    dils = tuple(dil for _, dil in DILATED_PATTERNS)
    extra_dils = tuple(dil for dil in dils if dil > 1)
    moe = None
    for l in range(depth):
        proj, views, xs = _norm_proj(xs, moe, norm_attn[l], w_in, l, extra_dils, 3)
        views = dict(zip(extra_dils, views))
        parts = [_dilated_attention(views.get(dil, proj), rel_bias, batch, seq, n_heads_a, w, dil)
                 for (w, dil) in DILATED_PATTERNS]
        out_a = _dilated_mixture([p[0] for p in parts], [p[1] for p in parts], dils, gain_a[l])

        lam_init = 0.8 - 0.6 * math.exp(-0.3 * l)
        lam = (jnp.exp(jnp.sum(lam_q1[l].astype(F32) * lam_k1[l].astype(F32)))
               - jnp.exp(jnp.sum(lam_q2[l].astype(F32) * lam_k2[l].astype(F32))) + lam_init)
        out_b = _diff_attention(proj, rel_bias, gain_b[l], lam, lam_init, batch, seq, n_heads_a, n_heads_b)

        w_r = jnp.concatenate([w_router_group[l], w_router_expert[l].transpose(1, 0, 2).reshape(d, N_EXPERTS)], axis=1)
        w_r = jnp.pad(w_r.astype(F32), ((0, 0), (0, LANES - w_r.shape[1])))
        wr_hi = w_r.astype(BF16)
        wr_lo = (w_r - wr_hi.astype(F32)).astype(BF16)
        xs, hn, route, counts = _out_router(out_a, out_b, xs, w_out, l, norm_ffn[l], wr_hi, wr_lo)

        planes = _moe(hn, *_dispatch(route, counts, t), w_gate, w_up, w_down, l)
        moe = (route, planes)
    out = _final_norm(xs, moe, norm_final)
    return out.reshape(batch, seq, d)
```
